```python
import math
import jax
import jax.numpy as jnp
from jax import lax
import numpy as np

D_MODEL = 1024
BATCH = 8
SEQ = 2048
DEPTH = 2
DEC_BATCH = 128
DEC_SEQ = 1
PAST_LEN = 16384
PAGE_SIZE = 128

DN_ALPHA = (2 * DEPTH) ** 0.25
DN_BETA = (8 * DEPTH) ** -0.25
LN_EPS = 1e-5
RMS_EPS = 1e-5
FFN_RES = 0.5
D_FF = 2752
N_MODS = 9

GLA_HEADS = 4
GLA_DK = 64
GLA_DV = 128
GLA_KEY = GLA_HEADS * GLA_DK
GLA_VAL = GLA_HEADS * GLA_DV
GLA_GATE_RANK = 16
GLA_GATE_NORM = 16.0
GLA_CHUNK = 32

S5_GROUP = 16
S5_WIDTH = 512
S5_GROUPS = S5_WIDTH // S5_GROUP
S5_STATE = 64
S5_DT_MIN = 1e-3
S5_DT_MAX = 1e-1

MIX_WIDTH = GLA_VAL + S5_WIDTH
IN_SPLITS = (GLA_KEY, 2 * GLA_KEY, 2 * GLA_KEY + GLA_VAL, 2 * GLA_KEY + 2 * GLA_VAL, 2 * GLA_KEY + 2 * GLA_VAL + GLA_GATE_RANK)
IN_WIDTH = 2 * GLA_KEY + 2 * GLA_VAL + GLA_GATE_RANK + S5_WIDTH

RWKV_HEAD = 64
RWKV_HEADS = D_MODEL // RWKV_HEAD
RWKV_DECAY_LORA = 64
RWKV_A_LORA = 64
RWKV_GATE_LORA = 160
RWKV_LNX_EPS = 64e-5
NORM_EPS = 1e-12

kernel_name = 'hybrid_gla_s5_rwkv7_adaln_deepnorm_step'


def layer_norm(x, g, b, eps=LN_EPS):
    xf = x.astype(jnp.float32)
    mu = jnp.mean(xf, -1, keepdims=True)
    var = jnp.mean(jnp.square(xf - mu), -1, keepdims=True)
    return ((xf - mu) * lax.rsqrt(var + eps) * g.astype(jnp.float32) + b.astype(jnp.float32)).astype(x.dtype)


def post_norm(x, r, g, b):
    return layer_norm(DN_ALPHA * x + r, g, b)


def swiglu(h, wg, wu, wd):
    return (jax.nn.silu(h @ wg) * (h @ wu)) @ wd


def ada_mods(c, w, b):
    m = jax.nn.silu(c) @ w + b
    return jnp.split(m[:, None, :], N_MODS, axis=-1)


def gla_chunked(q, k, v, gk, s0):
    f32 = jnp.float32
    B, T, H = q.shape[0], q.shape[1], q.shape[2]
    C = min(GLA_CHUNK, T)
    n = -(-T // C)
    pad = n * C - T

    def blk(t):
        t = jnp.pad(t.astype(f32), ((0, 0), (0, pad), (0, 0), (0, 0)))
        return t.reshape(B, n, C, t.shape[2], t.shape[3])

    q, k, v, gk = blk(q), blk(k), blk(v), blk(gk)
    b = jnp.cumsum(gk, axis=2)
    b_last = b[:, :, -1:]
    q_in = q * jnp.exp(b)
    k_in = k * jnp.exp(-b)
    k_st = k * jnp.exp(b_last - b)
    causal = jnp.tril(jnp.ones((C, C), dtype=bool))
    scores = jnp.where(causal, jnp.einsum('bnthd,bnshd->bnhts', q_in, k_in), 0.0)
    o_intra = jnp.einsum('bnhts,bnshv->bnthv', scores, v)
    kv = jnp.einsum('bnshd,bnshv->bnhdv', k_st, v)
    decay = jnp.exp(b_last[:, :, 0])

    def step(s, inp):
        dec, kv_n = inp
        return dec[..., None] * s + kv_n, s

    s_final, s_prev = lax.scan(step, s0.astype(f32), (jnp.moveaxis(decay, 1, 0), jnp.moveaxis(kv, 1, 0)))
    s_prev = jnp.moveaxis(s_prev, 0, 1)
    o = o_intra + jnp.einsum('bnthd,bnhdv->bnthv', q_in, s_prev)
    o = o.reshape(B, n * C, H, v.shape[-1])[:, :T]
    return o, s_final


def s5_scan(u, a_re, a_im, log_step, b_re, b_im, c_re, c_im, d, h0_re, h0_im):
    f32 = jnp.float32
    a_re, a_im = a_re.astype(f32), a_im.astype(f32)
    b_re, b_im = b_re.astype(f32), b_im.astype(f32)
    c_re, c_im = c_re.astype(f32), c_im.astype(f32)
    dt = jnp.exp(log_step.astype(f32))[:, None]
    mag = jnp.exp(a_re * dt)
    ab_re = mag * jnp.cos(a_im * dt)
    ab_im = mag * jnp.sin(a_im * dt)
    den = jnp.square(a_re) + jnp.square(a_im)
    nr = ab_re - 1.0
    z_re = (nr * a_re + ab_im * a_im) / den
    z_im = (ab_im * a_re - nr * a_im) / den
    bb_re = z_re[..., None] * b_re - z_im[..., None] * b_im
    bb_im = z_re[..., None] * b_im + z_im[..., None] * b_re
    bu_re = jnp.einsum('btgc,gpc->btgp', u, bb_re)
    bu_im = jnp.einsum('btgc,gpc->btgp', u, bb_im)
    h0_re, h0_im = h0_re.astype(f32), h0_im.astype(f32)
    bu_re = bu_re.at[:, 0].add(ab_re * h0_re - ab_im * h0_im)
    bu_im = bu_im.at[:, 0].add(ab_re * h0_im + ab_im * h0_re)
    T = u.shape[1]
    a_shape = (1, T) + ab_re.shape
    elems = (jnp.broadcast_to(ab_re, a_shape), jnp.broadcast_to(ab_im, a_shape), bu_re, bu_im)

    def combine(e1, e2):
        a1r, a1i, b1r, b1i = e1
        a2r, a2i, b2r, b2i = e2
        return (a2r * a1r - a2i * a1i, a2r * a1i + a2i * a1r,
                a2r * b1r - a2i * b1i + b2r, a2r * b1i + a2i * b1r + b2i)

    _, _, h_re, h_im = lax.associative_scan(combine, elems, axis=1)
    y = (jnp.einsum('gcp,btgp->btgc', c_re, h_re) - jnp.einsum('gcp,btgp->btgc', c_im, h_im)
         + d.astype(f32) * u)
    return y, h_re[:, -1], h_im[:, -1]


def gla_s5_mixer(h, s_gla, s_re, s_im, w_in, w_out, gla_w_gk, gla_b_gk, gla_norm_g,
                 s5_a_re, s5_a_im, s5_log_step, s5_b_re, s5_b_im, s5_c_re, s5_c_im, s5_d,
                 s5_w_glu, s5_b_glu):
    f32 = jnp.float32
    B, T, _ = h.shape
    q, k, v, g, gk_low, u = jnp.split(h @ w_in, IN_SPLITS, axis=-1)
    q = q.reshape(B, T, GLA_HEADS, GLA_DK) * (GLA_DK ** -0.5)
    k = k.reshape(B, T, GLA_HEADS, GLA_DK)
    v = v.reshape(B, T, GLA_HEADS, GLA_DV)
    gk = jax.nn.log_sigmoid((gk_low @ gla_w_gk + gla_b_gk).astype(f32)) / GLA_GATE_NORM
    o, s_gla = gla_chunked(q, k, v, gk.reshape(B, T, GLA_HEADS, GLA_DK), s_gla)
    o = o * lax.rsqrt(jnp.mean(jnp.square(o), -1, keepdims=True) + RMS_EPS) * gla_norm_g.astype(f32)
    o_gla = o.reshape(B, T, GLA_VAL).astype(h.dtype) * jax.nn.silu(g)
    y, s_re, s_im = s5_scan(u.reshape(B, T, S5_GROUPS, S5_GROUP).astype(f32), s5_a_re, s5_a_im,
                            s5_log_step, s5_b_re, s5_b_im, s5_c_re, s5_c_im, s5_d, s_re, s_im)
    z = jax.nn.gelu(y.reshape(B, T, S5_WIDTH)).astype(h.dtype)
    o_s5 = z * jax.nn.sigmoid(z @ s5_w_glu + s5_b_glu)
    out = jnp.concatenate([o_gla, o_s5], axis=-1) @ w_out
    return out, s_gla, s_re, s_im


def wkv7_scan(r, w, k, v, a, b, s0):
    def step(s, inp):
        r_t, w_t, k_t, v_t, a_t, b_t = inp
        sa = jnp.einsum('bhij,bhj->bhi', s, a_t)
        s = s * w_t[:, :, None, :] + sa[..., None] * b_t[:, :, None, :] + v_t[..., None] * k_t[:, :, None, :]
        return s, jnp.einsum('bhij,bhj->bhi', s, r_t)

    xs = tuple(jnp.moveaxis(t, 1, 0) for t in (r, w, k, v, a, b))
    s_final, y = lax.scan(step, s0, xs)
    return jnp.moveaxis(y, 0, 1), s_final


def rwkv7_mixer(h, s_shift, s_wkv, mu, w_r, w_k, w_v, w_o, w0, w1, w2, a0, a1, a2,
                g1, g2, k_k, k_a, r_k, lnx_g, lnx_b):
    f32 = jnp.float32
    B, T, D = h.shape
    H, N = RWKV_HEADS, RWKV_HEAD
    prev = jnp.concatenate([s_shift[:, None, :].astype(h.dtype), h[:, :-1]], axis=1)
    xx = prev - h
    xr, xw, xk, xv, xa, xg = [h + xx * mu[i] for i in range(6)]
    r = xr @ w_r
    k = xk @ w_k
    v = xv @ w_v
    w = -jax.nn.softplus(-(w0 + jnp.tanh(xw @ w1) @ w2).astype(f32)) - 0.5
    a = jax.nn.sigmoid((a0 + (xa @ a1) @ a2).astype(f32))
    gate = jax.nn.sigmoid(xg @ g1) @ g2

    def heads(t):
        return t.astype(f32).reshape(B, T, H, N)

    r, k, v, a, w = heads(r), heads(k), heads(v), heads(a), heads(w)
    kk = k * k_k.astype(f32).reshape(H, N)
    kk = kk / jnp.maximum(jnp.sqrt(jnp.sum(jnp.square(kk), -1, keepdims=True)), NORM_EPS)
    k = k * (1.0 + (a - 1.0) * k_a.astype(f32).reshape(H, N))
    y, s_wkv = wkv7_scan(r, jnp.exp(-jnp.exp(w)), k, v, -kk, kk * a, s_wkv.astype(f32))
    mu_y = jnp.mean(y, -1, keepdims=True)
    var_y = jnp.mean(jnp.square(y - mu_y), -1, keepdims=True)
    y = ((y - mu_y) * lax.rsqrt(var_y + RWKV_LNX_EPS) * lnx_g.astype(f32).reshape(H, N)
         + lnx_b.astype(f32).reshape(H, N))
    y = y + jnp.sum(r * k * r_k.astype(f32), -1, keepdims=True) * v
    out = (y.reshape(B, T, D).astype(h.dtype) * gate) @ w_o
    return out, h[:, -1], s_wkv


def setup_inputs(seed: int = 0) -> dict:
    key = jax.random.key(seed)
    ks = iter(jax.random.split(key, 96))
    f32 = jnp.float32

    def nrm(shape, s=1.0):
        return s * jax.random.normal(next(ks), shape, f32)

    def uni(shape, lo, hi):
        return jax.random.uniform(next(ks), shape, f32, lo, hi)

    D, P, G = D_MODEL, S5_STATE, S5_GROUPS
    inp = {}
    inp['x_prompt'] = nrm((BATCH, SEQ, D))
    inp['x_sample'] = nrm((DEC_BATCH, DEC_SEQ, D))
    inp['state_gla'] = nrm((DEC_BATCH, GLA_HEADS, GLA_DK, GLA_DV), 0.5)
    inp['state_s5_re'] = nrm((DEC_BATCH, G, P), 0.5)
    inp['state_s5_im'] = nrm((DEC_BATCH, G, P), 0.5)
    inp['state_rwkv_shift'] = nrm((DEC_BATCH, D))
    inp['state_rwkv_wkv'] = nrm((DEC_BATCH, RWKV_HEADS, RWKV_HEAD, RWKV_HEAD), 0.3)
    inp['c_prompt'] = nrm((BATCH, D))
    inp['c_sample'] = nrm((DEC_BATCH, D))
    inp['ada_w'] = nrm((DEPTH, D, N_MODS * D), 0.3 * D ** -0.5)
    inp['ada_b'] = nrm((DEPTH, N_MODS * D), 0.02)
    inp['ln_g'] = 1.0 + nrm((DEPTH, 3, D), 0.02)
    inp['ln_b'] = nrm((DEPTH, 3, D), 0.02)
    for name in ('ffn1', 'ffn2'):
        inp[name + '_wg'] = nrm((DEPTH, D, D_FF), D ** -0.5)
        inp[name + '_wu'] = nrm((DEPTH, D, D_FF), D ** -0.5)
        inp[name + '_wd'] = nrm((DEPTH, D_FF, D), DN_BETA * D_FF ** -0.5)
    inp['w_in'] = nrm((D, IN_WIDTH), D ** -0.5)
    inp['w_out'] = nrm((MIX_WIDTH, D), DN_BETA * MIX_WIDTH ** -0.5)
    inp['gla_w_gk'] = nrm((GLA_GATE_RANK, GLA_KEY), GLA_GATE_RANK ** -0.5)
    inp['gla_b_gk'] = nrm((GLA_KEY,), 0.1)
    inp['gla_norm_g'] = 1.0 + nrm((GLA_DV,), 0.02)
    inp['s5_a_re'] = -0.5 + nrm((G, P), 0.01)
    inp['s5_a_im'] = jnp.pi * jnp.arange(P, dtype=f32)[None, :] + nrm((G, P), 0.01)
    inp['s5_log_step'] = uni((G,), math.log(S5_DT_MIN), math.log(S5_DT_MAX))
    inp['s5_b_re'] = nrm((G, P, S5_GROUP), (2 * S5_GROUP) ** -0.5)
    inp['s5_b_im'] = nrm((G, P, S5_GROUP), (2 * S5_GROUP) ** -0.5)
    inp['s5_c_re'] = nrm((G, S5_GROUP, P), (2 * P) ** -0.5)
    inp['s5_c_im'] = nrm((G, S5_GROUP, P), (2 * P) ** -0.5)
    inp['s5_d'] = nrm((G, S5_GROUP))
    inp['s5_w_glu'] = nrm((S5_WIDTH, S5_WIDTH), S5_WIDTH ** -0.5)
    inp['s5_b_glu'] = nrm((S5_WIDTH,), 0.02)
    inp['rwkv_mu'] = uni((6, D), 0.0, 1.0)
    inp['rwkv_w_r'] = nrm((D, D), D ** -0.5)
    inp['rwkv_w_k'] = nrm((D, D), D ** -0.5)
    inp['rwkv_w_v'] = nrm((D, D), D ** -0.5)
    inp['rwkv_w_o'] = nrm((D, D), DN_BETA * D ** -0.5)
    inp['rwkv_w0'] = uni((D,), -6.5, -1.5)
    inp['rwkv_w1'] = nrm((D, RWKV_DECAY_LORA), D ** -0.5)
    inp['rwkv_w2'] = nrm((RWKV_DECAY_LORA, D), 0.1 * RWKV_DECAY_LORA ** -0.5)
    inp['rwkv_a0'] = nrm((D,), 0.1)
    inp['rwkv_a1'] = nrm((D, RWKV_A_LORA), D ** -0.5)
    inp['rwkv_a2'] = nrm((RWKV_A_LORA, D), 0.1 * RWKV_A_LORA ** -0.5)
    inp['rwkv_g1'] = nrm((D, RWKV_GATE_LORA), D ** -0.5)
    inp['rwkv_g2'] = nrm((RWKV_GATE_LORA, D), RWKV_GATE_LORA ** -0.5)
    inp['rwkv_k_k'] = 0.85 + nrm((D,), 0.02)
    inp['rwkv_k_a'] = 1.0 + nrm((D,), 0.02)
    inp['rwkv_r_k'] = nrm((RWKV_HEADS, RWKV_HEAD), 0.1)
    inp['rwkv_lnx_g'] = 1.0 + nrm((D,), 0.02)
    inp['rwkv_lnx_b'] = nrm((D,), 0.02)
    return inp


def reference(x_prompt, x_sample, state_gla, state_s5_re, state_s5_im, state_rwkv_shift, state_rwkv_wkv,
              c_prompt, c_sample, ada_w, ada_b, ln_g, ln_b,
              ffn1_wg, ffn1_wu, ffn1_wd, ffn2_wg, ffn2_wu, ffn2_wd,
              w_in, w_out, gla_w_gk, gla_b_gk, gla_norm_g,
              s5_a_re, s5_a_im, s5_log_step, s5_b_re, s5_b_im, s5_c_re, s5_c_im, s5_d, s5_w_glu, s5_b_glu,
              rwkv_mu, rwkv_w_r, rwkv_w_k, rwkv_w_v, rwkv_w_o, rwkv_w0, rwkv_w1, rwkv_w2,
              rwkv_a0, rwkv_a1, rwkv_a2, rwkv_g1, rwkv_g2, rwkv_k_k, rwkv_k_a, rwkv_r_k,
              rwkv_lnx_g, rwkv_lnx_b):
    f32 = jnp.float32

    def run_group(x, c, s_gla, s_re, s_im, s_shift, s_wkv):
        for layer in range(DEPTH):
            sh1, sc1, gt1, sh2, sc2, gt2, sh3, sc3, gt3 = ada_mods(c, ada_w[layer], ada_b[layer])
            y = swiglu(x * (1.0 + sc1) + sh1, ffn1_wg[layer], ffn1_wu[layer], ffn1_wd[layer])
            x = post_norm(x, FFN_RES * (1.0 + gt1) * y, ln_g[layer, 0], ln_b[layer, 0])
            h = x * (1.0 + sc2) + sh2
            if layer % 2 == 0:
                y, s_gla, s_re, s_im = gla_s5_mixer(
                    h, s_gla, s_re, s_im, w_in, w_out, gla_w_gk, gla_b_gk, gla_norm_g,
                    s5_a_re, s5_a_im, s5_log_step, s5_b_re, s5_b_im, s5_c_re, s5_c_im, s5_d,
                    s5_w_glu, s5_b_glu)
            else:
                y, s_shift, s_wkv = rwkv7_mixer(
                    h, s_shift, s_wkv, rwkv_mu, rwkv_w_r, rwkv_w_k, rwkv_w_v, rwkv_w_o,
                    rwkv_w0, rwkv_w1, rwkv_w2, rwkv_a0, rwkv_a1, rwkv_a2, rwkv_g1, rwkv_g2,
                    rwkv_k_k, rwkv_k_a, rwkv_r_k, rwkv_lnx_g, rwkv_lnx_b)
            x = post_norm(x, (1.0 + gt2) * y, ln_g[layer, 1], ln_b[layer, 1])
            y = swiglu(x * (1.0 + sc3) + sh3, ffn2_wg[layer], ffn2_wu[layer], ffn2_wd[layer])
            x = post_norm(x, FFN_RES * (1.0 + gt3) * y, ln_g[layer, 2], ln_b[layer, 2])
        return x, s_gla, s_re, s_im, s_shift, s_wkv

    bp = x_prompt.shape[0]
    y_prompt, gla_p, s5_re_p, s5_im_p, shift_p, wkv_p = run_group(
        x_prompt, c_prompt,
        jnp.zeros((bp, GLA_HEADS, GLA_DK, GLA_DV), f32),
        jnp.zeros((bp, S5_GROUPS, S5_STATE), f32),
        jnp.zeros((bp, S5_GROUPS, S5_STATE), f32),
        jnp.zeros((bp, D_MODEL), x_prompt.dtype),
        jnp.zeros((bp, RWKV_HEADS, RWKV_HEAD, RWKV_HEAD), f32))
    y_sample, gla_s, s5_re_s, s5_im_s, shift_s, wkv_s = run_group(
        x_sample, c_sample, state_gla, state_s5_re, state_s5_im, state_rwkv_shift, state_rwkv_wkv)
    return (y_prompt, y_sample, gla_p, s5_re_p, s5_im_p, shift_p, wkv_p,
            gla_s, s5_re_s, s5_im_s, shift_s, wkv_s)
```

```python
import functools
import math

import jax
import jax.numpy as jnp
from jax import lax
from jax.experimental import pallas as pl
from jax.experimental.pallas import tpu as pltpu

F32 = jnp.float32
BF16 = jnp.bfloat16

D_MODEL = 1024
DEPTH = 2
DN_ALPHA = (2 * DEPTH) ** 0.25
LN_EPS = 1e-5
RMS_EPS = 1e-5
FFN_RES = 0.5
D_FF = 2752
N_MODS = 9

GLA_HEADS = 4
GLA_DK = 64
GLA_DV = 128
GLA_KEY = GLA_HEADS * GLA_DK
GLA_VAL = GLA_HEADS * GLA_DV
GLA_GATE_RANK = 16
GLA_GATE_NORM = 16.0

S5_GROUP = 16
S5_WIDTH = 512
S5_GROUPS = S5_WIDTH // S5_GROUP
S5_STATE = 64
S5_CH = S5_GROUPS * S5_STATE

RWKV_HEAD = 64
RWKV_HEADS = D_MODEL // RWKV_HEAD
RWKV_LNX_EPS = 64e-5
NORM_EPS = 1e-12

LANES = 128
SUBLANES = 8
MXU_DIM = 256
VMEM_LIMIT = 56 * 1024 * 1024

CHUNK = 64
HEADS_PER_GROUP = MXU_DIM // RWKV_HEAD
N_GROUPS = RWKV_HEADS // HEADS_PER_GROUP
D_FF_PAD = 2816
FF_CHUNKS = ((0, 1024), (1024, 2048), (2048, D_FF_PAD))
LORA_PAD = 128
GATE_LORA_PAD = 256
PERM_STRIDE = CHUNK + SUBLANES
SAMPLE_BLOCK = 8


def _dot(a, b):
    return jnp.dot(a, b, preferred_element_type=F32)


def _dot_nt(a, b):
    return lax.dot_general(a, b, (((1,), (1,)), ((), ())), preferred_element_type=F32)


def _bdot(a, w_ref_or_val):
    return _dot(a.astype(BF16), w_ref_or_val)


def _sigmoid(x):
    return 1.0 / (1.0 + jnp.exp(-x))


def _silu(x):
    return x * _sigmoid(x)


def _softplus(x):
    return jnp.maximum(x, 0.0) + jnp.log(1.0 + jnp.exp(-jnp.abs(x)))


def _gelu_tanh(x):
    c = math.sqrt(2.0 / math.pi)
    return 0.5 * x * (1.0 + jnp.tanh(c * (x + 0.044715 * (x * x * x))))


def _layer_norm(x, g, b):
    mu = jnp.mean(x, axis=-1, keepdims=True)
    d = x - mu
    var = jnp.mean(d * d, axis=-1, keepdims=True)
    return d * lax.rsqrt(var + LN_EPS) * g + b


def _split3(x):
    hi = x.astype(BF16)
    r1 = x - hi.astype(F32)
    mid = r1.astype(BF16)
    lo = (r1 - mid.astype(F32)).astype(BF16)
    return hi, mid, lo


def _exact_dot_left01(m01, x):
    hi, mid, lo = _split3(x)
    return _dot(m01, hi) + _dot(m01, mid) + _dot(m01, lo)


def _exact_dot_right01(x, m01):
    hi, mid, lo = _split3(x)
    return _dot(hi, m01) + _dot(mid, m01) + _dot(lo, m01)


def _seg_ones(seg):
    r = lax.broadcasted_iota(jnp.int32, (LANES, LANES), 0) // seg
    c = lax.broadcasted_iota(jnp.int32, (LANES, LANES), 1) // seg
    return jnp.where(r == c, 1.0, 0.0).astype(BF16)


def _seg_sum(x, seg):
    ones = _seg_ones(seg)
    tiles = [
        _exact_dot_right01(x[:, i * LANES:(i + 1) * LANES], ones)
        for i in range(x.shape[1] // LANES)
    ]
    return jnp.concatenate(tiles, axis=1)


def _tril_ones(n):
    r = lax.broadcasted_iota(jnp.int32, (n, n), 0)
    c = lax.broadcasted_iota(jnp.int32, (n, n), 1)
    return jnp.where(r >= c, 1.0, 0.0).astype(BF16)


def _chunk_cumsum(x, n_batch):
    tri = _tril_ones(CHUNK)
    parts = [_exact_dot_left01(tri, x[b * CHUNK:(b + 1) * CHUNK, :]) for b in range(n_batch)]
    return jnp.concatenate(parts, axis=0)


def _head_stack(z, n_heads, width):
    head = lax.broadcasted_iota(jnp.int32, z.shape, 1) // width
    return jnp.concatenate([jnp.where(head == h, z, 0.0) for h in range(n_heads)], axis=0)


def _block_diag_mask(rows, cols, rblk, cblk):
    r = lax.broadcasted_iota(jnp.int32, (rows, cols), 0) // rblk
    c = lax.broadcasted_iota(jnp.int32, (rows, cols), 1) // cblk
    return r == c


def _full_spec(shape, grid_rank):
    zeros = (0,) * len(shape)
    if grid_rank == 1:
        return pl.BlockSpec(shape, lambda i: zeros, pipeline_mode=pl.Buffered(1))
    return pl.BlockSpec(shape, lambda i, j: zeros, pipeline_mode=pl.Buffered(1))


def _params(semantics):
    return pltpu.CompilerParams(dimension_semantics=semantics, vmem_limit_bytes=VMEM_LIMIT)


ADA_TN = 1152


def _ada_body(c_ref, w_ref, b_ref, o_ref):
    c = c_ref[...]
    o_ref[...] = _bdot(_silu(c), w_ref[...].astype(BF16)) + b_ref[...]


def _ada_mods(c_all, ada_w, ada_b):
    n = c_all.shape[0]
    width = N_MODS * D_MODEL
    return pl.pallas_call(
        _ada_body,
        grid=(DEPTH, width // ADA_TN),
        in_specs=[
            pl.BlockSpec((n, D_MODEL), lambda l, j: (0, 0)),
            pl.BlockSpec((None, D_MODEL, ADA_TN), lambda l, j: (l, 0, j)),
            pl.BlockSpec((None, 1, ADA_TN), lambda l, j: (l, 0, j)),
        ],
        out_specs=pl.BlockSpec((None, n, ADA_TN), lambda l, j: (l, 0, j)),
        out_shape=jax.ShapeDtypeStruct((DEPTH, n, width), F32),
        compiler_params=_params(("arbitrary", "arbitrary")),
        name="ada_mods",
    )(c_all, ada_w, ada_b.reshape(DEPTH, 1, width))


FFN_TM = 512


def _ffn_body(x_ref, sh_ref, sc_ref, gt_ref, wg_ref, wu_ref, wd_ref, g_ref, b_ref, o_ref):
    x = x_ref[...]
    h = (x * (1.0 + sc_ref[...]) + sh_ref[...]).astype(BF16)
    acc = None
    for lo, hi in FF_CHUNKS:
        g = _dot(h, wg_ref[:, lo:hi])
        u = _dot(h, wu_ref[:, lo:hi])
        y = _bdot(_silu(g) * u, wd_ref[lo:hi, :])
        acc = y if acc is None else acc + y
    r = FFN_RES * (1.0 + gt_ref[...]) * acc
    o_ref[...] = _layer_norm(DN_ALPHA * x + r, g_ref[...], b_ref[...])


def _mod_specs_prompt(layer, first):
    return [
        pl.BlockSpec((None, None, 1, D_MODEL), functools.partial(
            lambda b, i, col: (layer, b, 0, col), col=first + k))
        for k in range(3)
    ]


def _ffn_prompt(x, mods_p, layer, first, wg, wu, wd, ln_g, ln_b):
    bsz, seq, _ = x.shape
    w_specs = [_full_spec(wg.shape, 2), _full_spec(wu.shape, 2), _full_spec(wd.shape, 2),
               _full_spec((1, D_MODEL), 2), _full_spec((1, D_MODEL), 2)]
    return pl.pallas_call(
        _ffn_body,
        grid=(bsz, seq // FFN_TM),
        in_specs=[pl.BlockSpec((None, FFN_TM, D_MODEL), lambda b, i: (b, i, 0))]
        + _mod_specs_prompt(layer, first) + w_specs,
        out_specs=pl.BlockSpec((None, FFN_TM, D_MODEL), lambda b, i: (b, i, 0)),
        out_shape=jax.ShapeDtypeStruct(x.shape, F32),
        compiler_params=_params(("arbitrary", "arbitrary")),
        name="ffn_prompt",
    )(x, mods_p, mods_p, mods_p, wg, wu, wd, ln_g, ln_b)


def _mod_specs_sample(layer, first, rows):
    return [
        pl.BlockSpec((None, rows, D_MODEL), functools.partial(
            lambda i, col: (layer, 0, col), col=first + k))
        for k in range(3)
    ]


def _ffn_sample(x, mods_s, layer, first, wg, wu, wd, ln_g, ln_b):
    n = x.shape[0]
    w_specs = [_full_spec(wg.shape, 1), _full_spec(wu.shape, 1), _full_spec(wd.shape, 1),
               _full_spec((1, D_MODEL), 1), _full_spec((1, D_MODEL), 1)]
    return pl.pallas_call(
        _ffn_body,
        grid=(1,),
        in_specs=[_full_spec((n, D_MODEL), 1)] + _mod_specs_sample(layer, first, n) + w_specs,
        out_specs=_full_spec((n, D_MODEL), 1),
        out_shape=jax.ShapeDtypeStruct(x.shape, F32),
        compiler_params=_params(("arbitrary",)),
        name="ffn_sample",
    )(x, mods_s, mods_s, mods_s, wg, wu, wd, ln_g, ln_b)


def _rwkv_project(h, prev, mu_ref, wr, wk, wv, w0, w1, w2, a0, a1, a2, kk_ref, ka_ref):
    xx = prev - h
    mu = mu_ref[...]

    def mix(i):
        return h + xx * mu[i:i + 1, :]

    r = _bdot(mix(0), wr[...])
    k = _bdot(mix(2), wk[...])
    v = _bdot(mix(3), wv[...])
    wl = w0[...] + _bdot(jnp.tanh(_bdot(mix(1), w1[...])), w2[...])
    w_log = -_softplus(-wl) - 0.5
    log_decay = -jnp.exp(w_log)
    a_sig = _sigmoid(a0[...] + _bdot(_bdot(mix(4), a1[...]), a2[...]))
    kk = k * kk_ref[...]
    norm = jnp.sqrt(_seg_sum(kk * kk, RWKV_HEAD))
    kk = kk / jnp.maximum(norm, NORM_EPS)
    k2 = k * (1.0 + (a_sig - 1.0) * ka_ref[...])
    return r, log_decay, k2, v, -kk, kk * a_sig, mix(5).astype(BF16)


def _rwkv_bonus(r, k2, v, rk_ref):
    return _seg_sum(r * k2 * rk_ref[...], RWKV_HEAD) * v


def _rwkv_output(y, bonus, xg, x, gt, g1, g2, wo, lng, lnb, ln_g, ln_b):
    gate = _bdot(_sigmoid(_dot(xg, g1[...])), g2[...])
    inv_n = 1.0 / RWKV_HEAD
    mu_y = _seg_sum(y, RWKV_HEAD) * inv_n
    d = y - mu_y
    var = _seg_sum(d * d, RWKV_HEAD) * inv_n
    y = d * lax.rsqrt(var + RWKV_LNX_EPS) * lng[...] + lnb[...]
    out = _bdot((y + bonus) * gate, wo[...])
    return _layer_norm(DN_ALPHA * x + (1.0 + gt) * out, ln_g[...], ln_b[...])


def _rwkv_weight_list(p):
    return [p["mu"], p["wr"], p["wk"], p["wv"], p["w0"], p["w1"], p["w2"], p["a0"], p["a1"], p["a2"],
            p["k_k"], p["k_a"], p["r_k"]]


def _rwkv_out_weight_list(p, ln_g, ln_b):
    return [p["g1"], p["g2"], p["wo"], p["lnx_g"], p["lnx_b"], ln_g, ln_b]


NEUMANN_LEVELS = int(math.log2(CHUNK)) - 1


def _wkv_chunk(At, Rt, Kt, Bt, Kh, Bh, V, S0, decay_row):
    c = CHUNK
    hp = HEADS_PER_GROUP
    w = hp * RWKV_HEAD
    t_idx = lax.broadcasted_iota(jnp.int32, (c, w), 0)
    s_idx = lax.broadcasted_iota(jnp.int32, (c, w), 1) % c
    strict = t_idx > s_idx
    incl = t_idx >= s_idx
    eye = jnp.where(t_idx == s_idx, 1.0, 0.0)

    X2 = jnp.concatenate([At, Rt], axis=0)
    KB = jnp.concatenate([_head_stack(Kt, hp, RWKV_HEAD),
                          _head_stack(Bt, hp, RWKV_HEAD)], axis=0).astype(BF16)
    sc = _dot_nt(X2, KB)
    XS = _dot_nt(X2, S0.astype(BF16))
    n = hp * c
    Aak = jnp.where(strict, sc[:c, :n], 0.0)
    Aab = jnp.where(strict, sc[:c, n:], 0.0)
    Nrk = jnp.where(incl, sc[c:, :n], 0.0)
    Nrb = jnp.where(incl, sc[c:, n:], 0.0)

    Vbd = _head_stack(V, hp, RWKV_HEAD).astype(BF16)
    rhs = XS[:c] + _dot(Aak.astype(BF16), Vbd)

    Q = Aab
    P = eye + Aab
    Qbd = _head_stack(Q, hp, c).astype(BF16)
    for _ in range(NEUMANN_LEVELS):
        Q = _dot(Q.astype(BF16), Qbd)
        Qbd = _head_stack(Q, hp, c).astype(BF16)
        P = P + _dot(P.astype(BF16), Qbd)
    U = _dot(P.astype(BF16), _head_stack(rhs, hp, RWKV_HEAD).astype(BF16))

    Ubd = _head_stack(U, hp, RWKV_HEAD).astype(BF16)
    Y = XS[c:] + _dot(Nrk.astype(BF16), Vbd) + _dot(Nrb.astype(BF16), Ubd)

    VU = jnp.concatenate([V, U], axis=0)
    KBh = jnp.concatenate([Kh, Bh], axis=0)
    upd = _dot(VU.T.astype(BF16), KBh)
    S1 = S0 * decay_row + jnp.where(_block_diag_mask(w, w, RWKV_HEAD, RWKV_HEAD), upd, 0.0)
    return Y, S1


def _rwkv_prompt_body(x_ref, sh_ref, sc_ref, gt_ref,
                      mu, wr, wk, wv, w0, w1, w2, a0, a1, a2, kk_ref, ka_ref, rk_ref,
                      g1, g2, wo, lng, lnb, ln_g, ln_b,
                      o_ref, shift_ref, state_ref,
                      carry, s_at, s_rt, s_kt, s_bt, s_kh, s_bh, s_v, s_dec, s_y, s_bonus, s_xg):
    nb = x_ref.shape[0]
    rows = nb * CHUNK
    gw = HEADS_PER_GROUP * RWKV_HEAD
    step = pl.program_id(0)

    @pl.when(step == 0)
    def _():
        carry[...] = jnp.zeros(carry.shape, F32)
        state_ref[...] = jnp.zeros(state_ref.shape, F32)

    h3 = x_ref[...] * (1.0 + sc_ref[...]) + sh_ref[...]
    h = h3.reshape(rows, D_MODEL)
    first = lax.broadcasted_iota(jnp.int32, (rows, D_MODEL), 0) % CHUNK == 0
    carried = jnp.broadcast_to(carry[...], (nb, CHUNK, D_MODEL)).reshape(rows, D_MODEL)
    prev = jnp.where(first, carried, pltpu.roll(h, 1, axis=0))
    carry[...] = h3[:, CHUNK - 1:CHUNK, :]
    shift_ref[...] = h3[:, CHUNK - 1:CHUNK, :]

    r, lw, k2, v, av, bv, xg = _rwkv_project(
        h, prev, mu, wr, wk, wv, w0, w1, w2, a0, a1, a2, kk_ref, ka_ref)
    s_xg[...] = xg
    s_bonus[...] = _rwkv_bonus(r, k2, v, rk_ref)

    g = _chunk_cumsum(lw, nb)
    g3 = g.reshape(nb, CHUNK, D_MODEL)
    g_last = jnp.broadcast_to(g3[:, CHUNK - 1:CHUNK, :], (nb, CHUNK, D_MODEL)).reshape(rows, D_MODEL)
    e_neg = jnp.exp(-g)
    e_tail = jnp.exp(g_last - g)
    arrays = (
        (s_at, av * jnp.exp(g - lw)),
        (s_rt, r * jnp.exp(g)),
        (s_kt, k2 * e_neg),
        (s_bt, bv * e_neg),
        (s_kh, k2 * e_tail),
        (s_bh, bv * e_tail),
        (s_v, v),
    )
    for ref, val in arrays:
        for gi in range(N_GROUPS):
            ref[gi] = val[:, gi * gw:(gi + 1) * gw].astype(ref.dtype)
    dec = jnp.exp(g3[:, CHUNK - 1:CHUNK, :])
    for gi in range(N_GROUPS):
        s_dec[gi] = dec[:, :, gi * gw:(gi + 1) * gw]

    def chunk_step(i, carry_val):
        b = i // N_GROUPS
        gi = i % N_GROUPS
        row = pl.multiple_of(b * CHUNK, CHUNK)
        sl = pl.ds(row, CHUNK)
        Y, S1 = _wkv_chunk(s_at[gi, sl, :], s_rt[gi, sl, :], s_kt[gi, sl, :], s_bt[gi, sl, :],
                           s_kh[gi, sl, :], s_bh[gi, sl, :], s_v[gi, sl, :],
                           state_ref[b, gi], s_dec[gi, b])
        s_y[gi, sl, :] = Y
        state_ref[b, gi] = S1
        return carry_val

    lax.fori_loop(0, nb * N_GROUPS, chunk_step, 0)

    y = jnp.concatenate([s_y[gi] for gi in range(N_GROUPS)], axis=1)
    gt = jnp.broadcast_to(gt_ref[...], (nb, CHUNK, D_MODEL)).reshape(rows, D_MODEL)
    x = x_ref[...].reshape(rows, D_MODEL)
    out = _rwkv_output(y, s_bonus[...], s_xg[...], x, gt, g1, g2, wo, lng, lnb, ln_g, ln_b)
    o_ref[...] = out.reshape(nb, CHUNK, D_MODEL)


def _rwkv_prompt(x, mods_p, layer, p, ln_g, ln_b):
    bsz, seq, _ = x.shape
    rows = bsz * CHUNK
    gw = HEADS_PER_GROUP * RWKV_HEAD
    weights = _rwkv_weight_list(p) + _rwkv_out_weight_list(p, ln_g, ln_b)
    mod_specs = [
        pl.BlockSpec((None, bsz, 1, D_MODEL), functools.partial(lambda i, col: (layer, 0, 0, col), col=3 + k))
        for k in range(3)
    ]
    slab = pltpu.VMEM((N_GROUPS, rows, gw), F32)
    slab16 = pltpu.VMEM((N_GROUPS, rows, gw), BF16)
    out, shift, state = pl.pallas_call(
        _rwkv_prompt_body,
        grid=(seq // CHUNK,),
        in_specs=[pl.BlockSpec((bsz, CHUNK, D_MODEL), lambda i: (0, i, 0))] + mod_specs
        + [_full_spec(w.shape, 1) for w in weights],
        out_specs=[
            pl.BlockSpec((bsz, CHUNK, D_MODEL), lambda i: (0, i, 0)),
            _full_spec((bsz, 1, D_MODEL), 1),
            _full_spec((bsz, N_GROUPS, gw, gw), 1),
        ],
        out_shape=[
            jax.ShapeDtypeStruct(x.shape, F32),
            jax.ShapeDtypeStruct((bsz, 1, D_MODEL), F32),
            jax.ShapeDtypeStruct((bsz, N_GROUPS, gw, gw), F32),
        ],
        scratch_shapes=[pltpu.VMEM((bsz, 1, D_MODEL), F32), slab16, slab16, slab, slab, slab16, slab16, slab]
        + [pltpu.VMEM((N_GROUPS, bsz, 1, gw), F32), slab,
           pltpu.VMEM((rows, D_MODEL), F32), pltpu.VMEM((rows, D_MODEL), BF16)],
        compiler_params=_params(("arbitrary",)),
        name="rwkv_prompt",
    )(x, mods_p, mods_p, mods_p, *weights)
    s6 = state.reshape(bsz, N_GROUPS, HEADS_PER_GROUP, RWKV_HEAD, HEADS_PER_GROUP, RWKV_HEAD)
    wkv = jnp.stack([s6[:, :, hh, :, hh, :] for hh in range(HEADS_PER_GROUP)], axis=2)
    return out, shift.reshape(bsz, D_MODEL), wkv.reshape(bsz, RWKV_HEADS, RWKV_HEAD, RWKV_HEAD)


def _rwkv_sample_pre_body(x_ref, sh_ref, sc_ref, prev_ref,
                          mu, wr, wk, wv, w0, w1, w2, a0, a1, a2, kk_ref, ka_ref, rk_ref,
                          h_ref, r_ref, w_ref, k_ref, v_ref, a_ref, b_ref, xg_ref, bonus_ref):
    h = x_ref[...] * (1.0 + sc_ref[...]) + sh_ref[...]
    r, lw, k2, v, av, bv, xg = _rwkv_project(
        h, prev_ref[...], mu, wr, wk, wv, w0, w1, w2, a0, a1, a2, kk_ref, ka_ref)
    h_ref[...] = h
    r_ref[...] = r
    w_ref[...] = jnp.exp(lw)
    k_ref[...] = k2
    v_ref[...] = v
    a_ref[...] = av
    b_ref[...] = bv
    xg_ref[...] = xg
    bonus_ref[...] = _rwkv_bonus(r, k2, v, rk_ref)


def _rwkv_sample_step_body(s_ref, vcol_ref, r_ref, w_ref, k_ref, a_ref, b_ref, so_ref, ycol_ref):
    n = RWKV_HEAD
    for j in range(SAMPLE_BLOCK):
        for hh in range(RWKV_HEADS):
            lanes = slice(hh * n, (hh + 1) * n)
            S = s_ref[j, hh]
            vcol = vcol_ref[lanes, j:j + 1]
            sa = jnp.sum(S * a_ref[j:j + 1, lanes], axis=1, keepdims=True)
            S1 = S * w_ref[j:j + 1, lanes] + sa * b_ref[j:j + 1, lanes] + vcol * k_ref[j:j + 1, lanes]
            so_ref[j, hh] = S1
            ycol_ref[lanes, j:j + 1] = jnp.sum(S1 * r_ref[j:j + 1, lanes], axis=1, keepdims=True)


def _rwkv_sample_post_body(y_ref, bonus_ref, xg_ref, x_ref, gt_ref,
                           g1, g2, wo, lng, lnb, ln_g, ln_b, o_ref):
    o_ref[...] = _rwkv_output(y_ref[...], bonus_ref[...], xg_ref[...], x_ref[...], gt_ref[...],
                              g1, g2, wo, lng, lnb, ln_g, ln_b)


def _to_columns(a):
    n, f = a.shape
    return a.reshape(n // SAMPLE_BLOCK, SAMPLE_BLOCK, f).transpose(0, 2, 1)


def _from_columns(a):
    nb, f, blk = a.shape
    return a.transpose(0, 2, 1).reshape(nb * blk, f)


def _rwkv_sample(x, mods_s, layer, s_shift, s_wkv, p, ln_g, ln_b):
    n = x.shape[0]
    row = jax.ShapeDtypeStruct((n, D_MODEL), F32)
    row_spec = _full_spec((n, D_MODEL), 1)
    pre_w = _rwkv_weight_list(p)
    ms = _mod_specs_sample(layer, 3, n)
    h, r, w, k2, v, av, bv, xg, bonus = pl.pallas_call(
        _rwkv_sample_pre_body,
        grid=(1,),
        in_specs=[row_spec, ms[0], ms[1], row_spec] + [_full_spec(a.shape, 1) for a in pre_w],
        out_specs=[row_spec] * 9,
        out_shape=[row] * 7 + [jax.ShapeDtypeStruct((n, D_MODEL), BF16), row],
        compiler_params=_params(("arbitrary",)),
        name="rwkv_sample_pre",
    )(x, mods_s, mods_s, s_shift, *pre_w)

    nblk = n // SAMPLE_BLOCK
    blk_row = pl.BlockSpec((SAMPLE_BLOCK, D_MODEL), lambda i: (i, 0))
    st_spec = pl.BlockSpec((SAMPLE_BLOCK, RWKV_HEADS, RWKV_HEAD, RWKV_HEAD), lambda i: (i, 0, 0, 0))
    col_spec = pl.BlockSpec((None, D_MODEL, SAMPLE_BLOCK), lambda i: (i, 0, 0))
    s_new, ycol = pl.pallas_call(
        _rwkv_sample_step_body,
        grid=(nblk,),
        in_specs=[st_spec, col_spec] + [blk_row] * 5,
        out_specs=[st_spec, col_spec],
        out_shape=[jax.ShapeDtypeStruct(s_wkv.shape, F32),
                   jax.ShapeDtypeStruct((nblk, D_MODEL, SAMPLE_BLOCK), F32)],
        compiler_params=_params(("arbitrary",)),
        name="rwkv_sample_step",
    )(s_wkv, _to_columns(v), r, w, k2, av, bv)

    post_w = _rwkv_out_weight_list(p, ln_g, ln_b)
    out = pl.pallas_call(
        _rwkv_sample_post_body,
        grid=(1,),
        in_specs=[row_spec] * 4 + [ms[2]] + [_full_spec(a.shape, 1) for a in post_w],
        out_specs=row_spec,
        out_shape=row,
        compiler_params=_params(("arbitrary",)),
        name="rwkv_sample_post",
    )(_from_columns(ycol), bonus, xg, x, mods_s, *post_w)
    return out, h, s_new


def _s5_prep_body(are_ref, aim_ref, ls_ref, bre_ref, bim_ref, abre_ref, abim_ref, bbre_ref, bbim_ref):
    a_re = are_ref[...]
    a_im = aim_ref[...]
    dt = jnp.exp(ls_ref[...])
    mag = jnp.exp(a_re * dt)
    ab_re = mag * jnp.cos(a_im * dt)
    ab_im = mag * jnp.sin(a_im * dt)
    den = a_re * a_re + a_im * a_im
    nr = ab_re - 1.0
    z_re = (nr * a_re + ab_im * a_im) / den
    z_im = (ab_im * a_re - nr * a_im) / den
    b_re = bre_ref[...]
    b_im = bim_ref[...]
    abre_ref[...] = ab_re
    abim_ref[...] = ab_im
    bbre_ref[...] = z_re * b_re - z_im * b_im
    bbim_ref[...] = z_re * b_im + z_im * b_re


def _s5_prepare(s5_a_re, s5_a_im, s5_log_step, s5_b_re, s5_b_im, s5_c_re, s5_c_im, s5_d):
    G, P, C = S5_GROUPS, S5_STATE, S5_GROUP
    small = jax.ShapeDtypeStruct((G, 1, P), F32)
    big = jax.ShapeDtypeStruct((G, C, P), F32)
    ls = jnp.broadcast_to(s5_log_step.reshape(G, 1, 1), (G, 1, P))
    ab_re, ab_im, bb_re, bb_im = pl.pallas_call(
        _s5_prep_body,
        grid=(1,),
        in_specs=[_full_spec((G, 1, P), 1)] * 3 + [_full_spec((G, C, P), 1)] * 2,
        out_specs=[_full_spec((G, 1, P), 1)] * 2 + [_full_spec((G, C, P), 1)] * 2,
        out_shape=[small, small, big, big],
        name="s5_prepare",
    )(s5_a_re.reshape(G, 1, P), s5_a_im.reshape(G, 1, P), ls,
      s5_b_re.transpose(0, 2, 1), s5_b_im.transpose(0, 2, 1))

    eye = jnp.eye(G, dtype=F32)
    gpt = LANES // C
    n_tiles = G // gpt

    def in_blocks(bb):
        full = (bb[:, :, None, :] * eye[:, None, :, None]).reshape(G * C, G * P)
        return jnp.stack([full[k * LANES:(k + 1) * LANES, k * gpt * P:(k + 1) * gpt * P]
                          for k in range(n_tiles)]).astype(BF16)

    def out_blocks(cc):
        full = (cc.transpose(0, 2, 1)[:, :, None, :] * eye[:, None, :, None]).reshape(G * P, G * C)
        return jnp.stack([full[k * gpt * P:(k + 1) * gpt * P, k * LANES:(k + 1) * LANES]
                          for k in range(n_tiles)]).astype(BF16)

    return dict(
        ab_re=ab_re.reshape(1, G * P), ab_im=ab_im.reshape(1, G * P),
        bb_re=in_blocks(bb_re), bb_im=in_blocks(bb_im),
        c_re=out_blocks(s5_c_re), c_im=out_blocks(s5_c_im),
        d=s5_d.reshape(1, G * C),
    )


def _s5_input(u, bb_re_ref, bb_im_ref):
    n_tiles = bb_re_ref.shape[0]
    ub = u.astype(BF16)
    re = [_dot(ub[:, k * LANES:(k + 1) * LANES], bb_re_ref[k]) for k in range(n_tiles)]
    im = [_dot(ub[:, k * LANES:(k + 1) * LANES], bb_im_ref[k]) for k in range(n_tiles)]
    return jnp.concatenate(re, axis=1), jnp.concatenate(im, axis=1)


def _s5_output(h_re, h_im, c_re_ref, c_im_ref):
    n_tiles = c_re_ref.shape[0]
    w = h_re.shape[1] // n_tiles
    hr = h_re.astype(BF16)
    hi = h_im.astype(BF16)
    ys = [_dot(hr[:, k * w:(k + 1) * w], c_re_ref[k]) - _dot(hi[:, k * w:(k + 1) * w], c_im_ref[k])
          for k in range(n_tiles)]
    return jnp.concatenate(ys, axis=1)


C_Q, C_K, C_V, C_G, C_U, C_GK, C_END = 0, 256, 512, 1024, 1536, 2048, 2176


def _mix_project(h, w_in, w_gk, b_gk):
    p = _bdot(h, w_in[...])
    q = p[:, C_Q:C_K] * (GLA_DK ** -0.5)
    k = p[:, C_K:C_V]
    v = p[:, C_V:C_G]
    gg = p[:, C_G:C_U]
    u = p[:, C_U:C_GK]
    z = _bdot(p[:, C_GK:C_END], w_gk[...]) + b_gk[...]
    gk = -_softplus(-z) * (1.0 / GLA_GATE_NORM)
    return q, k, v, gg, u, gk


def _gla_finish(o, gg, norm_g):
    parts = []
    for hh in range(GLA_HEADS):
        oh = o[:, hh * GLA_DV:(hh + 1) * GLA_DV]
        parts.append(oh * lax.rsqrt(jnp.mean(oh * oh, axis=-1, keepdims=True) + RMS_EPS) * norm_g[...])
    return jnp.concatenate(parts, axis=1) * _silu(gg)


def _mix_output(o_gla, y_s5, x, gt, w_glu, b_glu, w_out, ln_g, ln_b):
    z = _gelu_tanh(y_s5)
    o_s5 = z * _sigmoid(_bdot(z, w_glu[...]) + b_glu[...])
    out = _bdot(o_gla, w_out[0:GLA_VAL, :]) + _bdot(o_s5, w_out[GLA_VAL:GLA_VAL + S5_WIDTH, :])
    return _layer_norm(DN_ALPHA * x + (1.0 + gt) * out, ln_g[...], ln_b[...])


def _gla_chunk(q, k, v, bc, ST):
    c = CHUNK
    mid = c // 2 - 1
    b_mid = bc[mid:mid + 1, :]
    b_last = bc[c - 1:c, :]
    q_in = q * jnp.exp(bc - b_mid)
    k_in = k * jnp.exp(b_mid - bc)
    q_full = q * jnp.exp(bc)
    k_st = k * jnp.exp(b_last - bc)
    t_idx = lax.broadcasted_iota(jnp.int32, (c, GLA_KEY), 0)
    s_idx = lax.broadcasted_iota(jnp.int32, (c, GLA_KEY), 1) % c
    scores = _dot_nt(q_in.astype(BF16), _head_stack(k_in, GLA_HEADS, GLA_DK).astype(BF16))
    scores = jnp.where(t_idx >= s_idx, scores, 0.0)
    o = _dot(scores.astype(BF16), _head_stack(v, GLA_HEADS, GLA_DV).astype(BF16))
    o = o + _dot_nt(q_full.astype(BF16), ST.astype(BF16))
    upd = _dot(v.T.astype(BF16), k_st.astype(BF16))
    ST1 = ST * jnp.exp(b_last) + jnp.where(
        _block_diag_mask(GLA_VAL, GLA_KEY, GLA_DV, GLA_DK), upd, 0.0)
    return o, ST1


def _mix_prompt_body(x_ref, sh_ref, sc_ref, gt_ref,
                     w_in, w_gk, b_gk, norm_g, ab_re_ref, ab_im_ref, bb_re, bb_im, c_re, c_im, d_ref,
                     w_glu, b_glu, w_out, ln_g, ln_b,
                     o_ref, gla_ref, hre_ref, him_ref,
                     s_q, s_k, s_v, s_bc, s_o, s_perm, s_re, s_im):
    nb = x_ref.shape[0]
    rows = nb * CHUNK
    step = pl.program_id(0)

    @pl.when(step == 0)
    def _():
        gla_ref[...] = jnp.zeros(gla_ref.shape, F32)
        hre_ref[...] = jnp.zeros(hre_ref.shape, F32)
        him_ref[...] = jnp.zeros(him_ref.shape, F32)

    x3 = x_ref[...]
    h3 = x3 * (1.0 + sc_ref[...]) + sh_ref[...]
    x = x3.reshape(rows, D_MODEL)
    h = h3.reshape(rows, D_MODEL)
    q, k, v, gg, u, gk = _mix_project(h, w_in, w_gk, b_gk)

    s_q[...] = q
    s_k[...] = k
    s_v[...] = v
    s_bc[...] = _chunk_cumsum(gk, nb)

    def gla_step(b, carry_val):
        sl = pl.ds(pl.multiple_of(b * CHUNK, CHUNK), CHUNK)
        o, ST1 = _gla_chunk(s_q[sl, :], s_k[sl, :], s_v[sl, :], s_bc[sl, :], gla_ref[b])
        s_o[sl, :] = o
        gla_ref[b] = ST1
        return carry_val

    lax.fori_loop(0, nb, gla_step, 0)
    o_gla = _gla_finish(s_o[...], gg, norm_g)

    n_lt = S5_WIDTH // LANES
    for b in range(nb):
        for lt in range(n_lt):
            s_perm[lt, b * PERM_STRIDE:b * PERM_STRIDE + CHUNK, :] = (
                u[b * CHUNK:(b + 1) * CHUNK, lt * LANES:(lt + 1) * LANES])
    u_tb = jnp.concatenate(
        [jnp.concatenate([s_perm[lt, pl.ds(t, nb, stride=PERM_STRIDE), :] for lt in range(n_lt)], axis=1)
         for t in range(CHUNK)], axis=0)
    bu_re, bu_im = _s5_input(u_tb, bb_re, bb_im)
    s_re[...] = bu_re
    s_im[...] = bu_im

    n_ch = 4
    cw = S5_CH // n_ch
    for ci in range(n_ch):
        lanes = slice(ci * cw, (ci + 1) * cw)
        a_re = jnp.broadcast_to(ab_re_ref[:, lanes], (nb, cw))
        a_im = jnp.broadcast_to(ab_im_ref[:, lanes], (nb, cw))

        def scan_step(t, hc, lanes=lanes, a_re=a_re, a_im=a_im):
            h_re, h_im = hc
            sl = pl.ds(pl.multiple_of(t * nb, nb), nb)
            n_re = a_re * h_re - a_im * h_im + s_re[sl, lanes]
            n_im = a_re * h_im + a_im * h_re + s_im[sl, lanes]
            s_re[sl, lanes] = n_re
            s_im[sl, lanes] = n_im
            return n_re, n_im

        f_re, f_im = lax.fori_loop(0, CHUNK, scan_step, (hre_ref[:, lanes], him_ref[:, lanes]))
        hre_ref[:, lanes] = f_re
        him_ref[:, lanes] = f_im

    y_tb = _s5_output(s_re[...], s_im[...], c_re, c_im)
    for t in range(CHUNK):
        for lt in range(n_lt):
            s_perm[lt, pl.ds(t, nb, stride=PERM_STRIDE), :] = (
                y_tb[t * nb:(t + 1) * nb, lt * LANES:(lt + 1) * LANES])
    y_s5 = jnp.concatenate(
        [jnp.concatenate([s_perm[lt, b * PERM_STRIDE:b * PERM_STRIDE + CHUNK, :] for lt in range(n_lt)],
                         axis=1) for b in range(nb)], axis=0)
    y_s5 = y_s5 + d_ref[...] * u

    gt = jnp.broadcast_to(gt_ref[...], (nb, CHUNK, D_MODEL)).reshape(rows, D_MODEL)
    out = _mix_output(o_gla, y_s5, x, gt, w_glu, b_glu, w_out, ln_g, ln_b)
    o_ref[...] = out.reshape(nb, CHUNK, D_MODEL)


def _mix_weight_list(m, s5):
    return [m["w_in"], m["w_gk"], m["b_gk"], m["norm_g"], s5["ab_re"], s5["ab_im"], s5["bb_re"], s5["bb_im"],
            s5["c_re"], s5["c_im"], s5["d"], m["w_glu"], m["b_glu"], m["w_out"]]


def _mix_prompt(x, mods_p, layer, m, s5, ln_g, ln_b):
    bsz, seq, _ = x.shape
    rows = bsz * CHUNK
    weights = _mix_weight_list(m, s5) + [ln_g, ln_b]
    mod_specs = [
        pl.BlockSpec((None, bsz, 1, D_MODEL), functools.partial(lambda i, col: (layer, 0, 0, col), col=3 + k))
        for k in range(3)
    ]
    out, gla_t, h_re, h_im = pl.pallas_call(
        _mix_prompt_body,
        grid=(seq // CHUNK,),
        in_specs=[pl.BlockSpec((bsz, CHUNK, D_MODEL), lambda i: (0, i, 0))] + mod_specs
        + [_full_spec(w.shape, 1) for w in weights],
        out_specs=[
            pl.BlockSpec((bsz, CHUNK, D_MODEL), lambda i: (0, i, 0)),
            _full_spec((bsz, GLA_VAL, GLA_KEY), 1),
            _full_spec((bsz, S5_CH), 1),
            _full_spec((bsz, S5_CH), 1),
        ],
        out_shape=[
            jax.ShapeDtypeStruct(x.shape, F32),
            jax.ShapeDtypeStruct((bsz, GLA_VAL, GLA_KEY), F32),
            jax.ShapeDtypeStruct((bsz, S5_CH), F32),
            jax.ShapeDtypeStruct((bsz, S5_CH), F32),
        ],
        scratch_shapes=[
            pltpu.VMEM((rows, GLA_KEY), F32), pltpu.VMEM((rows, GLA_KEY), F32),
            pltpu.VMEM((rows, GLA_VAL), F32), pltpu.VMEM((rows, GLA_KEY), F32),
            pltpu.VMEM((rows, GLA_VAL), F32),
            pltpu.VMEM((S5_WIDTH // LANES, bsz * PERM_STRIDE, LANES), F32),
            pltpu.VMEM((rows, S5_CH), F32), pltpu.VMEM((rows, S5_CH), F32),
        ],
        compiler_params=_params(("arbitrary",)),
        name="gla_s5_prompt",
    )(x, mods_p, mods_p, mods_p, *weights)
    s6 = gla_t.reshape(bsz, GLA_HEADS, GLA_DV, GLA_HEADS, GLA_DK)
    gla = jnp.stack([s6[:, hh, :, hh, :] for hh in range(GLA_HEADS)], axis=1).transpose(0, 1, 3, 2)
    return (out, gla, h_re.reshape(bsz, S5_GROUPS, S5_STATE), h_im.reshape(bsz, S5_GROUPS, S5_STATE))


def _mix_sample_pre_body(x_ref, sh_ref, sc_ref, hre_ref, him_ref,
                         w_in, w_gk, b_gk, ab_re_ref, ab_im_ref, bb_re, bb_im, c_re, c_im, d_ref,
                         q_ref, k_ref, v_ref, gg_ref, dec_ref, y_ref, nre_ref, nim_ref):
    h = x_ref[...] * (1.0 + sc_ref[...]) + sh_ref[...]
    q, k, v, gg, u, gk = _mix_project(h, w_in, w_gk, b_gk)
    q_ref[...] = q
    k_ref[...] = k
    v_ref[...] = v
    gg_ref[...] = gg
    dec_ref[...] = jnp.exp(gk)
    bu_re, bu_im = _s5_input(u, bb_re, bb_im)
    a_re = ab_re_ref[...]
    a_im = ab_im_ref[...]
    h_re = hre_ref[...]
    h_im = him_ref[...]
    n_re = a_re * h_re - a_im * h_im + bu_re
    n_im = a_re * h_im + a_im * h_re + bu_im
    nre_ref[...] = n_re
    nim_ref[...] = n_im
    y_ref[...] = _s5_output(n_re, n_im, c_re, c_im) + d_ref[...] * u


def _gla_sample_step_body(s_ref, qc_ref, kc_ref, dc_ref, v_ref, so_ref, o_ref):
    for j in range(SAMPLE_BLOCK):
        for hh in range(GLA_HEADS):
            rows = slice(hh * GLA_DK, (hh + 1) * GLA_DK)
            lanes = slice(hh * GLA_DV, (hh + 1) * GLA_DV)
            S1 = dc_ref[rows, j:j + 1] * s_ref[j, hh] + kc_ref[rows, j:j + 1] * v_ref[j:j + 1, lanes]
            so_ref[j, hh] = S1
            o_ref[j:j + 1, lanes] = jnp.sum(qc_ref[rows, j:j + 1] * S1, axis=0, keepdims=True)


def _mix_sample_post_body(o_ref_in, gg_ref, y_ref, x_ref, gt_ref, norm_g, w_glu, b_glu, w_out, ln_g, ln_b,
                          o_ref):
    o_gla = _gla_finish(o_ref_in[...], gg_ref[...], norm_g)
    o_ref[...] = _mix_output(o_gla, y_ref[...], x_ref[...], gt_ref[...], w_glu, b_glu, w_out, ln_g, ln_b)


def _mix_sample(x, mods_s, layer, s_gla, s_re, s_im, m, s5, ln_g, ln_b):
    n = x.shape[0]
    ms = _mod_specs_sample(layer, 3, n)
    row_spec = _full_spec((n, D_MODEL), 1)

    def sds(w):
        return jax.ShapeDtypeStruct((n, w), F32)

    def fs(w):
        return _full_spec((n, w), 1)

    pre_w = [m["w_in"], m["w_gk"], m["b_gk"], s5["ab_re"], s5["ab_im"], s5["bb_re"], s5["bb_im"],
             s5["c_re"], s5["c_im"], s5["d"]]
    widths = [GLA_KEY, GLA_KEY, GLA_VAL, GLA_VAL, GLA_KEY, S5_WIDTH, S5_CH, S5_CH]
    q, k, v, gg, dec, y_s5, n_re, n_im = pl.pallas_call(
        _mix_sample_pre_body,
        grid=(1,),
        in_specs=[row_spec, ms[0], ms[1], fs(S5_CH), fs(S5_CH)] + [_full_spec(a.shape, 1) for a in pre_w],
        out_specs=[fs(w) for w in widths],
        out_shape=[sds(w) for w in widths],
        compiler_params=_params(("arbitrary",)),
        name="gla_s5_sample_pre",
    )(x, mods_s, mods_s, s_re.reshape(n, S5_CH), s_im.reshape(n, S5_CH), *pre_w)

    nblk = n // SAMPLE_BLOCK
    st_spec = pl.BlockSpec((SAMPLE_BLOCK, GLA_HEADS, GLA_DK, GLA_DV), lambda i: (i, 0, 0, 0))
    col_spec = pl.BlockSpec((None, GLA_KEY, SAMPLE_BLOCK), lambda i: (i, 0, 0))
    v_spec = pl.BlockSpec((SAMPLE_BLOCK, GLA_VAL), lambda i: (i, 0))
    s_new, o = pl.pallas_call(
        _gla_sample_step_body,
        grid=(nblk,),
        in_specs=[st_spec, col_spec, col_spec, col_spec, v_spec],
        out_specs=[st_spec, v_spec],
        out_shape=[jax.ShapeDtypeStruct(s_gla.shape, F32), sds(GLA_VAL)],
        compiler_params=_params(("arbitrary",)),
        name="gla_sample_step",
    )(s_gla, _to_columns(q), _to_columns(k), _to_columns(dec), v)

    post_w = [m["norm_g"], m["w_glu"], m["b_glu"], m["w_out"], ln_g, ln_b]
    out = pl.pallas_call(
        _mix_sample_post_body,
        grid=(1,),
        in_specs=[fs(GLA_VAL), fs(GLA_VAL), fs(S5_WIDTH), row_spec, ms[2]]
        + [_full_spec(a.shape, 1) for a in post_w],
        out_specs=row_spec,
        out_shape=sds(D_MODEL),
        compiler_params=_params(("arbitrary",)),
        name="gla_s5_sample_post",
    )(o, gg, y_s5, x, mods_s, *post_w)
    return (out, s_new, n_re.reshape(n, S5_GROUPS, S5_STATE), n_im.reshape(n, S5_GROUPS, S5_STATE))


def _pad_to(a, axis, size):
    pad = [(0, 0)] * a.ndim
    pad[axis] = (0, size - a.shape[axis])
    return jnp.pad(a, pad)


def _row(a):
    return a.reshape(1, -1).astype(F32)


def _prep_ffn(wg, wu, wd):
    return (_pad_to(wg, 1, D_FF_PAD).astype(BF16), _pad_to(wu, 1, D_FF_PAD).astype(BF16),
            _pad_to(wd, 0, D_FF_PAD).astype(BF16))


def _prep_mix(w_in, w_out, gla_w_gk, gla_b_gk, gla_norm_g, s5_w_glu, s5_b_glu):
    q, k, v, g, gk_low, u = jnp.split(
        w_in, (GLA_KEY, 2 * GLA_KEY, 2 * GLA_KEY + GLA_VAL, 2 * GLA_KEY + 2 * GLA_VAL,
               2 * GLA_KEY + 2 * GLA_VAL + GLA_GATE_RANK), axis=1)
    w_in_r = jnp.concatenate([q, k, v, g, u, _pad_to(gk_low, 1, LANES)], axis=1)
    return dict(
        w_in=w_in_r.astype(BF16), w_out=w_out.astype(BF16),
        w_gk=_pad_to(gla_w_gk, 0, LANES).astype(BF16), b_gk=_row(gla_b_gk), norm_g=_row(gla_norm_g),
        w_glu=s5_w_glu.astype(BF16), b_glu=_row(s5_b_glu),
    )


def _prep_rwkv(mu, w_r, w_k, w_v, w_o, w0, w1, w2, a0, a1, a2, g1, g2, k_k, k_a, r_k, lnx_g, lnx_b):
    return dict(
        mu=_pad_to(mu, 0, SUBLANES), wr=w_r.astype(BF16), wk=w_k.astype(BF16), wv=w_v.astype(BF16),
        wo=w_o.astype(BF16), w0=_row(w0),
        w1=_pad_to(w1, 1, LORA_PAD).astype(BF16), w2=_pad_to(w2, 0, LORA_PAD).astype(BF16),
        a0=_row(a0),
        a1=_pad_to(a1, 1, LORA_PAD).astype(BF16), a2=_pad_to(a2, 0, LORA_PAD).astype(BF16),
        g1=_pad_to(g1, 1, GATE_LORA_PAD).astype(BF16), g2=_pad_to(g2, 0, GATE_LORA_PAD).astype(BF16),
        k_k=_row(k_k), k_a=_row(k_a), r_k=_row(r_k), lnx_g=_row(lnx_g), lnx_b=_row(lnx_b),
    )


def kernel(x_prompt, x_sample, state_gla, state_s5_re, state_s5_im, state_rwkv_shift, state_rwkv_wkv,
           c_prompt, c_sample, ada_w, ada_b, ln_g, ln_b,
           ffn1_wg, ffn1_wu, ffn1_wd, ffn2_wg, ffn2_wu, ffn2_wd,
           w_in, w_out, gla_w_gk, gla_b_gk, gla_norm_g,
           s5_a_re, s5_a_im, s5_log_step, s5_b_re, s5_b_im, s5_c_re, s5_c_im, s5_d, s5_w_glu, s5_b_glu,
           rwkv_mu, rwkv_w_r, rwkv_w_k, rwkv_w_v, rwkv_w_o, rwkv_w0, rwkv_w1, rwkv_w2,
           rwkv_a0, rwkv_a1, rwkv_a2, rwkv_g1, rwkv_g2, rwkv_k_k, rwkv_k_a, rwkv_r_k,
           rwkv_lnx_g, rwkv_lnx_b):
    bp = x_prompt.shape[0]
    ns = x_sample.shape[0]

    mods = _ada_mods(jnp.concatenate([c_sample, c_prompt], axis=0), ada_w, ada_b)
    mods_s = mods[:, :ns]
    mods_p = mods[:, ns:].reshape(DEPTH, bp, 1, N_MODS * D_MODEL)

    ffn1 = [_prep_ffn(ffn1_wg[l], ffn1_wu[l], ffn1_wd[l]) for l in range(DEPTH)]
    ffn2 = [_prep_ffn(ffn2_wg[l], ffn2_wu[l], ffn2_wd[l]) for l in range(DEPTH)]
    mix = _prep_mix(w_in, w_out, gla_w_gk, gla_b_gk, gla_norm_g, s5_w_glu, s5_b_glu)
    s5 = _s5_prepare(s5_a_re, s5_a_im, s5_log_step, s5_b_re, s5_b_im, s5_c_re, s5_c_im, s5_d)
    rwkv = _prep_rwkv(rwkv_mu, rwkv_w_r, rwkv_w_k, rwkv_w_v, rwkv_w_o, rwkv_w0, rwkv_w1, rwkv_w2,
                      rwkv_a0, rwkv_a1, rwkv_a2, rwkv_g1, rwkv_g2, rwkv_k_k, rwkv_k_a, rwkv_r_k,
                      rwkv_lnx_g, rwkv_lnx_b)

    def lnp(layer, idx):
        return _row(ln_g[layer, idx]), _row(ln_b[layer, idx])

    x = x_prompt
    x = _ffn_prompt(x, mods_p, 0, 0, *ffn1[0], *lnp(0, 0))
    x, gla_p, s5_re_p, s5_im_p = _mix_prompt(x, mods_p, 0, mix, s5, *lnp(0, 1))
    x = _ffn_prompt(x, mods_p, 0, 6, *ffn2[0], *lnp(0, 2))
    x = _ffn_prompt(x, mods_p, 1, 0, *ffn1[1], *lnp(1, 0))
    x, shift_p, wkv_p = _rwkv_prompt(x, mods_p, 1, rwkv, *lnp(1, 1))
    y_prompt = _ffn_prompt(x, mods_p, 1, 6, *ffn2[1], *lnp(1, 2))

    xs = x_sample.reshape(ns, D_MODEL)
    xs = _ffn_sample(xs, mods_s, 0, 0, *ffn1[0], *lnp(0, 0))
    xs, gla_s, s5_re_s, s5_im_s = _mix_sample(xs, mods_s, 0, state_gla, state_s5_re, state_s5_im,
                                              mix, s5, *lnp(0, 1))
    xs = _ffn_sample(xs, mods_s, 0, 6, *ffn2[0], *lnp(0, 2))
    xs = _ffn_sample(xs, mods_s, 1, 0, *ffn1[1], *lnp(1, 0))
    xs, shift_s, wkv_s = _rwkv_sample(xs, mods_s, 1, state_rwkv_shift, state_rwkv_wkv, rwkv, *lnp(1, 1))
    xs = _ffn_sample(xs, mods_s, 1, 6, *ffn2[1], *lnp(1, 2))
    y_sample = xs.reshape(ns, 1, D_MODEL)

    return (y_prompt, y_sample, gla_p, s5_re_p, s5_im_p, shift_p, wkv_p,
            gla_s, s5_re_s, s5_im_s, shift_s, wkv_s)
```

```python
import functools
import math

import jax
import jax.numpy as jnp
from jax import lax
from jax.experimental import pallas as pl
from jax.experimental.pallas import tpu as pltpu

F32 = jnp.float32
BF16 = jnp.bfloat16

D_MODEL = 1024
DEPTH = 2
DN_ALPHA = (2 * DEPTH) ** 0.25
LN_EPS = 1e-5
RMS_EPS = 1e-5
FFN_RES = 0.5
D_FF = 2752
N_MODS = 9

GLA_HEADS = 4
GLA_DK = 64
GLA_DV = 128
GLA_KEY = GLA_HEADS * GLA_DK
GLA_VAL = GLA_HEADS * GLA_DV
GLA_GATE_RANK = 16
GLA_GATE_NORM = 16.0

S5_GROUP = 16
S5_WIDTH = 512
S5_GROUPS = S5_WIDTH // S5_GROUP
S5_STATE = 64
S5_CH = S5_GROUPS * S5_STATE

RWKV_HEAD = 64
RWKV_HEADS = D_MODEL // RWKV_HEAD
RWKV_LNX_EPS = 64e-5
NORM_EPS = 1e-12

LANES = 128
SUBLANES = 8
MXU_DIM = 256
VMEM_LIMIT = 56 * 1024 * 1024

CHUNK = 64
HEADS_PER_GROUP = MXU_DIM // RWKV_HEAD
N_GROUPS = RWKV_HEADS // HEADS_PER_GROUP
D_FF_PAD = 2816
FF_CHUNKS = ((0, 1024), (1024, 2048), (2048, D_FF_PAD))
LORA_PAD = 128
GATE_LORA_PAD = 256
PERM_STRIDE = CHUNK + SUBLANES
SAMPLE_BLOCK = 8


def _dot(a, b):
    return jnp.dot(a, b, preferred_element_type=F32)


def _dot_nt(a, b):
    return lax.dot_general(a, b, (((1,), (1,)), ((), ())), preferred_element_type=F32)


def _bdot(a, w_ref_or_val):
    return _dot(a.astype(BF16), w_ref_or_val)


def _sigmoid(x):
    return 1.0 / (1.0 + jnp.exp(-x))


def _silu(x):
    return x * _sigmoid(x)


def _softplus(x):
    return jnp.maximum(x, 0.0) + jnp.log(1.0 + jnp.exp(-jnp.abs(x)))


def _gelu_tanh(x):
    c = math.sqrt(2.0 / math.pi)
    return 0.5 * x * (1.0 + jnp.tanh(c * (x + 0.044715 * (x * x * x))))


def _layer_norm(x, g, b):
    mu = jnp.mean(x, axis=-1, keepdims=True)
    d = x - mu
    var = jnp.mean(d * d, axis=-1, keepdims=True)
    return d * lax.rsqrt(var + LN_EPS) * g + b


def _split3(x):
    hi = x.astype(BF16)
    r1 = x - hi.astype(F32)
    mid = r1.astype(BF16)
    lo = (r1 - mid.astype(F32)).astype(BF16)
    return hi, mid, lo


def _exact_dot_left01(m01, x):
    hi, mid, lo = _split3(x)
    return _dot(m01, hi) + _dot(m01, mid) + _dot(m01, lo)


def _dot_right01(x, m01):
    hi = x.astype(BF16)
    lo = (x - hi.astype(F32)).astype(BF16)
    return _dot(hi, m01) + _dot(lo, m01)


def _seg_ones(seg):
    r = lax.broadcasted_iota(jnp.int32, (LANES, LANES), 0) // seg
    c = lax.broadcasted_iota(jnp.int32, (LANES, LANES), 1) // seg
    return jnp.where(r == c, 1.0, 0.0).astype(BF16)


def _seg_sum(x, seg):
    ones = _seg_ones(seg)
    tiles = [
        _dot_right01(x[:, i * LANES:(i + 1) * LANES], ones)
        for i in range(x.shape[1] // LANES)
    ]
    return jnp.concatenate(tiles, axis=1)


def _tril_ones(n):
    r = lax.broadcasted_iota(jnp.int32, (n, n), 0)
    c = lax.broadcasted_iota(jnp.int32, (n, n), 1)
    return jnp.where(r >= c, 1.0, 0.0).astype(BF16)


def _chunk_cumsum(x, n_batch):
    tri = _tril_ones(CHUNK)
    parts = [_exact_dot_left01(tri, x[b * CHUNK:(b + 1) * CHUNK, :]) for b in range(n_batch)]
    return jnp.concatenate(parts, axis=0)


def _head_stack(z, n_heads, width):
    head = lax.broadcasted_iota(jnp.int32, z.shape, 1) // width
    return jnp.concatenate([jnp.where(head == h, z, 0.0) for h in range(n_heads)], axis=0)


def _block_diag_mask(rows, cols, rblk, cblk):
    r = lax.broadcasted_iota(jnp.int32, (rows, cols), 0) // rblk
    c = lax.broadcasted_iota(jnp.int32, (rows, cols), 1) // cblk
    return r == c


def _full_spec(shape, grid_rank):
    zeros = (0,) * len(shape)
    if grid_rank == 1:
        return pl.BlockSpec(shape, lambda i: zeros, pipeline_mode=pl.Buffered(1))
    return pl.BlockSpec(shape, lambda i, j: zeros, pipeline_mode=pl.Buffered(1))


def _params(semantics):
    return pltpu.CompilerParams(dimension_semantics=semantics, vmem_limit_bytes=VMEM_LIMIT)


ADA_TN = 1152


def _ada_body(c_ref, w_ref, b_ref, o_ref):
    c = c_ref[...]
    o_ref[...] = _bdot(_silu(c), w_ref[...].astype(BF16)) + b_ref[...]


def _ada_mods(c_all, ada_w, ada_b):
    n = c_all.shape[0]
    width = N_MODS * D_MODEL
    return pl.pallas_call(
        _ada_body,
        grid=(DEPTH, width // ADA_TN),
        in_specs=[
            pl.BlockSpec((n, D_MODEL), lambda l, j: (0, 0)),
            pl.BlockSpec((None, D_MODEL, ADA_TN), lambda l, j: (l, 0, j)),
            pl.BlockSpec((None, 1, ADA_TN), lambda l, j: (l, 0, j)),
        ],
        out_specs=pl.BlockSpec((None, n, ADA_TN), lambda l, j: (l, 0, j)),
        out_shape=jax.ShapeDtypeStruct((DEPTH, n, width), F32),
        compiler_params=_params(("arbitrary", "arbitrary")),
        name="ada_mods",
    )(c_all, ada_w, ada_b.reshape(DEPTH, 1, width))


FFN_TM = 512


def _ffn_body(x_ref, sh_ref, sc_ref, gt_ref, wg_ref, wu_ref, wd_ref, g_ref, b_ref, o_ref):
    x = x_ref[...]
    h = (x * (1.0 + sc_ref[...]) + sh_ref[...]).astype(BF16)
    acc = None
    for lo, hi in FF_CHUNKS:
        g = _dot(h, wg_ref[:, lo:hi])
        u = _dot(h, wu_ref[:, lo:hi])
        y = _bdot(_silu(g) * u, wd_ref[lo:hi, :])
        acc = y if acc is None else acc + y
    r = FFN_RES * (1.0 + gt_ref[...]) * acc
    o_ref[...] = _layer_norm(DN_ALPHA * x + r, g_ref[...], b_ref[...])


def _mod_specs_prompt(layer, first):
    return [
        pl.BlockSpec((None, None, 1, D_MODEL), functools.partial(
            lambda b, i, col: (layer, b, 0, col), col=first + k))
        for k in range(3)
    ]


def _ffn_prompt(x, mods_p, layer, first, wg, wu, wd, ln_g, ln_b):
    bsz, seq, _ = x.shape
    w_specs = [_full_spec(wg.shape, 2), _full_spec(wu.shape, 2), _full_spec(wd.shape, 2),
               _full_spec((1, D_MODEL), 2), _full_spec((1, D_MODEL), 2)]
    return pl.pallas_call(
        _ffn_body,
        grid=(bsz, seq // FFN_TM),
        in_specs=[pl.BlockSpec((None, FFN_TM, D_MODEL), lambda b, i: (b, i, 0))]
        + _mod_specs_prompt(layer, first) + w_specs,
        out_specs=pl.BlockSpec((None, FFN_TM, D_MODEL), lambda b, i: (b, i, 0)),
        out_shape=jax.ShapeDtypeStruct(x.shape, F32),
        compiler_params=_params(("arbitrary", "arbitrary")),
        name="ffn_prompt",
    )(x, mods_p, mods_p, mods_p, wg, wu, wd, ln_g, ln_b)


def _mod_specs_sample(layer, first, rows):
    return [
        pl.BlockSpec((None, rows, D_MODEL), functools.partial(
            lambda i, col: (layer, 0, col), col=first + k))
        for k in range(3)
    ]


def _ffn_sample(x, mods_s, layer, first, wg, wu, wd, ln_g, ln_b):
    n = x.shape[0]
    w_specs = [_full_spec(wg.shape, 1), _full_spec(wu.shape, 1), _full_spec(wd.shape, 1),
               _full_spec((1, D_MODEL), 1), _full_spec((1, D_MODEL), 1)]
    return pl.pallas_call(
        _ffn_body,
        grid=(1,),
        in_specs=[_full_spec((n, D_MODEL), 1)] + _mod_specs_sample(layer, first, n) + w_specs,
        out_specs=_full_spec((n, D_MODEL), 1),
        out_shape=jax.ShapeDtypeStruct(x.shape, F32),
        compiler_params=_params(("arbitrary",)),
        name="ffn_sample",
    )(x, mods_s, mods_s, mods_s, wg, wu, wd, ln_g, ln_b)


def _rwkv_project(h, prev, mu_ref, wr, wk, wv, w0, w1, w2, a0, a1, a2, kk_ref, ka_ref):
    xx = prev - h
    mu = mu_ref[...]

    def mix(i):
        return h + xx * mu[i:i + 1, :]

    r = _bdot(mix(0), wr[...])
    k = _bdot(mix(2), wk[...])
    v = _bdot(mix(3), wv[...])
    wl = w0[...] + _bdot(jnp.tanh(_bdot(mix(1), w1[...])), w2[...])
    log_decay = -math.exp(-0.5) * _sigmoid(wl)
    a_sig = _sigmoid(a0[...] + _bdot(_bdot(mix(4), a1[...]), a2[...]))
    kk = k * kk_ref[...]
    norm = jnp.sqrt(_seg_sum(kk * kk, RWKV_HEAD))
    kk = kk / jnp.maximum(norm, NORM_EPS)
    k2 = k * (1.0 + (a_sig - 1.0) * ka_ref[...])
    return r, log_decay, k2, v, -kk, kk * a_sig, mix(5).astype(BF16)


def _rwkv_bonus(r, k2, v, rk_ref):
    return _seg_sum(r * k2 * rk_ref[...], RWKV_HEAD) * v


def _rwkv_output(y, bonus, xg, x, gt, g1, g2, wo, lng, lnb, ln_g, ln_b):
    gate = _bdot(_sigmoid(_dot(xg, g1[...])), g2[...])
    inv_n = 1.0 / RWKV_HEAD
    mu_y = _seg_sum(y, RWKV_HEAD) * inv_n
    d = y - mu_y
    var = _seg_sum(d * d, RWKV_HEAD) * inv_n
    y = d * lax.rsqrt(var + RWKV_LNX_EPS) * lng[...] + lnb[...]
    out = _bdot((y + bonus) * gate, wo[...])
    return _layer_norm(DN_ALPHA * x + (1.0 + gt) * out, ln_g[...], ln_b[...])


def _rwkv_weight_list(p):
    return [p["mu"], p["wr"], p["wk"], p["wv"], p["w0"], p["w1"], p["w2"], p["a0"], p["a1"], p["a2"],
            p["k_k"], p["k_a"], p["r_k"]]


def _rwkv_out_weight_list(p, ln_g, ln_b):
    return [p["g1"], p["g2"], p["wo"], p["lnx_g"], p["lnx_b"], ln_g, ln_b]


NEUMANN_LEVELS = int(math.log2(CHUNK)) - 1


def _wkv_chunks(problems):
    c = CHUNK
    hp = HEADS_PER_GROUP
    w = hp * RWKV_HEAD
    n = hp * c
    t_idx = lax.broadcasted_iota(jnp.int32, (c, w), 0)
    s_idx = lax.broadcasted_iota(jnp.int32, (c, w), 1) % c
    strict = t_idx > s_idx
    incl = t_idx >= s_idx
    eye = jnp.where(t_idx == s_idx, 1.0, 0.0)
    bd_mask = _block_diag_mask(w, w, RWKV_HEAD, RWKV_HEAD)
    idx = range(len(problems))
    At, Rt, Kt, Bt, Kh, Bh, V, S0, dec = zip(*problems)

    def bf(x):
        return x.astype(BF16)

    def stack(x, width):
        return bf(_head_stack(x, hp, width))

    X2 = [jnp.concatenate([At[i], Rt[i]], axis=0) for i in idx]
    KB = [jnp.concatenate([stack(Kt[i], RWKV_HEAD), stack(Bt[i], RWKV_HEAD)], axis=0) for i in idx]
    sc = [_dot_nt(X2[i], KB[i]) for i in idx]
    XS = [_dot_nt(X2[i], bf(S0[i])) for i in idx]
    Vbd = [stack(V[i], RWKV_HEAD) for i in idx]
    Aab = [jnp.where(strict, sc[i][:c, n:], 0.0) for i in idx]
    rhs = [XS[i][:c] + _dot(bf(jnp.where(strict, sc[i][:c, :n], 0.0)), Vbd[i]) for i in idx]

    Q = Aab
    P = [eye + Aab[i] for i in idx]
    Qbd = [stack(Q[i], c) for i in idx]
    for _ in range(NEUMANN_LEVELS):
        Q = [_dot(bf(Q[i]), Qbd[i]) for i in idx]
        Qbd = [stack(Q[i], c) for i in idx]
        P = [P[i] + _dot(bf(P[i]), Qbd[i]) for i in idx]
    U = [_dot(bf(P[i]), stack(rhs[i], RWKV_HEAD)) for i in idx]

    Y = [XS[i][c:] + _dot(bf(jnp.where(incl, sc[i][c:, :n], 0.0)), Vbd[i])
         + _dot(bf(jnp.where(incl, sc[i][c:, n:], 0.0)), stack(U[i], RWKV_HEAD)) for i in idx]

    upd = [_dot(bf(jnp.concatenate([V[i], U[i]], axis=0).T), jnp.concatenate([Kh[i], Bh[i]], axis=0))
           for i in idx]
    S1 = [S0[i] * dec[i] + jnp.where(bd_mask, upd[i], 0.0) for i in idx]
    return Y, S1


def _rwkv_prompt_body(x_ref, sh_ref, sc_ref, gt_ref,
                      mu, wr, wk, wv, w0, w1, w2, a0, a1, a2, kk_ref, ka_ref, rk_ref,
                      g1, g2, wo, lng, lnb, ln_g, ln_b,
                      o_ref, shift_ref, state_ref,
                      carry, s_at, s_rt, s_kt, s_bt, s_kh, s_bh, s_v, s_dec, s_y, s_bonus, s_xg):
    nb = x_ref.shape[0]
    rows = nb * CHUNK
    gw = HEADS_PER_GROUP * RWKV_HEAD
    step = pl.program_id(0)

    @pl.when(step == 0)
    def _():
        carry[...] = jnp.zeros(carry.shape, F32)
        state_ref[...] = jnp.zeros(state_ref.shape, F32)

    h3 = x_ref[...] * (1.0 + sc_ref[...]) + sh_ref[...]
    h = h3.reshape(rows, D_MODEL)
    first = lax.broadcasted_iota(jnp.int32, (rows, D_MODEL), 0) % CHUNK == 0
    carried = jnp.broadcast_to(carry[...], (nb, CHUNK, D_MODEL)).reshape(rows, D_MODEL)
    prev = jnp.where(first, carried, pltpu.roll(h, 1, axis=0))
    carry[...] = h3[:, CHUNK - 1:CHUNK, :]
    shift_ref[...] = h3[:, CHUNK - 1:CHUNK, :]

    r, lw, k2, v, av, bv, xg = _rwkv_project(
        h, prev, mu, wr, wk, wv, w0, w1, w2, a0, a1, a2, kk_ref, ka_ref)
    s_xg[...] = xg
    s_bonus[...] = _rwkv_bonus(r, k2, v, rk_ref)

    g = _chunk_cumsum(lw, nb)
    g3 = g.reshape(nb, CHUNK, D_MODEL)
    g_last = jnp.broadcast_to(g3[:, CHUNK - 1:CHUNK, :], (nb, CHUNK, D_MODEL)).reshape(rows, D_MODEL)
    e_neg = jnp.exp(-g)
    e_tail = jnp.exp(g_last - g)
    arrays = (
        (s_at, av * jnp.exp(g - lw)),
        (s_rt, r * jnp.exp(g)),
        (s_kt, k2 * e_neg),
        (s_bt, bv * e_neg),
        (s_kh, k2 * e_tail),
        (s_bh, bv * e_tail),
        (s_v, v),
    )
    for ref, val in arrays:
        for gi in range(N_GROUPS):
            ref[gi] = val[:, gi * gw:(gi + 1) * gw].astype(ref.dtype)
    dec = jnp.exp(g3[:, CHUNK - 1:CHUNK, :])
    for gi in range(N_GROUPS):
        s_dec[gi] = dec[:, :, gi * gw:(gi + 1) * gw]

    def chunk_step(b, carry_val):
        sl = pl.ds(pl.multiple_of(b * CHUNK, CHUNK), CHUNK)
        Y, S1 = _wkv_chunks([
            (s_at[gi, sl, :], s_rt[gi, sl, :], s_kt[gi, sl, :], s_bt[gi, sl, :],
             s_kh[gi, sl, :], s_bh[gi, sl, :], s_v[gi, sl, :], state_ref[b, gi], s_dec[gi, b])
            for gi in range(N_GROUPS)])
        for gi in range(N_GROUPS):
            s_y[gi, sl, :] = Y[gi]
            state_ref[b, gi] = S1[gi]
        return carry_val

    lax.fori_loop(0, nb, chunk_step, 0)

    y = jnp.concatenate([s_y[gi] for gi in range(N_GROUPS)], axis=1)
    gt = jnp.broadcast_to(gt_ref[...], (nb, CHUNK, D_MODEL)).reshape(rows, D_MODEL)
    x = x_ref[...].reshape(rows, D_MODEL)
    out = _rwkv_output(y, s_bonus[...], s_xg[...], x, gt, g1, g2, wo, lng, lnb, ln_g, ln_b)
    o_ref[...] = out.reshape(nb, CHUNK, D_MODEL)


def _rwkv_prompt(x, mods_p, layer, p, ln_g, ln_b):
    bsz, seq, _ = x.shape
    rows = bsz * CHUNK
    gw = HEADS_PER_GROUP * RWKV_HEAD
    weights = _rwkv_weight_list(p) + _rwkv_out_weight_list(p, ln_g, ln_b)
    mod_specs = [
        pl.BlockSpec((None, bsz, 1, D_MODEL), functools.partial(lambda i, col: (layer, 0, 0, col), col=3 + k))
        for k in range(3)
    ]
    slab = pltpu.VMEM((N_GROUPS, rows, gw), F32)
    slab16 = pltpu.VMEM((N_GROUPS, rows, gw), BF16)
    out, shift, state = pl.pallas_call(
        _rwkv_prompt_body,
        grid=(seq // CHUNK,),
        in_specs=[pl.BlockSpec((bsz, CHUNK, D_MODEL), lambda i: (0, i, 0))] + mod_specs
        + [_full_spec(w.shape, 1) for w in weights],
        out_specs=[
            pl.BlockSpec((bsz, CHUNK, D_MODEL), lambda i: (0, i, 0)),
            _full_spec((bsz, 1, D_MODEL), 1),
            _full_spec((bsz, N_GROUPS, gw, gw), 1),
        ],
        out_shape=[
            jax.ShapeDtypeStruct(x.shape, F32),
            jax.ShapeDtypeStruct((bsz, 1, D_MODEL), F32),
            jax.ShapeDtypeStruct((bsz, N_GROUPS, gw, gw), F32),
        ],
        scratch_shapes=[pltpu.VMEM((bsz, 1, D_MODEL), F32), slab16, slab16, slab, slab, slab16, slab16, slab]
        + [pltpu.VMEM((N_GROUPS, bsz, 1, gw), F32), slab,
           pltpu.VMEM((rows, D_MODEL), F32), pltpu.VMEM((rows, D_MODEL), BF16)],
        compiler_params=_params(("arbitrary",)),
        name="rwkv_prompt",
    )(x, mods_p, mods_p, mods_p, *weights)
    s6 = state.reshape(bsz, N_GROUPS, HEADS_PER_GROUP, RWKV_HEAD, HEADS_PER_GROUP, RWKV_HEAD)
    wkv = jnp.stack([s6[:, :, hh, :, hh, :] for hh in range(HEADS_PER_GROUP)], axis=2)
    return out, shift.reshape(bsz, D_MODEL), wkv.reshape(bsz, RWKV_HEADS, RWKV_HEAD, RWKV_HEAD)


def _rwkv_sample_pre_body(x_ref, sh_ref, sc_ref, prev_ref,
                          mu, wr, wk, wv, w0, w1, w2, a0, a1, a2, kk_ref, ka_ref, rk_ref,
                          h_ref, wr_ref, w_ref, k_ref, v_ref, a_ref, b_ref, br_ref, kr_ref, xg_ref, bonus_ref):
    h = x_ref[...] * (1.0 + sc_ref[...]) + sh_ref[...]
    r, lw, k2, v, av, bv, xg = _rwkv_project(
        h, prev_ref[...], mu, wr, wk, wv, w0, w1, w2, a0, a1, a2, kk_ref, ka_ref)
    w = jnp.exp(lw)
    h_ref[...] = h
    wr_ref[...] = w * r
    w_ref[...] = w
    k_ref[...] = k2
    v_ref[...] = v
    a_ref[...] = av
    b_ref[...] = bv
    br_ref[...] = _seg_sum(bv * r, RWKV_HEAD)
    kr_ref[...] = _seg_sum(k2 * r, RWKV_HEAD)
    xg_ref[...] = xg
    bonus_ref[...] = _rwkv_bonus(r, k2, v, rk_ref)


def _rwkv_sample_step_body(s_ref, wr_ref, w_ref, k_ref, v_ref, a_ref, b_ref, br_ref, kr_ref,
                           so_ref, y_ref):
    n = RWKV_HEAD
    hds = RWKV_HEADS
    rowi = lax.broadcasted_iota(jnp.int32, (hds, SUBLANES, n), 1)
    eye = (lax.broadcasted_iota(jnp.int32, (hds, n, n), 1)
           == lax.broadcasted_iota(jnp.int32, (hds, n, n), 2))
    ones = jnp.ones((n, n), BF16)
    for j in range(SAMPLE_BLOCK):
        S = s_ref[j]
        a, w, b, k, v = a_ref[j], w_ref[j], b_ref[j], k_ref[j], v_ref[j]
        X = jnp.where(rowi == 0, a, jnp.where(rowi == 1, wr_ref[j], 0.0)).astype(BF16)
        M = lax.dot_general(X, S.astype(BF16), (((2,), (2,)), ((0,), (0,))),
                            preferred_element_type=F32)
        y_ref[j] = M[:, 1:2, :] + M[:, 0:1, :] * br_ref[j] + v * kr_ref[j]
        sa_b = _dot((S * a).reshape(hds * n, n).astype(BF16), ones).reshape(hds, n, n)
        v_b = _dot_right01(jnp.where(eye, v, 0.0).reshape(hds * n, n), ones).reshape(hds, n, n)
        so_ref[j] = S * w + sa_b * b + v_b * k


def _rwkv_sample_post_body(y_ref, bonus_ref, xg_ref, x_ref, gt_ref,
                           g1, g2, wo, lng, lnb, ln_g, ln_b, o_ref):
    o_ref[...] = _rwkv_output(y_ref[...], bonus_ref[...], xg_ref[...], x_ref[...], gt_ref[...],
                              g1, g2, wo, lng, lnb, ln_g, ln_b)


def _to_columns(a):
    n, f = a.shape
    return a.reshape(n // SAMPLE_BLOCK, SAMPLE_BLOCK, f).transpose(0, 2, 1)


def _rwkv_sample(x, mods_s, layer, s_shift, s_wkv, p, ln_g, ln_b):
    n = x.shape[0]
    row = jax.ShapeDtypeStruct((n, D_MODEL), F32)
    row_spec = _full_spec((n, D_MODEL), 1)
    pre_w = _rwkv_weight_list(p)
    ms = _mod_specs_sample(layer, 3, n)
    h, *vecs, xg, bonus = pl.pallas_call(
        _rwkv_sample_pre_body,
        grid=(1,),
        in_specs=[row_spec, ms[0], ms[1], row_spec] + [_full_spec(a.shape, 1) for a in pre_w],
        out_specs=[row_spec] * 11,
        out_shape=[row] * 9 + [jax.ShapeDtypeStruct((n, D_MODEL), BF16), row],
        compiler_params=_params(("arbitrary",)),
        name="rwkv_sample_pre",
    )(x, mods_s, mods_s, s_shift, *pre_w)

    nblk = n // SAMPLE_BLOCK
    vec_shape = (n, RWKV_HEADS, 1, RWKV_HEAD)
    vec_spec = pl.BlockSpec((SAMPLE_BLOCK, RWKV_HEADS, 1, RWKV_HEAD), lambda i: (i, 0, 0, 0))
    st_spec = pl.BlockSpec((SAMPLE_BLOCK, RWKV_HEADS, RWKV_HEAD, RWKV_HEAD), lambda i: (i, 0, 0, 0))
    s_new, y = pl.pallas_call(
        _rwkv_sample_step_body,
        grid=(nblk,),
        in_specs=[st_spec] + [vec_spec] * len(vecs),
        out_specs=[st_spec, vec_spec],
        out_shape=[jax.ShapeDtypeStruct(s_wkv.shape, F32), jax.ShapeDtypeStruct(vec_shape, F32)],
        compiler_params=_params(("arbitrary",)),
        name="rwkv_sample_step",
    )(s_wkv, *[a.reshape(vec_shape) for a in vecs])

    post_w = _rwkv_out_weight_list(p, ln_g, ln_b)
    out = pl.pallas_call(
        _rwkv_sample_post_body,
        grid=(1,),
        in_specs=[row_spec] * 4 + [ms[2]] + [_full_spec(a.shape, 1) for a in post_w],
        out_specs=row_spec,
        out_shape=row,
        compiler_params=_params(("arbitrary",)),
        name="rwkv_sample_post",
    )(y.reshape(n, D_MODEL), bonus, xg, x, mods_s, *post_w)
    return out, h, s_new


def _s5_prep_body(are_ref, aim_ref, ls_ref, bre_ref, bim_ref, abre_ref, abim_ref, bbre_ref, bbim_ref):
    a_re = are_ref[...]
    a_im = aim_ref[...]
    dt = jnp.exp(ls_ref[...])
    mag = jnp.exp(a_re * dt)
    ab_re = mag * jnp.cos(a_im * dt)
    ab_im = mag * jnp.sin(a_im * dt)
    den = a_re * a_re + a_im * a_im
    nr = ab_re - 1.0
    z_re = (nr * a_re + ab_im * a_im) / den
    z_im = (ab_im * a_re - nr * a_im) / den
    b_re = bre_ref[...]
    b_im = bim_ref[...]
    abre_ref[...] = ab_re
    abim_ref[...] = ab_im
    bbre_ref[...] = z_re * b_re - z_im * b_im
    bbim_ref[...] = z_re * b_im + z_im * b_re


def _s5_prepare(s5_a_re, s5_a_im, s5_log_step, s5_b_re, s5_b_im, s5_c_re, s5_c_im, s5_d):
    G, P, C = S5_GROUPS, S5_STATE, S5_GROUP
    small = jax.ShapeDtypeStruct((G, 1, P), F32)
    big = jax.ShapeDtypeStruct((G, C, P), F32)
    ls = jnp.broadcast_to(s5_log_step.reshape(G, 1, 1), (G, 1, P))
    ab_re, ab_im, bb_re, bb_im = pl.pallas_call(
        _s5_prep_body,
        grid=(1,),
        in_specs=[_full_spec((G, 1, P), 1)] * 3 + [_full_spec((G, C, P), 1)] * 2,
        out_specs=[_full_spec((G, 1, P), 1)] * 2 + [_full_spec((G, C, P), 1)] * 2,
        out_shape=[small, small, big, big],
        name="s5_prepare",
    )(s5_a_re.reshape(G, 1, P), s5_a_im.reshape(G, 1, P), ls,
      s5_b_re.transpose(0, 2, 1), s5_b_im.transpose(0, 2, 1))

    eye = jnp.eye(G, dtype=F32)
    gpt = LANES // C
    n_tiles = G // gpt

    def in_blocks(bb):
        full = (bb[:, :, None, :] * eye[:, None, :, None]).reshape(G * C, G * P)
        return jnp.stack([full[k * LANES:(k + 1) * LANES, k * gpt * P:(k + 1) * gpt * P]
                          for k in range(n_tiles)]).astype(BF16)

    def out_blocks(cc):
        full = (cc.transpose(0, 2, 1)[:, :, None, :] * eye[:, None, :, None]).reshape(G * P, G * C)
        return jnp.stack([full[k * gpt * P:(k + 1) * gpt * P, k * LANES:(k + 1) * LANES]
                          for k in range(n_tiles)]).astype(BF16)

    return dict(
        ab_re=ab_re.reshape(1, G * P), ab_im=ab_im.reshape(1, G * P),
        bb_re=in_blocks(bb_re), bb_im=in_blocks(bb_im),
        c_re=out_blocks(s5_c_re), c_im=out_blocks(s5_c_im),
        d=s5_d.reshape(1, G * C),
    )


def _s5_input(u, bb_re_ref, bb_im_ref):
    n_tiles = bb_re_ref.shape[0]
    ub = u.astype(BF16)
    re = [_dot(ub[:, k * LANES:(k + 1) * LANES], bb_re_ref[k]) for k in range(n_tiles)]
    im = [_dot(ub[:, k * LANES:(k + 1) * LANES], bb_im_ref[k]) for k in range(n_tiles)]
    return jnp.concatenate(re, axis=1), jnp.concatenate(im, axis=1)


def _s5_output(h_re, h_im, c_re_ref, c_im_ref):
    n_tiles = c_re_ref.shape[0]
    w = h_re.shape[1] // n_tiles
    hr = h_re.astype(BF16)
    hi = h_im.astype(BF16)
    ys = [_dot(hr[:, k * w:(k + 1) * w], c_re_ref[k]) - _dot(hi[:, k * w:(k + 1) * w], c_im_ref[k])
          for k in range(n_tiles)]
    return jnp.concatenate(ys, axis=1)


C_Q, C_K, C_V, C_G, C_U, C_GK, C_END = 0, 256, 512, 1024, 1536, 2048, 2176


def _mix_project(h, w_in, w_gk, b_gk):
    p = _bdot(h, w_in[...])
    q = p[:, C_Q:C_K] * (GLA_DK ** -0.5)
    k = p[:, C_K:C_V]
    v = p[:, C_V:C_G]
    gg = p[:, C_G:C_U]
    u = p[:, C_U:C_GK]
    z = _bdot(p[:, C_GK:C_END], w_gk[...]) + b_gk[...]
    gk = -_softplus(-z) * (1.0 / GLA_GATE_NORM)
    return q, k, v, gg, u, gk


def _gla_finish(o, gg, norm_g):
    parts = []
    for hh in range(GLA_HEADS):
        oh = o[:, hh * GLA_DV:(hh + 1) * GLA_DV]
        parts.append(oh * lax.rsqrt(jnp.mean(oh * oh, axis=-1, keepdims=True) + RMS_EPS) * norm_g[...])
    return jnp.concatenate(parts, axis=1) * _silu(gg)


def _mix_output(o_gla, y_s5, x, gt, w_glu, b_glu, w_out, ln_g, ln_b):
    z = _gelu_tanh(y_s5)
    o_s5 = z * _sigmoid(_bdot(z, w_glu[...]) + b_glu[...])
    out = _bdot(o_gla, w_out[0:GLA_VAL, :]) + _bdot(o_s5, w_out[GLA_VAL:GLA_VAL + S5_WIDTH, :])
    return _layer_norm(DN_ALPHA * x + (1.0 + gt) * out, ln_g[...], ln_b[...])


def _gla_chunk(q, k, v, bc, ST):
    c = CHUNK
    mid = c // 2 - 1
    b_mid = bc[mid:mid + 1, :]
    b_last = bc[c - 1:c, :]
    q_in = q * jnp.exp(bc - b_mid)
    k_in = k * jnp.exp(b_mid - bc)
    q_full = q * jnp.exp(bc)
    k_st = k * jnp.exp(b_last - bc)
    t_idx = lax.broadcasted_iota(jnp.int32, (c, GLA_KEY), 0)
    s_idx = lax.broadcasted_iota(jnp.int32, (c, GLA_KEY), 1) % c
    scores = _dot_nt(q_in.astype(BF16), _head_stack(k_in, GLA_HEADS, GLA_DK).astype(BF16))
    scores = jnp.where(t_idx >= s_idx, scores, 0.0)
    o = _dot(scores.astype(BF16), _head_stack(v, GLA_HEADS, GLA_DV).astype(BF16))
    o = o + _dot_nt(q_full.astype(BF16), ST.astype(BF16))
    upd = _dot(v.T.astype(BF16), k_st.astype(BF16))
    ST1 = ST * jnp.exp(b_last) + jnp.where(
        _block_diag_mask(GLA_VAL, GLA_KEY, GLA_DV, GLA_DK), upd, 0.0)
    return o, ST1


def _mix_prompt_body(x_ref, sh_ref, sc_ref, gt_ref,
                     w_in, w_gk, b_gk, norm_g, ab_re_ref, ab_im_ref, bb_re, bb_im, c_re, c_im, d_ref,
                     w_glu, b_glu, w_out, ln_g, ln_b,
                     o_ref, gla_ref, hre_ref, him_ref,
                     s_q, s_k, s_v, s_bc, s_o, s_perm, s_re, s_im):
    nb = x_ref.shape[0]
    rows = nb * CHUNK
    step = pl.program_id(0)

    @pl.when(step == 0)
    def _():
        gla_ref[...] = jnp.zeros(gla_ref.shape, F32)
        hre_ref[...] = jnp.zeros(hre_ref.shape, F32)
        him_ref[...] = jnp.zeros(him_ref.shape, F32)

    x3 = x_ref[...]
    h3 = x3 * (1.0 + sc_ref[...]) + sh_ref[...]
    x = x3.reshape(rows, D_MODEL)
    h = h3.reshape(rows, D_MODEL)
    q, k, v, gg, u, gk = _mix_project(h, w_in, w_gk, b_gk)

    s_q[...] = q
    s_k[...] = k
    s_v[...] = v
    s_bc[...] = _chunk_cumsum(gk, nb)

    def gla_step(b, carry_val):
        sl = pl.ds(pl.multiple_of(b * CHUNK, CHUNK), CHUNK)
        o, ST1 = _gla_chunk(s_q[sl, :], s_k[sl, :], s_v[sl, :], s_bc[sl, :], gla_ref[b])
        s_o[sl, :] = o
        gla_ref[b] = ST1
        return carry_val

    lax.fori_loop(0, nb, gla_step, 0)
    o_gla = _gla_finish(s_o[...], gg, norm_g)

    n_lt = S5_WIDTH // LANES
    for b in range(nb):
        for lt in range(n_lt):
            s_perm[lt, b * PERM_STRIDE:b * PERM_STRIDE + CHUNK, :] = (
                u[b * CHUNK:(b + 1) * CHUNK, lt * LANES:(lt + 1) * LANES])
    u_tb = jnp.concatenate(
        [jnp.concatenate([s_perm[lt, pl.ds(t, nb, stride=PERM_STRIDE), :] for lt in range(n_lt)], axis=1)
         for t in range(CHUNK)], axis=0)
    bu_re, bu_im = _s5_input(u_tb, bb_re, bb_im)
    s_re[...] = bu_re
    s_im[...] = bu_im

    n_ch = 4
    cw = S5_CH // n_ch
    for ci in range(n_ch):
        lanes = slice(ci * cw, (ci + 1) * cw)
        a_re = jnp.broadcast_to(ab_re_ref[:, lanes], (nb, cw))
        a_im = jnp.broadcast_to(ab_im_ref[:, lanes], (nb, cw))

        def scan_step(t, hc, lanes=lanes, a_re=a_re, a_im=a_im):
            h_re, h_im = hc
            sl = pl.ds(pl.multiple_of(t * nb, nb), nb)
            n_re = a_re * h_re - a_im * h_im + s_re[sl, lanes]
            n_im = a_re * h_im + a_im * h_re + s_im[sl, lanes]
            s_re[sl, lanes] = n_re
            s_im[sl, lanes] = n_im
            return n_re, n_im

        f_re, f_im = lax.fori_loop(0, CHUNK, scan_step, (hre_ref[:, lanes], him_ref[:, lanes]))
        hre_ref[:, lanes] = f_re
        him_ref[:, lanes] = f_im

    y_tb = _s5_output(s_re[...], s_im[...], c_re, c_im)
    for t in range(CHUNK):
        for lt in range(n_lt):
            s_perm[lt, pl.ds(t, nb, stride=PERM_STRIDE), :] = (
                y_tb[t * nb:(t + 1) * nb, lt * LANES:(lt + 1) * LANES])
    y_s5 = jnp.concatenate(
        [jnp.concatenate([s_perm[lt, b * PERM_STRIDE:b * PERM_STRIDE + CHUNK, :] for lt in range(n_lt)],
                         axis=1) for b in range(nb)], axis=0)
    y_s5 = y_s5 + d_ref[...] * u

    gt = jnp.broadcast_to(gt_ref[...], (nb, CHUNK, D_MODEL)).reshape(rows, D_MODEL)
    out = _mix_output(o_gla, y_s5, x, gt, w_glu, b_glu, w_out, ln_g, ln_b)
    o_ref[...] = out.reshape(nb, CHUNK, D_MODEL)


def _mix_weight_list(m, s5):
    return [m["w_in"], m["w_gk"], m["b_gk"], m["norm_g"], s5["ab_re"], s5["ab_im"], s5["bb_re"], s5["bb_im"],
            s5["c_re"], s5["c_im"], s5["d"], m["w_glu"], m["b_glu"], m["w_out"]]


def _mix_prompt(x, mods_p, layer, m, s5, ln_g, ln_b):
    bsz, seq, _ = x.shape
    rows = bsz * CHUNK
    weights = _mix_weight_list(m, s5) + [ln_g, ln_b]
    mod_specs = [
        pl.BlockSpec((None, bsz, 1, D_MODEL), functools.partial(lambda i, col: (layer, 0, 0, col), col=3 + k))
        for k in range(3)
    ]
    out, gla_t, h_re, h_im = pl.pallas_call(
        _mix_prompt_body,
        grid=(seq // CHUNK,),
        in_specs=[pl.BlockSpec((bsz, CHUNK, D_MODEL), lambda i: (0, i, 0))] + mod_specs
        + [_full_spec(w.shape, 1) for w in weights],
        out_specs=[
            pl.BlockSpec((bsz, CHUNK, D_MODEL), lambda i: (0, i, 0)),
            _full_spec((bsz, GLA_VAL, GLA_KEY), 1),
            _full_spec((bsz, S5_CH), 1),
            _full_spec((bsz, S5_CH), 1),
        ],
        out_shape=[
            jax.ShapeDtypeStruct(x.shape, F32),
            jax.ShapeDtypeStruct((bsz, GLA_VAL, GLA_KEY), F32),
            jax.ShapeDtypeStruct((bsz, S5_CH), F32),
            jax.ShapeDtypeStruct((bsz, S5_CH), F32),
        ],
        scratch_shapes=[
            pltpu.VMEM((rows, GLA_KEY), F32), pltpu.VMEM((rows, GLA_KEY), F32),
            pltpu.VMEM((rows, GLA_VAL), F32), pltpu.VMEM((rows, GLA_KEY), F32),
            pltpu.VMEM((rows, GLA_VAL), F32),
            pltpu.VMEM((S5_WIDTH // LANES, bsz * PERM_STRIDE, LANES), F32),
            pltpu.VMEM((rows, S5_CH), F32), pltpu.VMEM((rows, S5_CH), F32),
        ],
        compiler_params=_params(("arbitrary",)),
        name="gla_s5_prompt",
    )(x, mods_p, mods_p, mods_p, *weights)
    s6 = gla_t.reshape(bsz, GLA_HEADS, GLA_DV, GLA_HEADS, GLA_DK)
    gla = jnp.stack([s6[:, hh, :, hh, :] for hh in range(GLA_HEADS)], axis=1).transpose(0, 1, 3, 2)
    return (out, gla, h_re.reshape(bsz, S5_GROUPS, S5_STATE), h_im.reshape(bsz, S5_GROUPS, S5_STATE))


def _mix_sample_pre_body(x_ref, sh_ref, sc_ref, hre_ref, him_ref,
                         w_in, w_gk, b_gk, ab_re_ref, ab_im_ref, bb_re, bb_im, c_re, c_im, d_ref,
                         q_ref, k_ref, v_ref, gg_ref, dec_ref, y_ref, nre_ref, nim_ref):
    h = x_ref[...] * (1.0 + sc_ref[...]) + sh_ref[...]
    q, k, v, gg, u, gk = _mix_project(h, w_in, w_gk, b_gk)
    q_ref[...] = q
    k_ref[...] = k
    v_ref[...] = v
    gg_ref[...] = gg
    dec_ref[...] = jnp.exp(gk)
    bu_re, bu_im = _s5_input(u, bb_re, bb_im)
    a_re = ab_re_ref[...]
    a_im = ab_im_ref[...]
    h_re = hre_ref[...]
    h_im = him_ref[...]
    n_re = a_re * h_re - a_im * h_im + bu_re
    n_im = a_re * h_im + a_im * h_re + bu_im
    nre_ref[...] = n_re
    nim_ref[...] = n_im
    y_ref[...] = _s5_output(n_re, n_im, c_re, c_im) + d_ref[...] * u


def _gla_sample_step_body(s_ref, qc_ref, kc_ref, dc_ref, v_ref, so_ref, o_ref):
    for j in range(SAMPLE_BLOCK):
        for hh in range(GLA_HEADS):
            rows = slice(hh * GLA_DK, (hh + 1) * GLA_DK)
            lanes = slice(hh * GLA_DV, (hh + 1) * GLA_DV)
            S1 = dc_ref[rows, j:j + 1] * s_ref[j, hh] + kc_ref[rows, j:j + 1] * v_ref[j:j + 1, lanes]
            so_ref[j, hh] = S1
            o_ref[j:j + 1, lanes] = jnp.sum(qc_ref[rows, j:j + 1] * S1, axis=0, keepdims=True)


def _mix_sample_post_body(o_ref_in, gg_ref, y_ref, x_ref, gt_ref, norm_g, w_glu, b_glu, w_out, ln_g, ln_b,
                          o_ref):
    o_gla = _gla_finish(o_ref_in[...], gg_ref[...], norm_g)
    o_ref[...] = _mix_output(o_gla, y_ref[...], x_ref[...], gt_ref[...], w_glu, b_glu, w_out, ln_g, ln_b)


def _mix_sample(x, mods_s, layer, s_gla, s_re, s_im, m, s5, ln_g, ln_b):
    n = x.shape[0]
    ms = _mod_specs_sample(layer, 3, n)
    row_spec = _full_spec((n, D_MODEL), 1)

    def sds(w):
        return jax.ShapeDtypeStruct((n, w), F32)

    def fs(w):
        return _full_spec((n, w), 1)

    pre_w = [m["w_in"], m["w_gk"], m["b_gk"], s5["ab_re"], s5["ab_im"], s5["bb_re"], s5["bb_im"],
             s5["c_re"], s5["c_im"], s5["d"]]
    widths = [GLA_KEY, GLA_KEY, GLA_VAL, GLA_VAL, GLA_KEY, S5_WIDTH, S5_CH, S5_CH]
    q, k, v, gg, dec, y_s5, n_re, n_im = pl.pallas_call(
        _mix_sample_pre_body,
        grid=(1,),
        in_specs=[row_spec, ms[0], ms[1], fs(S5_CH), fs(S5_CH)] + [_full_spec(a.shape, 1) for a in pre_w],
        out_specs=[fs(w) for w in widths],
        out_shape=[sds(w) for w in widths],
        compiler_params=_params(("arbitrary",)),
        name="gla_s5_sample_pre",
    )(x, mods_s, mods_s, s_re.reshape(n, S5_CH), s_im.reshape(n, S5_CH), *pre_w)

    nblk = n // SAMPLE_BLOCK
    st_spec = pl.BlockSpec((SAMPLE_BLOCK, GLA_HEADS, GLA_DK, GLA_DV), lambda i: (i, 0, 0, 0))
    col_spec = pl.BlockSpec((None, GLA_KEY, SAMPLE_BLOCK), lambda i: (i, 0, 0))
    v_spec = pl.BlockSpec((SAMPLE_BLOCK, GLA_VAL), lambda i: (i, 0))
    s_new, o = pl.pallas_call(
        _gla_sample_step_body,
        grid=(nblk,),
        in_specs=[st_spec, col_spec, col_spec, col_spec, v_spec],
        out_specs=[st_spec, v_spec],
        out_shape=[jax.ShapeDtypeStruct(s_gla.shape, F32), sds(GLA_VAL)],
        compiler_params=_params(("arbitrary",)),
        name="gla_sample_step",
    )(s_gla, _to_columns(q), _to_columns(k), _to_columns(dec), v)

    post_w = [m["norm_g"], m["w_glu"], m["b_glu"], m["w_out"], ln_g, ln_b]
    out = pl.pallas_call(
        _mix_sample_post_body,
        grid=(1,),
        in_specs=[fs(GLA_VAL), fs(GLA_VAL), fs(S5_WIDTH), row_spec, ms[2]]
        + [_full_spec(a.shape, 1) for a in post_w],
        out_specs=row_spec,
        out_shape=sds(D_MODEL),
        compiler_params=_params(("arbitrary",)),
        name="gla_s5_sample_post",
    )(o, gg, y_s5, x, mods_s, *post_w)
    return (out, s_new, n_re.reshape(n, S5_GROUPS, S5_STATE), n_im.reshape(n, S5_GROUPS, S5_STATE))


def _pad_to(a, axis, size):
    pad = [(0, 0)] * a.ndim
    pad[axis] = (0, size - a.shape[axis])
    return jnp.pad(a, pad)


def _row(a):
    return a.reshape(1, -1).astype(F32)


def _prep_ffn(wg, wu, wd):
    return (_pad_to(wg, 1, D_FF_PAD).astype(BF16), _pad_to(wu, 1, D_FF_PAD).astype(BF16),
            _pad_to(wd, 0, D_FF_PAD).astype(BF16))


def _prep_mix(w_in, w_out, gla_w_gk, gla_b_gk, gla_norm_g, s5_w_glu, s5_b_glu):
    q, k, v, g, gk_low, u = jnp.split(
        w_in, (GLA_KEY, 2 * GLA_KEY, 2 * GLA_KEY + GLA_VAL, 2 * GLA_KEY + 2 * GLA_VAL,
               2 * GLA_KEY + 2 * GLA_VAL + GLA_GATE_RANK), axis=1)
    w_in_r = jnp.concatenate([q, k, v, g, u, _pad_to(gk_low, 1, LANES)], axis=1)
    return dict(
        w_in=w_in_r.astype(BF16), w_out=w_out.astype(BF16),
        w_gk=_pad_to(gla_w_gk, 0, LANES).astype(BF16), b_gk=_row(gla_b_gk), norm_g=_row(gla_norm_g),
        w_glu=s5_w_glu.astype(BF16), b_glu=_row(s5_b_glu),
    )


def _prep_rwkv(mu, w_r, w_k, w_v, w_o, w0, w1, w2, a0, a1, a2, g1, g2, k_k, k_a, r_k, lnx_g, lnx_b):
    return dict(
        mu=_pad_to(mu, 0, SUBLANES), wr=w_r.astype(BF16), wk=w_k.astype(BF16), wv=w_v.astype(BF16),
        wo=w_o.astype(BF16), w0=_row(w0),
        w1=_pad_to(w1, 1, LORA_PAD).astype(BF16), w2=_pad_to(w2, 0, LORA_PAD).astype(BF16),
        a0=_row(a0),
        a1=_pad_to(a1, 1, LORA_PAD).astype(BF16), a2=_pad_to(a2, 0, LORA_PAD).astype(BF16),
        g1=_pad_to(g1, 1, GATE_LORA_PAD).astype(BF16), g2=_pad_to(g2, 0, GATE_LORA_PAD).astype(BF16),
        k_k=_row(k_k), k_a=_row(k_a), r_k=_row(r_k), lnx_g=_row(lnx_g), lnx_b=_row(lnx_b),
    )


def kernel(x_prompt, x_sample, state_gla, state_s5_re, state_s5_im, state_rwkv_shift, state_rwkv_wkv,
           c_prompt, c_sample, ada_w, ada_b, ln_g, ln_b,
           ffn1_wg, ffn1_wu, ffn1_wd, ffn2_wg, ffn2_wu, ffn2_wd,
           w_in, w_out, gla_w_gk, gla_b_gk, gla_norm_g,
           s5_a_re, s5_a_im, s5_log_step, s5_b_re, s5_b_im, s5_c_re, s5_c_im, s5_d, s5_w_glu, s5_b_glu,
           rwkv_mu, rwkv_w_r, rwkv_w_k, rwkv_w_v, rwkv_w_o, rwkv_w0, rwkv_w1, rwkv_w2,
           rwkv_a0, rwkv_a1, rwkv_a2, rwkv_g1, rwkv_g2, rwkv_k_k, rwkv_k_a, rwkv_r_k,
           rwkv_lnx_g, rwkv_lnx_b):
    bp = x_prompt.shape[0]
    ns = x_sample.shape[0]

    mods = _ada_mods(jnp.concatenate([c_sample, c_prompt], axis=0), ada_w, ada_b)
    mods_s = mods[:, :ns]
    mods_p = mods[:, ns:].reshape(DEPTH, bp, 1, N_MODS * D_MODEL)

    ffn1 = [_prep_ffn(ffn1_wg[l], ffn1_wu[l], ffn1_wd[l]) for l in range(DEPTH)]
    ffn2 = [_prep_ffn(ffn2_wg[l], ffn2_wu[l], ffn2_wd[l]) for l in range(DEPTH)]
    mix = _prep_mix(w_in, w_out, gla_w_gk, gla_b_gk, gla_norm_g, s5_w_glu, s5_b_glu)
    s5 = _s5_prepare(s5_a_re, s5_a_im, s5_log_step, s5_b_re, s5_b_im, s5_c_re, s5_c_im, s5_d)
    rwkv = _prep_rwkv(rwkv_mu, rwkv_w_r, rwkv_w_k, rwkv_w_v, rwkv_w_o, rwkv_w0, rwkv_w1, rwkv_w2,
                      rwkv_a0, rwkv_a1, rwkv_a2, rwkv_g1, rwkv_g2, rwkv_k_k, rwkv_k_a, rwkv_r_k,
                      rwkv_lnx_g, rwkv_lnx_b)

    def lnp(layer, idx):
        return _row(ln_g[layer, idx]), _row(ln_b[layer, idx])

    x = x_prompt
    x = _ffn_prompt(x, mods_p, 0, 0, *ffn1[0], *lnp(0, 0))
    x, gla_p, s5_re_p, s5_im_p = _mix_prompt(x, mods_p, 0, mix, s5, *lnp(0, 1))
    x = _ffn_prompt(x, mods_p, 0, 6, *ffn2[0], *lnp(0, 2))
    x = _ffn_prompt(x, mods_p, 1, 0, *ffn1[1], *lnp(1, 0))
    x, shift_p, wkv_p = _rwkv_prompt(x, mods_p, 1, rwkv, *lnp(1, 1))
    y_prompt = _ffn_prompt(x, mods_p, 1, 6, *ffn2[1], *lnp(1, 2))

    xs = x_sample.reshape(ns, D_MODEL)
    xs = _ffn_sample(xs, mods_s, 0, 0, *ffn1[0], *lnp(0, 0))
    xs, gla_s, s5_re_s, s5_im_s = _mix_sample(xs, mods_s, 0, state_gla, state_s5_re, state_s5_im,
                                              mix, s5, *lnp(0, 1))
    xs = _ffn_sample(xs, mods_s, 0, 6, *ffn2[0], *lnp(0, 2))
    xs = _ffn_sample(xs, mods_s, 1, 0, *ffn1[1], *lnp(1, 0))
    xs, shift_s, wkv_s = _rwkv_sample(xs, mods_s, 1, state_rwkv_shift, state_rwkv_wkv, rwkv, *lnp(1, 1))
    xs = _ffn_sample(xs, mods_s, 1, 6, *ffn2[1], *lnp(1, 2))
    y_sample = xs.reshape(ns, 1, D_MODEL)

    return (y_prompt, y_sample, gla_p, s5_re_p, s5_im_p, shift_p, wkv_p,
            gla_s, s5_re_s, s5_im_s, shift_s, wkv_s)
```

```python
import functools
import math

import jax
import jax.numpy as jnp
from jax import lax
from jax.experimental import pallas as pl
from jax.experimental.pallas import tpu as pltpu

F32 = jnp.float32
BF16 = jnp.bfloat16

D_MODEL = 1024
DEPTH = 2
DN_ALPHA = (2 * DEPTH) ** 0.25
LN_EPS = 1e-5
RMS_EPS = 1e-5
FFN_RES = 0.5
D_FF = 2752
N_MODS = 9

GLA_HEADS = 4
GLA_DK = 64
GLA_DV = 128
GLA_KEY = GLA_HEADS * GLA_DK
GLA_VAL = GLA_HEADS * GLA_DV
GLA_GATE_RANK = 16
GLA_GATE_NORM = 16.0

S5_GROUP = 16
S5_WIDTH = 512
S5_GROUPS = S5_WIDTH // S5_GROUP
S5_STATE = 64
S5_CH = S5_GROUPS * S5_STATE

RWKV_HEAD = 64
RWKV_HEADS = D_MODEL // RWKV_HEAD
RWKV_LNX_EPS = 64e-5
NORM_EPS = 1e-12

LANES = 128
SUBLANES = 8
MXU_DIM = 256
VMEM_LIMIT = 58 * 1024 * 1024

CHUNK = 64
HEADS_PER_GROUP = MXU_DIM // RWKV_HEAD
N_GROUPS = RWKV_HEADS // HEADS_PER_GROUP
D_FF_PAD = 2816
FF_CHUNKS = ((0, 1024), (1024, 2048), (2048, D_FF_PAD))
LORA_PAD = 128
GATE_LORA_PAD = 256
PERM_STRIDE = CHUNK + SUBLANES
SAMPLE_BLOCK = 8


def _dot(a, b):
    return jnp.dot(a, b, preferred_element_type=F32)


def _dot_nt(a, b):
    return lax.dot_general(a, b, (((1,), (1,)), ((), ())), preferred_element_type=F32)


def _bdot(a, w_ref_or_val):
    return _dot(a.astype(BF16), w_ref_or_val)


def _sigmoid(x):
    return 1.0 / (1.0 + jnp.exp(-x))


def _silu(x):
    return x * _sigmoid(x)


def _softplus(x):
    return jnp.maximum(x, 0.0) + jnp.log(1.0 + jnp.exp(-jnp.abs(x)))


def _gelu_tanh(x):
    c = math.sqrt(2.0 / math.pi)
    return 0.5 * x * (1.0 + jnp.tanh(c * (x + 0.044715 * (x * x * x))))


def _layer_norm(x, g, b):
    mu = jnp.mean(x, axis=-1, keepdims=True)
    d = x - mu
    var = jnp.mean(d * d, axis=-1, keepdims=True)
    return d * lax.rsqrt(var + LN_EPS) * g + b


def _split3(x):
    hi = x.astype(BF16)
    r1 = x - hi.astype(F32)
    mid = r1.astype(BF16)
    lo = (r1 - mid.astype(F32)).astype(BF16)
    return hi, mid, lo


def _exact_dot_left01(m01, x):
    hi, mid, lo = _split3(x)
    return _dot(m01, hi) + _dot(m01, mid) + _dot(m01, lo)


def _dot_right01(x, m01):
    hi = x.astype(BF16)
    lo = (x - hi.astype(F32)).astype(BF16)
    return _dot(hi, m01) + _dot(lo, m01)


def _seg_ones(seg):
    r = lax.broadcasted_iota(jnp.int32, (LANES, LANES), 0) // seg
    c = lax.broadcasted_iota(jnp.int32, (LANES, LANES), 1) // seg
    return jnp.where(r == c, 1.0, 0.0).astype(BF16)


def _seg_sum(x, seg):
    ones = _seg_ones(seg)
    tiles = [
        _dot_right01(x[:, i * LANES:(i + 1) * LANES], ones)
        for i in range(x.shape[1] // LANES)
    ]
    return jnp.concatenate(tiles, axis=1)


def _tril_ones(n):
    r = lax.broadcasted_iota(jnp.int32, (n, n), 0)
    c = lax.broadcasted_iota(jnp.int32, (n, n), 1)
    return jnp.where(r >= c, 1.0, 0.0).astype(BF16)


def _chunk_cumsum(x, n_batch):
    tri = _tril_ones(CHUNK)
    parts = [_exact_dot_left01(tri, x[b * CHUNK:(b + 1) * CHUNK, :]) for b in range(n_batch)]
    return jnp.concatenate(parts, axis=0)


def _head_stack(z, n_heads, width):
    head = lax.broadcasted_iota(jnp.int32, z.shape, 1) // width
    return jnp.concatenate([jnp.where(head == h, z, 0.0) for h in range(n_heads)], axis=0)


def _block_diag_mask(rows, cols, rblk, cblk):
    r = lax.broadcasted_iota(jnp.int32, (rows, cols), 0) // rblk
    c = lax.broadcasted_iota(jnp.int32, (rows, cols), 1) // cblk
    return r == c


def _full_spec(shape, grid_rank):
    zeros = (0,) * len(shape)
    if grid_rank == 1:
        return pl.BlockSpec(shape, lambda i: zeros, pipeline_mode=pl.Buffered(1))
    return pl.BlockSpec(shape, lambda i, j: zeros, pipeline_mode=pl.Buffered(1))


def _params(semantics):
    return pltpu.CompilerParams(dimension_semantics=semantics, vmem_limit_bytes=VMEM_LIMIT)


ADA_TN = 1152


def _ada_body(cs_ref, cp_ref, w_ref, b_ref, os_ref, op_ref):
    w = w_ref[...].astype(BF16)
    os_ref[...] = _bdot(_silu(cs_ref[...]), w) + b_ref[...]
    op_ref[...] = _bdot(_silu(cp_ref[...]), w) + b_ref[...]


def _ada_mods(c_sample, c_prompt, ada_w, ada_b):
    ns, bp = c_sample.shape[0], c_prompt.shape[0]
    width = N_MODS * D_MODEL
    return pl.pallas_call(
        _ada_body,
        grid=(DEPTH, width // ADA_TN),
        in_specs=[
            pl.BlockSpec((ns, D_MODEL), lambda l, j: (0, 0)),
            pl.BlockSpec((bp, D_MODEL), lambda l, j: (0, 0)),
            pl.BlockSpec((None, D_MODEL, ADA_TN), lambda l, j: (l, 0, j)),
            pl.BlockSpec((None, 1, ADA_TN), lambda l, j: (l, 0, j)),
        ],
        out_specs=[pl.BlockSpec((None, ns, ADA_TN), lambda l, j: (l, 0, j)),
                   pl.BlockSpec((None, bp, ADA_TN), lambda l, j: (l, 0, j))],
        out_shape=[jax.ShapeDtypeStruct((DEPTH, ns, width), F32),
                   jax.ShapeDtypeStruct((DEPTH, bp, width), F32)],
        compiler_params=_params(("arbitrary", "arbitrary")),
        name="ada_mods",
    )(c_sample, c_prompt, ada_w, ada_b.reshape(DEPTH, 1, width))


FFN_TM = 512


def _ffn_body(x_ref, sh_ref, sc_ref, gt_ref, wg_ref, wu_ref, wd_ref, g_ref, b_ref, o_ref):
    x = x_ref[...]
    h = (x * (1.0 + sc_ref[...]) + sh_ref[...]).astype(BF16)
    acc = None
    for lo, hi in FF_CHUNKS:
        g = _dot_nt(h, wg_ref[lo:hi, :])
        u = _dot_nt(h, wu_ref[lo:hi, :])
        y = _bdot(_silu(g) * u, wd_ref[lo:hi, :])
        acc = y if acc is None else acc + y
    r = FFN_RES * (1.0 + gt_ref[...]) * acc
    o_ref[...] = _layer_norm(DN_ALPHA * x + r, g_ref[...], b_ref[...])


def _mod_specs_prompt(layer, first):
    return [
        pl.BlockSpec((None, None, 1, D_MODEL), functools.partial(
            lambda b, i, col: (layer, b, 0, col), col=first + k))
        for k in range(3)
    ]


def _ffn_prompt(x, mods_p, layer, first, wg, wu, wd, ln_g, ln_b):
    bsz, seq, _ = x.shape
    w_specs = [_full_spec(wg.shape, 2), _full_spec(wu.shape, 2), _full_spec(wd.shape, 2),
               _full_spec((1, D_MODEL), 2), _full_spec((1, D_MODEL), 2)]
    return pl.pallas_call(
        _ffn_body,
        grid=(bsz, seq // FFN_TM),
        in_specs=[pl.BlockSpec((None, FFN_TM, D_MODEL), lambda b, i: (b, i, 0))]
        + _mod_specs_prompt(layer, first) + w_specs,
        out_specs=pl.BlockSpec((None, FFN_TM, D_MODEL), lambda b, i: (b, i, 0)),
        out_shape=jax.ShapeDtypeStruct(x.shape, F32),
        compiler_params=_params(("arbitrary", "arbitrary")),
        name="ffn_prompt",
    )(x, mods_p, mods_p, mods_p, wg, wu, wd, ln_g, ln_b)


def _mod_specs_sample(layer, first, rows):
    return [
        pl.BlockSpec((None, rows, D_MODEL), functools.partial(
            lambda i, col: (layer, 0, col), col=first + k))
        for k in range(3)
    ]


def _ffn_sample(x, mods_s, layer, first, wg, wu, wd, ln_g, ln_b):
    n = x.shape[0]
    w_specs = [_full_spec(wg.shape, 1), _full_spec(wu.shape, 1), _full_spec(wd.shape, 1),
               _full_spec((1, D_MODEL), 1), _full_spec((1, D_MODEL), 1)]
    return pl.pallas_call(
        _ffn_body,
        grid=(1,),
        in_specs=[_full_spec((n, D_MODEL), 1)] + _mod_specs_sample(layer, first, n) + w_specs,
        out_specs=_full_spec((n, D_MODEL), 1),
        out_shape=jax.ShapeDtypeStruct(x.shape, F32),
        compiler_params=_params(("arbitrary",)),
        name="ffn_sample",
    )(x, mods_s, mods_s, mods_s, wg, wu, wd, ln_g, ln_b)


def _rwkv_project(h, prev, mu_ref, wr, wk, wv, w0, w1, w2, a0, a1, a2, kk_ref, ka_ref):
    xx = prev - h
    mu = mu_ref[...]

    def mix(i):
        return h + xx * mu[i:i + 1, :]

    r = _bdot(mix(0), wr[...])
    k = _bdot(mix(2), wk[...])
    v = _bdot(mix(3), wv[...])
    wl = w0[...] + _bdot(jnp.tanh(_bdot(mix(1), w1[...])), w2[...])
    log_decay = -math.exp(-0.5) * _sigmoid(wl)
    a_sig = _sigmoid(a0[...] + _bdot(_bdot(mix(4), a1[...]), a2[...]))
    kk = k * kk_ref[...]
    norm = jnp.sqrt(_seg_sum(kk * kk, RWKV_HEAD))
    kk = kk / jnp.maximum(norm, NORM_EPS)
    k2 = k * (1.0 + (a_sig - 1.0) * ka_ref[...])
    return r, log_decay, k2, v, -kk, kk * a_sig, mix(5).astype(BF16)


def _rwkv_bonus(r, k2, v, rk_ref):
    return _seg_sum(r * k2 * rk_ref[...], RWKV_HEAD) * v


def _rwkv_output(y, bonus, xg, x, gt, g1, g2, wo, lng, lnb, ln_g, ln_b):
    gate = _bdot(_sigmoid(_dot(xg, g1[...])), g2[...])
    inv_n = 1.0 / RWKV_HEAD
    mu_y = _seg_sum(y, RWKV_HEAD) * inv_n
    d = y - mu_y
    var = _seg_sum(d * d, RWKV_HEAD) * inv_n
    y = d * lax.rsqrt(var + RWKV_LNX_EPS) * lng[...] + lnb[...]
    out = _bdot((y + bonus) * gate, wo[...])
    return _layer_norm(DN_ALPHA * x + (1.0 + gt) * out, ln_g[...], ln_b[...])


def _rwkv_weight_list(p):
    return [p["mu"], p["wr"], p["wk"], p["wv"], p["w0"], p["w1"], p["w2"], p["a0"], p["a1"], p["a2"],
            p["k_k"], p["k_a"], p["r_k"]]


def _rwkv_out_weight_list(p, ln_g, ln_b):
    return [p["g1"], p["g2"], p["wo"], p["lnx_g"], p["lnx_b"], ln_g, ln_b]


NEUMANN_LEVELS = int(math.log2(CHUNK)) - 1
WKV_BATCH_UNROLL = 2


def _wkv_chunks(problems):
    c = CHUNK
    hp = HEADS_PER_GROUP
    w = hp * RWKV_HEAD
    n = hp * c
    t_idx = lax.broadcasted_iota(jnp.int32, (c, w), 0)
    s_idx = lax.broadcasted_iota(jnp.int32, (c, w), 1) % c
    strict = t_idx > s_idx
    incl = t_idx >= s_idx
    eye = jnp.where(t_idx == s_idx, 1.0, 0.0)
    bd_mask = _block_diag_mask(w, w, RWKV_HEAD, RWKV_HEAD)
    idx = range(len(problems))
    At, Rt, Kt, Bt, Kh, Bh, V, S0, dec = zip(*problems)

    def bf(x):
        return x.astype(BF16)

    def stack(x, width):
        return bf(_head_stack(x, hp, width))

    X2 = [jnp.concatenate([At[i], Rt[i]], axis=0) for i in idx]
    KB = [jnp.concatenate([stack(Kt[i], RWKV_HEAD), stack(Bt[i], RWKV_HEAD)], axis=0) for i in idx]
    sc = [_dot_nt(X2[i], KB[i]) for i in idx]
    XS = [_dot_nt(X2[i], bf(S0[i])) for i in idx]
    Vbd = [stack(V[i], RWKV_HEAD) for i in idx]
    Aab = [jnp.where(strict, sc[i][:c, n:], 0.0) for i in idx]
    rhs = [XS[i][:c] + _dot(bf(jnp.where(strict, sc[i][:c, :n], 0.0)), Vbd[i]) for i in idx]

    Q = Aab
    P = [eye + Aab[i] for i in idx]
    Qbd = [stack(Q[i], c) for i in idx]
    for _ in range(NEUMANN_LEVELS):
        Q = [_dot(bf(Q[i]), Qbd[i]) for i in idx]
        Qbd = [stack(Q[i], c) for i in idx]
        P = [P[i] + _dot(bf(P[i]), Qbd[i]) for i in idx]
    U = [_dot(bf(P[i]), stack(rhs[i], RWKV_HEAD)) for i in idx]

    Y = [XS[i][c:] + _dot(bf(jnp.where(incl, sc[i][c:, :n], 0.0)), Vbd[i])
         + _dot(bf(jnp.where(incl, sc[i][c:, n:], 0.0)), stack(U[i], RWKV_HEAD)) for i in idx]

    upd = [_dot(bf(jnp.concatenate([V[i], U[i]], axis=0).T), jnp.concatenate([Kh[i], Bh[i]], axis=0))
           for i in idx]
    S1 = [S0[i] * dec[i] + jnp.where(bd_mask, upd[i], 0.0) for i in idx]
    return Y, S1


def _rwkv_prompt_body(x_ref, sh_ref, sc_ref, gt_ref,
                      mu, wr, wk, wv, w0, w1, w2, a0, a1, a2, kk_ref, ka_ref, rk_ref,
                      g1, g2, wo, lng, lnb, ln_g, ln_b,
                      o_ref, shift_ref, state_ref,
                      carry, s_at, s_rt, s_kt, s_bt, s_kh, s_bh, s_v, s_dec, s_y, s_bonus, s_xg):
    nb = x_ref.shape[0]
    rows = nb * CHUNK
    gw = HEADS_PER_GROUP * RWKV_HEAD
    step = pl.program_id(0)

    @pl.when(step == 0)
    def _():
        carry[...] = jnp.zeros(carry.shape, F32)
        state_ref[...] = jnp.zeros(state_ref.shape, F32)

    h3 = x_ref[...] * (1.0 + sc_ref[...]) + sh_ref[...]
    h = h3.reshape(rows, D_MODEL)
    first = lax.broadcasted_iota(jnp.int32, (rows, D_MODEL), 0) % CHUNK == 0
    carried = jnp.broadcast_to(carry[...], (nb, CHUNK, D_MODEL)).reshape(rows, D_MODEL)
    prev = jnp.where(first, carried, pltpu.roll(h, 1, axis=0))
    carry[...] = h3[:, CHUNK - 1:CHUNK, :]
    shift_ref[...] = h3[:, CHUNK - 1:CHUNK, :]

    r, lw, k2, v, av, bv, xg = _rwkv_project(
        h, prev, mu, wr, wk, wv, w0, w1, w2, a0, a1, a2, kk_ref, ka_ref)
    s_xg[...] = xg
    s_bonus[...] = _rwkv_bonus(r, k2, v, rk_ref)

    g = _chunk_cumsum(lw, nb)
    g3 = g.reshape(nb, CHUNK, D_MODEL)
    g_last = jnp.broadcast_to(g3[:, CHUNK - 1:CHUNK, :], (nb, CHUNK, D_MODEL)).reshape(rows, D_MODEL)
    e_neg = jnp.exp(-g)
    e_tail = jnp.exp(g_last - g)
    arrays = (
        (s_at, av * jnp.exp(g - lw)),
        (s_rt, r * jnp.exp(g)),
        (s_kt, k2 * e_neg),
        (s_bt, bv * e_neg),
        (s_kh, k2 * e_tail),
        (s_bh, bv * e_tail),
        (s_v, v),
    )
    for ref, val in arrays:
        for gi in range(N_GROUPS):
            ref[gi] = val[:, gi * gw:(gi + 1) * gw].astype(ref.dtype)
    dec = jnp.exp(g3[:, CHUNK - 1:CHUNK, :])
    for gi in range(N_GROUPS):
        s_dec[gi] = dec[:, :, gi * gw:(gi + 1) * gw]

    def chunk_step(it, carry_val):
        keys = []
        for db in range(WKV_BATCH_UNROLL):
            b = it * WKV_BATCH_UNROLL + db
            sl = pl.ds(pl.multiple_of(b * CHUNK, CHUNK), CHUNK)
            keys += [(b, gi, sl) for gi in range(N_GROUPS)]
        Y, S1 = _wkv_chunks([
            (s_at[gi, sl, :], s_rt[gi, sl, :], s_kt[gi, sl, :], s_bt[gi, sl, :],
             s_kh[gi, sl, :], s_bh[gi, sl, :], s_v[gi, sl, :], state_ref[b, gi], s_dec[gi, b])
            for b, gi, sl in keys])
        for (b, gi, sl), y_val, s_val in zip(keys, Y, S1):
            s_y[gi, sl, :] = y_val
            state_ref[b, gi] = s_val
        return carry_val

    lax.fori_loop(0, nb // WKV_BATCH_UNROLL, chunk_step, 0)

    y = jnp.concatenate([s_y[gi] for gi in range(N_GROUPS)], axis=1)
    gt = jnp.broadcast_to(gt_ref[...], (nb, CHUNK, D_MODEL)).reshape(rows, D_MODEL)
    x = x_ref[...].reshape(rows, D_MODEL)
    out = _rwkv_output(y, s_bonus[...], s_xg[...], x, gt, g1, g2, wo, lng, lnb, ln_g, ln_b)
    o_ref[...] = out.reshape(nb, CHUNK, D_MODEL)


def _rwkv_prompt(x, mods_p, layer, p, ln_g, ln_b):
    bsz, seq, _ = x.shape
    rows = bsz * CHUNK
    gw = HEADS_PER_GROUP * RWKV_HEAD
    weights = _rwkv_weight_list(p) + _rwkv_out_weight_list(p, ln_g, ln_b)
    mod_specs = [
        pl.BlockSpec((None, bsz, 1, D_MODEL), functools.partial(lambda i, col: (layer, 0, 0, col), col=3 + k))
        for k in range(3)
    ]
    slab = pltpu.VMEM((N_GROUPS, rows, gw), F32)
    slab16 = pltpu.VMEM((N_GROUPS, rows, gw), BF16)
    out, shift, state = pl.pallas_call(
        _rwkv_prompt_body,
        grid=(seq // CHUNK,),
        in_specs=[pl.BlockSpec((bsz, CHUNK, D_MODEL), lambda i: (0, i, 0))] + mod_specs
        + [_full_spec(w.shape, 1) for w in weights],
        out_specs=[
            pl.BlockSpec((bsz, CHUNK, D_MODEL), lambda i: (0, i, 0)),
            _full_spec((bsz, 1, D_MODEL), 1),
            _full_spec((bsz, N_GROUPS, gw, gw), 1),
        ],
        out_shape=[
            jax.ShapeDtypeStruct(x.shape, F32),
            jax.ShapeDtypeStruct((bsz, 1, D_MODEL), F32),
            jax.ShapeDtypeStruct((bsz, N_GROUPS, gw, gw), F32),
        ],
        scratch_shapes=[pltpu.VMEM((bsz, 1, D_MODEL), F32), slab16, slab16, slab, slab, slab16, slab16, slab]
        + [pltpu.VMEM((N_GROUPS, bsz, 1, gw), F32), slab,
           pltpu.VMEM((rows, D_MODEL), F32), pltpu.VMEM((rows, D_MODEL), BF16)],
        compiler_params=_params(("arbitrary",)),
        name="rwkv_prompt",
    )(x, mods_p, mods_p, mods_p, *weights)
    s6 = state.reshape(bsz, N_GROUPS, HEADS_PER_GROUP, RWKV_HEAD, HEADS_PER_GROUP, RWKV_HEAD)
    wkv = jnp.stack([s6[:, :, hh, :, hh, :] for hh in range(HEADS_PER_GROUP)], axis=2)
    return out, shift.reshape(bsz, D_MODEL), wkv.reshape(bsz, RWKV_HEADS, RWKV_HEAD, RWKV_HEAD)


def _rwkv_sample_pre_body(x_ref, sh_ref, sc_ref, prev_ref,
                          mu, wr, wk, wv, w0, w1, w2, a0, a1, a2, kk_ref, ka_ref, rk_ref,
                          h_ref, rt_ref, wt_ref, kt_ref, vt_ref, at_ref, bt_ref, xg_ref, bonus_ref):
    h = x_ref[...] * (1.0 + sc_ref[...]) + sh_ref[...]
    r, lw, k2, v, av, bv, xg = _rwkv_project(
        h, prev_ref[...], mu, wr, wk, wv, w0, w1, w2, a0, a1, a2, kk_ref, ka_ref)
    h_ref[...] = h
    rt_ref[...] = r.T
    wt_ref[...] = jnp.exp(lw).T
    kt_ref[...] = k2.T
    vt_ref[...] = v.T
    at_ref[...] = av.T
    bt_ref[...] = bv.T
    xg_ref[...] = xg
    bonus_ref[...] = _rwkv_bonus(r, k2, v, rk_ref)


def _rwkv_sample_step_body(s_ref, r_ref, w_ref, k_ref, v_ref, a_ref, b_ref, so_ref, y_ref):
    a, w, b, k, r = a_ref[...], w_ref[...], b_ref[...], k_ref[...], r_ref[...]

    def row_step(i, carry_val):
        S = s_ref[i]
        sa = jnp.sum(S * a, axis=0, keepdims=True)
        S1 = S * w + sa * b + v_ref[pl.ds(i, 1), :] * k
        so_ref[i] = S1
        y_ref[pl.ds(i, 1), :] = jnp.sum(S1 * r, axis=0, keepdims=True)
        return carry_val

    lax.fori_loop(0, RWKV_HEAD, row_step, 0, unroll=2)


def _rwkv_sample_post_body(yt_ref, bonus_ref, xg_ref, x_ref, gt_ref,
                           g1, g2, wo, lng, lnb, ln_g, ln_b, o_ref):
    o_ref[...] = _rwkv_output(yt_ref[...].T, bonus_ref[...], xg_ref[...], x_ref[...], gt_ref[...],
                              g1, g2, wo, lng, lnb, ln_g, ln_b)


def _to_columns(a):
    n, f = a.shape
    return a.reshape(n // SAMPLE_BLOCK, SAMPLE_BLOCK, f).transpose(0, 2, 1)


def _rwkv_sample(x, mods_s, layer, s_shift, s_wkv, p, ln_g, ln_b):
    n = x.shape[0]
    row = jax.ShapeDtypeStruct((n, D_MODEL), F32)
    row_spec = _full_spec((n, D_MODEL), 1)
    pre_w = _rwkv_weight_list(p)
    ms = _mod_specs_sample(layer, 3, n)
    col = jax.ShapeDtypeStruct((D_MODEL, n), F32)
    col_spec = _full_spec((D_MODEL, n), 1)
    h, *vecs, xg, bonus = pl.pallas_call(
        _rwkv_sample_pre_body,
        grid=(1,),
        in_specs=[row_spec, ms[0], ms[1], row_spec] + [_full_spec(a.shape, 1) for a in pre_w],
        out_specs=[row_spec] + [col_spec] * 6 + [row_spec] * 2,
        out_shape=[row] + [col] * 6 + [jax.ShapeDtypeStruct((n, D_MODEL), BF16), row],
        compiler_params=_params(("arbitrary",)),
        name="rwkv_sample_pre",
    )(x, mods_s, mods_s, s_shift, *pre_w)

    st_spec = pl.BlockSpec((None, RWKV_HEAD, RWKV_HEAD, n), lambda hd: (hd, 0, 0, 0))
    vec_spec = pl.BlockSpec((RWKV_HEAD, n), lambda hd: (hd, 0))
    st_new, yt = pl.pallas_call(
        _rwkv_sample_step_body,
        grid=(RWKV_HEADS,),
        in_specs=[st_spec] + [vec_spec] * len(vecs),
        out_specs=[st_spec, vec_spec],
        out_shape=[jax.ShapeDtypeStruct((RWKV_HEADS, RWKV_HEAD, RWKV_HEAD, n), F32), col],
        compiler_params=_params(("arbitrary",)),
        name="rwkv_sample_step",
    )(jnp.transpose(s_wkv, (1, 2, 3, 0)), *vecs)
    s_new = jnp.transpose(st_new, (3, 0, 1, 2))

    post_w = _rwkv_out_weight_list(p, ln_g, ln_b)
    out = pl.pallas_call(
        _rwkv_sample_post_body,
        grid=(1,),
        in_specs=[col_spec] + [row_spec] * 3 + [ms[2]] + [_full_spec(a.shape, 1) for a in post_w],
        out_specs=row_spec,
        out_shape=row,
        compiler_params=_params(("arbitrary",)),
        name="rwkv_sample_post",
    )(yt, bonus, xg, x, mods_s, *post_w)
    return out, h, s_new


def _s5_prep_body(are_ref, aim_ref, ls_ref, bre_ref, bim_ref, abre_ref, abim_ref, bbre_ref, bbim_ref):
    a_re = are_ref[...]
    a_im = aim_ref[...]
    dt = jnp.exp(ls_ref[...])
    mag = jnp.exp(a_re * dt)
    ab_re = mag * jnp.cos(a_im * dt)
    ab_im = mag * jnp.sin(a_im * dt)
    den = a_re * a_re + a_im * a_im
    nr = ab_re - 1.0
    z_re = (nr * a_re + ab_im * a_im) / den
    z_im = (ab_im * a_re - nr * a_im) / den
    b_re = bre_ref[...]
    b_im = bim_ref[...]
    abre_ref[...] = ab_re
    abim_ref[...] = ab_im
    bbre_ref[...] = z_re * b_re - z_im * b_im
    bbim_ref[...] = z_re * b_im + z_im * b_re


def _s5_prepare(s5_a_re, s5_a_im, s5_log_step, s5_b_re, s5_b_im, s5_c_re, s5_c_im, s5_d):
    G, P, C = S5_GROUPS, S5_STATE, S5_GROUP
    small = jax.ShapeDtypeStruct((G, 1, P), F32)
    big = jax.ShapeDtypeStruct((G, C, P), F32)
    ls = jnp.broadcast_to(s5_log_step.reshape(G, 1, 1), (G, 1, P))
    ab_re, ab_im, bb_re, bb_im = pl.pallas_call(
        _s5_prep_body,
        grid=(1,),
        in_specs=[_full_spec((G, 1, P), 1)] * 3 + [_full_spec((G, C, P), 1)] * 2,
        out_specs=[_full_spec((G, 1, P), 1)] * 2 + [_full_spec((G, C, P), 1)] * 2,
        out_shape=[small, small, big, big],
        name="s5_prepare",
    )(s5_a_re.reshape(G, 1, P), s5_a_im.reshape(G, 1, P), ls,
      s5_b_re.transpose(0, 2, 1), s5_b_im.transpose(0, 2, 1))

    eye = jnp.eye(G, dtype=F32)
    gpt = LANES // C
    n_tiles = G // gpt

    def in_blocks(bb):
        full = (bb[:, :, None, :] * eye[:, None, :, None]).reshape(G * C, G * P)
        return jnp.stack([full[k * LANES:(k + 1) * LANES, k * gpt * P:(k + 1) * gpt * P]
                          for k in range(n_tiles)]).astype(BF16)

    def out_blocks(cc):
        full = (cc.transpose(0, 2, 1)[:, :, None, :] * eye[:, None, :, None]).reshape(G * P, G * C)
        return jnp.stack([full[k * gpt * P:(k + 1) * gpt * P, k * LANES:(k + 1) * LANES]
                          for k in range(n_tiles)]).astype(BF16)

    return dict(
        ab_re=ab_re.reshape(1, G * P), ab_im=ab_im.reshape(1, G * P),
        bb_re=in_blocks(bb_re), bb_im=in_blocks(bb_im),
        c_re=out_blocks(s5_c_re), c_im=out_blocks(s5_c_im),
        d=s5_d.reshape(1, G * C),
    )


def _s5_input(u, bb_re_ref, bb_im_ref):
    n_tiles = bb_re_ref.shape[0]
    ub = u.astype(BF16)
    re = [_dot(ub[:, k * LANES:(k + 1) * LANES], bb_re_ref[k]) for k in range(n_tiles)]
    im = [_dot(ub[:, k * LANES:(k + 1) * LANES], bb_im_ref[k]) for k in range(n_tiles)]
    return jnp.concatenate(re, axis=1), jnp.concatenate(im, axis=1)


def _s5_output(h_re, h_im, c_re_ref, c_im_ref):
    n_tiles = c_re_ref.shape[0]
    w = h_re.shape[1] // n_tiles
    hr = h_re.astype(BF16)
    hi = h_im.astype(BF16)
    ys = [_dot(hr[:, k * w:(k + 1) * w], c_re_ref[k]) - _dot(hi[:, k * w:(k + 1) * w], c_im_ref[k])
          for k in range(n_tiles)]
    return jnp.concatenate(ys, axis=1)


C_Q, C_K, C_V, C_G, C_U, C_GK, C_END = 0, 256, 512, 1024, 1536, 2048, 2176


def _mix_project(h, w_in, w_gk, b_gk):
    p = _bdot(h, w_in[...])
    q = p[:, C_Q:C_K] * (GLA_DK ** -0.5)
    k = p[:, C_K:C_V]
    v = p[:, C_V:C_G]
    gg = p[:, C_G:C_U]
    u = p[:, C_U:C_GK]
    z = _bdot(p[:, C_GK:C_END], w_gk[...]) + b_gk[...]
    gk = -_softplus(-z) * (1.0 / GLA_GATE_NORM)
    return q, k, v, gg, u, gk


def _gla_finish(o, gg, norm_g):
    parts = []
    for hh in range(GLA_HEADS):
        oh = o[:, hh * GLA_DV:(hh + 1) * GLA_DV]
        parts.append(oh * lax.rsqrt(jnp.mean(oh * oh, axis=-1, keepdims=True) + RMS_EPS) * norm_g[...])
    return jnp.concatenate(parts, axis=1) * _silu(gg)


def _mix_output(o_gla, y_s5, x, gt, w_glu, b_glu, w_out, ln_g, ln_b):
    z = _gelu_tanh(y_s5)
    o_s5 = z * _sigmoid(_bdot(z, w_glu[...]) + b_glu[...])
    out = _bdot(o_gla, w_out[0:GLA_VAL, :]) + _bdot(o_s5, w_out[GLA_VAL:GLA_VAL + S5_WIDTH, :])
    return _layer_norm(DN_ALPHA * x + (1.0 + gt) * out, ln_g[...], ln_b[...])


def _gla_chunk(q, k, v, bc, ST):
    c = CHUNK
    mid = c // 2 - 1
    b_mid = bc[mid:mid + 1, :]
    b_last = bc[c - 1:c, :]
    q_in = q * jnp.exp(bc - b_mid)
    k_in = k * jnp.exp(b_mid - bc)
    q_full = q * jnp.exp(bc)
    k_st = k * jnp.exp(b_last - bc)
    t_idx = lax.broadcasted_iota(jnp.int32, (c, GLA_KEY), 0)
    s_idx = lax.broadcasted_iota(jnp.int32, (c, GLA_KEY), 1) % c
    scores = _dot_nt(q_in.astype(BF16), _head_stack(k_in, GLA_HEADS, GLA_DK).astype(BF16))
    scores = jnp.where(t_idx >= s_idx, scores, 0.0)
    o = _dot(scores.astype(BF16), _head_stack(v, GLA_HEADS, GLA_DV).astype(BF16))
    o = o + _dot_nt(q_full.astype(BF16), ST.astype(BF16))
    upd = _dot(v.T.astype(BF16), k_st.astype(BF16))
    ST1 = ST * jnp.exp(b_last) + jnp.where(
        _block_diag_mask(GLA_VAL, GLA_KEY, GLA_DV, GLA_DK), upd, 0.0)
    return o, ST1


def _mix_prompt_body(x_ref, sh_ref, sc_ref, gt_ref,
                     w_in, w_gk, b_gk, norm_g, ab_re_ref, ab_im_ref, bb_re, bb_im, c_re, c_im, d_ref,
                     w_glu, b_glu, w_out, ln_g, ln_b,
                     o_ref, gla_ref, hre_ref, him_ref,
                     s_q, s_k, s_v, s_bc, s_o, s_perm, s_re, s_im):
    nb = x_ref.shape[0]
    rows = nb * CHUNK
    step = pl.program_id(0)

    @pl.when(step == 0)
    def _():
        gla_ref[...] = jnp.zeros(gla_ref.shape, F32)
        hre_ref[...] = jnp.zeros(hre_ref.shape, F32)
        him_ref[...] = jnp.zeros(him_ref.shape, F32)

    x3 = x_ref[...]
    h3 = x3 * (1.0 + sc_ref[...]) + sh_ref[...]
    x = x3.reshape(rows, D_MODEL)
    h = h3.reshape(rows, D_MODEL)
    q, k, v, gg, u, gk = _mix_project(h, w_in, w_gk, b_gk)

    s_q[...] = q
    s_k[...] = k
    s_v[...] = v
    s_bc[...] = _chunk_cumsum(gk, nb)

    def gla_step(b, carry_val):
        sl = pl.ds(pl.multiple_of(b * CHUNK, CHUNK), CHUNK)
        o, ST1 = _gla_chunk(s_q[sl, :], s_k[sl, :], s_v[sl, :], s_bc[sl, :], gla_ref[b])
        s_o[sl, :] = o
        gla_ref[b] = ST1
        return carry_val

    lax.fori_loop(0, nb, gla_step, 0)
    o_gla = _gla_finish(s_o[...], gg, norm_g)

    n_lt = S5_WIDTH // LANES
    for b in range(nb):
        for lt in range(n_lt):
            s_perm[lt, b * PERM_STRIDE:b * PERM_STRIDE + CHUNK, :] = (
                u[b * CHUNK:(b + 1) * CHUNK, lt * LANES:(lt + 1) * LANES])
    u_tb = jnp.concatenate(
        [jnp.concatenate([s_perm[lt, pl.ds(t, nb, stride=PERM_STRIDE), :] for lt in range(n_lt)], axis=1)
         for t in range(CHUNK)], axis=0)
    bu_re, bu_im = _s5_input(u_tb, bb_re, bb_im)
    s_re[...] = bu_re
    s_im[...] = bu_im

    n_ch = 4
    cw = S5_CH // n_ch
    for ci in range(n_ch):
        lanes = slice(ci * cw, (ci + 1) * cw)
        a_re = jnp.broadcast_to(ab_re_ref[:, lanes], (nb, cw))
        a_im = jnp.broadcast_to(ab_im_ref[:, lanes], (nb, cw))

        def scan_step(t, hc, lanes=lanes, a_re=a_re, a_im=a_im):
            h_re, h_im = hc
            sl = pl.ds(pl.multiple_of(t * nb, nb), nb)
            n_re = a_re * h_re - a_im * h_im + s_re[sl, lanes]
            n_im = a_re * h_im + a_im * h_re + s_im[sl, lanes]
            s_re[sl, lanes] = n_re
            s_im[sl, lanes] = n_im
            return n_re, n_im

        f_re, f_im = lax.fori_loop(0, CHUNK, scan_step, (hre_ref[:, lanes], him_ref[:, lanes]))
        hre_ref[:, lanes] = f_re
        him_ref[:, lanes] = f_im

    y_tb = _s5_output(s_re[...], s_im[...], c_re, c_im)
    for t in range(CHUNK):
        for lt in range(n_lt):
            s_perm[lt, pl.ds(t, nb, stride=PERM_STRIDE), :] = (
                y_tb[t * nb:(t + 1) * nb, lt * LANES:(lt + 1) * LANES])
    y_s5 = jnp.concatenate(
        [jnp.concatenate([s_perm[lt, b * PERM_STRIDE:b * PERM_STRIDE + CHUNK, :] for lt in range(n_lt)],
                         axis=1) for b in range(nb)], axis=0)
    y_s5 = y_s5 + d_ref[...] * u

    gt = jnp.broadcast_to(gt_ref[...], (nb, CHUNK, D_MODEL)).reshape(rows, D_MODEL)
    out = _mix_output(o_gla, y_s5, x, gt, w_glu, b_glu, w_out, ln_g, ln_b)
    o_ref[...] = out.reshape(nb, CHUNK, D_MODEL)


def _mix_weight_list(m, s5):
    return [m["w_in"], m["w_gk"], m["b_gk"], m["norm_g"], s5["ab_re"], s5["ab_im"], s5["bb_re"], s5["bb_im"],
            s5["c_re"], s5["c_im"], s5["d"], m["w_glu"], m["b_glu"], m["w_out"]]


def _mix_prompt(x, mods_p, layer, m, s5, ln_g, ln_b):
    bsz, seq, _ = x.shape
    rows = bsz * CHUNK
    weights = _mix_weight_list(m, s5) + [ln_g, ln_b]
    mod_specs = [
        pl.BlockSpec((None, bsz, 1, D_MODEL), functools.partial(lambda i, col: (layer, 0, 0, col), col=3 + k))
        for k in range(3)
    ]
    out, gla_t, h_re, h_im = pl.pallas_call(
        _mix_prompt_body,
        grid=(seq // CHUNK,),
        in_specs=[pl.BlockSpec((bsz, CHUNK, D_MODEL), lambda i: (0, i, 0))] + mod_specs
        + [_full_spec(w.shape, 1) for w in weights],
        out_specs=[
            pl.BlockSpec((bsz, CHUNK, D_MODEL), lambda i: (0, i, 0)),
            _full_spec((bsz, GLA_VAL, GLA_KEY), 1),
            _full_spec((bsz, S5_CH), 1),
            _full_spec((bsz, S5_CH), 1),
        ],
        out_shape=[
            jax.ShapeDtypeStruct(x.shape, F32),
            jax.ShapeDtypeStruct((bsz, GLA_VAL, GLA_KEY), F32),
            jax.ShapeDtypeStruct((bsz, S5_CH), F32),
            jax.ShapeDtypeStruct((bsz, S5_CH), F32),
        ],
        scratch_shapes=[
            pltpu.VMEM((rows, GLA_KEY), F32), pltpu.VMEM((rows, GLA_KEY), F32),
            pltpu.VMEM((rows, GLA_VAL), F32), pltpu.VMEM((rows, GLA_KEY), F32),
            pltpu.VMEM((rows, GLA_VAL), F32),
            pltpu.VMEM((S5_WIDTH // LANES, bsz * PERM_STRIDE, LANES), F32),
            pltpu.VMEM((rows, S5_CH), F32), pltpu.VMEM((rows, S5_CH), F32),
        ],
        compiler_params=_params(("arbitrary",)),
        name="gla_s5_prompt",
    )(x, mods_p, mods_p, mods_p, *weights)
    s6 = gla_t.reshape(bsz, GLA_HEADS, GLA_DV, GLA_HEADS, GLA_DK)
    gla = jnp.stack([s6[:, hh, :, hh, :] for hh in range(GLA_HEADS)], axis=1).transpose(0, 1, 3, 2)
    return (out, gla, h_re.reshape(bsz, S5_GROUPS, S5_STATE), h_im.reshape(bsz, S5_GROUPS, S5_STATE))


def _mix_sample_pre_body(x_ref, sh_ref, sc_ref, hre_ref, him_ref,
                         w_in, w_gk, b_gk, ab_re_ref, ab_im_ref, bb_re, bb_im, c_re, c_im, d_ref,
                         q_ref, k_ref, v_ref, gg_ref, dec_ref, y_ref, nre_ref, nim_ref):
    h = x_ref[...] * (1.0 + sc_ref[...]) + sh_ref[...]
    q, k, v, gg, u, gk = _mix_project(h, w_in, w_gk, b_gk)
    q_ref[...] = q
    k_ref[...] = k
    v_ref[...] = v
    gg_ref[...] = gg
    dec_ref[...] = jnp.exp(gk)
    bu_re, bu_im = _s5_input(u, bb_re, bb_im)
    a_re = ab_re_ref[...]
    a_im = ab_im_ref[...]
    h_re = hre_ref[...]
    h_im = him_ref[...]
    n_re = a_re * h_re - a_im * h_im + bu_re
    n_im = a_re * h_im + a_im * h_re + bu_im
    nre_ref[...] = n_re
    nim_ref[...] = n_im
    y_ref[...] = _s5_output(n_re, n_im, c_re, c_im) + d_ref[...] * u


def _gla_sample_step_body(s_ref, qc_ref, kc_ref, dc_ref, v_ref, so_ref, o_ref):
    for j in range(SAMPLE_BLOCK):
        for hh in range(GLA_HEADS):
            rows = slice(hh * GLA_DK, (hh + 1) * GLA_DK)
            lanes = slice(hh * GLA_DV, (hh + 1) * GLA_DV)
            S1 = dc_ref[rows, j:j + 1] * s_ref[j, hh] + kc_ref[rows, j:j + 1] * v_ref[j:j + 1, lanes]
            so_ref[j, hh] = S1
            o_ref[j:j + 1, lanes] = jnp.sum(qc_ref[rows, j:j + 1] * S1, axis=0, keepdims=True)


def _mix_sample_post_body(o_ref_in, gg_ref, y_ref, x_ref, gt_ref, norm_g, w_glu, b_glu, w_out, ln_g, ln_b,
                          o_ref):
    o_gla = _gla_finish(o_ref_in[...], gg_ref[...], norm_g)
    o_ref[...] = _mix_output(o_gla, y_ref[...], x_ref[...], gt_ref[...], w_glu, b_glu, w_out, ln_g, ln_b)


def _mix_sample(x, mods_s, layer, s_gla, s_re, s_im, m, s5, ln_g, ln_b):
    n = x.shape[0]
    ms = _mod_specs_sample(layer, 3, n)
    row_spec = _full_spec((n, D_MODEL), 1)

    def sds(w):
        return jax.ShapeDtypeStruct((n, w), F32)

    def fs(w):
        return _full_spec((n, w), 1)

    pre_w = [m["w_in"], m["w_gk"], m["b_gk"], s5["ab_re"], s5["ab_im"], s5["bb_re"], s5["bb_im"],
             s5["c_re"], s5["c_im"], s5["d"]]
    widths = [GLA_KEY, GLA_KEY, GLA_VAL, GLA_VAL, GLA_KEY, S5_WIDTH, S5_CH, S5_CH]
    q, k, v, gg, dec, y_s5, n_re, n_im = pl.pallas_call(
        _mix_sample_pre_body,
        grid=(1,),
        in_specs=[row_spec, ms[0], ms[1], fs(S5_CH), fs(S5_CH)] + [_full_spec(a.shape, 1) for a in pre_w],
        out_specs=[fs(w) for w in widths],
        out_shape=[sds(w) for w in widths],
        compiler_params=_params(("arbitrary",)),
        name="gla_s5_sample_pre",
    )(x, mods_s, mods_s, s_re.reshape(n, S5_CH), s_im.reshape(n, S5_CH), *pre_w)

    nblk = n // SAMPLE_BLOCK
    st_spec = pl.BlockSpec((SAMPLE_BLOCK, GLA_HEADS, GLA_DK, GLA_DV), lambda i: (i, 0, 0, 0))
    col_spec = pl.BlockSpec((None, GLA_KEY, SAMPLE_BLOCK), lambda i: (i, 0, 0))
    v_spec = pl.BlockSpec((SAMPLE_BLOCK, GLA_VAL), lambda i: (i, 0))
    s_new, o = pl.pallas_call(
        _gla_sample_step_body,
        grid=(nblk,),
        in_specs=[st_spec, col_spec, col_spec, col_spec, v_spec],
        out_specs=[st_spec, v_spec],
        out_shape=[jax.ShapeDtypeStruct(s_gla.shape, F32), sds(GLA_VAL)],
        compiler_params=_params(("arbitrary",)),
        name="gla_sample_step",
    )(s_gla, _to_columns(q), _to_columns(k), _to_columns(dec), v)

    post_w = [m["norm_g"], m["w_glu"], m["b_glu"], m["w_out"], ln_g, ln_b]
    out = pl.pallas_call(
        _mix_sample_post_body,
        grid=(1,),
        in_specs=[fs(GLA_VAL), fs(GLA_VAL), fs(S5_WIDTH), row_spec, ms[2]]
        + [_full_spec(a.shape, 1) for a in post_w],
        out_specs=row_spec,
        out_shape=sds(D_MODEL),
        compiler_params=_params(("arbitrary",)),
        name="gla_s5_sample_post",
    )(o, gg, y_s5, x, mods_s, *post_w)
    return (out, s_new, n_re.reshape(n, S5_GROUPS, S5_STATE), n_im.reshape(n, S5_GROUPS, S5_STATE))


def _pad_to(a, axis, size):
    pad = [(0, 0)] * a.ndim
    pad[axis] = (0, size - a.shape[axis])
    return jnp.pad(a, pad)


def _row(a):
    return a.reshape(1, -1).astype(F32)


def _prep_ffn(wg, wu, wd):
    return (_pad_to(wg.T, 0, D_FF_PAD).astype(BF16), _pad_to(wu.T, 0, D_FF_PAD).astype(BF16),
            _pad_to(wd, 0, D_FF_PAD).astype(BF16))


def _prep_mix(w_in, w_out, gla_w_gk, gla_b_gk, gla_norm_g, s5_w_glu, s5_b_glu):
    q, k, v, g, gk_low, u = jnp.split(
        w_in, (GLA_KEY, 2 * GLA_KEY, 2 * GLA_KEY + GLA_VAL, 2 * GLA_KEY + 2 * GLA_VAL,
               2 * GLA_KEY + 2 * GLA_VAL + GLA_GATE_RANK), axis=1)
    w_in_r = jnp.concatenate([q, k, v, g, u, _pad_to(gk_low, 1, LANES)], axis=1)
    return dict(
        w_in=w_in_r.astype(BF16), w_out=w_out.astype(BF16),
        w_gk=_pad_to(gla_w_gk, 0, LANES).astype(BF16), b_gk=_row(gla_b_gk), norm_g=_row(gla_norm_g),
        w_glu=s5_w_glu.astype(BF16), b_glu=_row(s5_b_glu),
    )


def _prep_rwkv(mu, w_r, w_k, w_v, w_o, w0, w1, w2, a0, a1, a2, g1, g2, k_k, k_a, r_k, lnx_g, lnx_b):
    return dict(
        mu=_pad_to(mu, 0, SUBLANES), wr=w_r.astype(BF16), wk=w_k.astype(BF16), wv=w_v.astype(BF16),
        wo=w_o.astype(BF16), w0=_row(w0),
        w1=_pad_to(w1, 1, LORA_PAD).astype(BF16), w2=_pad_to(w2, 0, LORA_PAD).astype(BF16),
        a0=_row(a0),
        a1=_pad_to(a1, 1, LORA_PAD).astype(BF16), a2=_pad_to(a2, 0, LORA_PAD).astype(BF16),
        g1=_pad_to(g1, 1, GATE_LORA_PAD).astype(BF16), g2=_pad_to(g2, 0, GATE_LORA_PAD).astype(BF16),
        k_k=_row(k_k), k_a=_row(k_a), r_k=_row(r_k), lnx_g=_row(lnx_g), lnx_b=_row(lnx_b),
    )


def kernel(x_prompt, x_sample, state_gla, state_s5_re, state_s5_im, state_rwkv_shift, state_rwkv_wkv,
           c_prompt, c_sample, ada_w, ada_b, ln_g, ln_b,
           ffn1_wg, ffn1_wu, ffn1_wd, ffn2_wg, ffn2_wu, ffn2_wd,
           w_in, w_out, gla_w_gk, gla_b_gk, gla_norm_g,
           s5_a_re, s5_a_im, s5_log_step, s5_b_re, s5_b_im, s5_c_re, s5_c_im, s5_d, s5_w_glu, s5_b_glu,
           rwkv_mu, rwkv_w_r, rwkv_w_k, rwkv_w_v, rwkv_w_o, rwkv_w0, rwkv_w1, rwkv_w2,
           rwkv_a0, rwkv_a1, rwkv_a2, rwkv_g1, rwkv_g2, rwkv_k_k, rwkv_k_a, rwkv_r_k,
           rwkv_lnx_g, rwkv_lnx_b):
    bp = x_prompt.shape[0]
    ns = x_sample.shape[0]

    mods_s, mods_p = _ada_mods(c_sample, c_prompt, ada_w, ada_b)
    mods_p = mods_p.reshape(DEPTH, bp, 1, N_MODS * D_MODEL)

    ffn1 = [_prep_ffn(ffn1_wg[l], ffn1_wu[l], ffn1_wd[l]) for l in range(DEPTH)]
    ffn2 = [_prep_ffn(ffn2_wg[l], ffn2_wu[l], ffn2_wd[l]) for l in range(DEPTH)]
    mix = _prep_mix(w_in, w_out, gla_w_gk, gla_b_gk, gla_norm_g, s5_w_glu, s5_b_glu)
    s5 = _s5_prepare(s5_a_re, s5_a_im, s5_log_step, s5_b_re, s5_b_im, s5_c_re, s5_c_im, s5_d)
    rwkv = _prep_rwkv(rwkv_mu, rwkv_w_r, rwkv_w_k, rwkv_w_v, rwkv_w_o, rwkv_w0, rwkv_w1, rwkv_w2,
                      rwkv_a0, rwkv_a1, rwkv_a2, rwkv_g1, rwkv_g2, rwkv_k_k, rwkv_k_a, rwkv_r_k,
                      rwkv_lnx_g, rwkv_lnx_b)

    def lnp(layer, idx):
        return _row(ln_g[layer, idx]), _row(ln_b[layer, idx])

    x = x_prompt
    x = _ffn_prompt(x, mods_p, 0, 0, *ffn1[0], *lnp(0, 0))
    x, gla_p, s5_re_p, s5_im_p = _mix_prompt(x, mods_p, 0, mix, s5, *lnp(0, 1))
    x = _ffn_prompt(x, mods_p, 0, 6, *ffn2[0], *lnp(0, 2))
    x = _ffn_prompt(x, mods_p, 1, 0, *ffn1[1], *lnp(1, 0))
    x, shift_p, wkv_p = _rwkv_prompt(x, mods_p, 1, rwkv, *lnp(1, 1))
    y_prompt = _ffn_prompt(x, mods_p, 1, 6, *ffn2[1], *lnp(1, 2))

    xs = x_sample.reshape(ns, D_MODEL)
    xs = _ffn_sample(xs, mods_s, 0, 0, *ffn1[0], *lnp(0, 0))
    xs, gla_s, s5_re_s, s5_im_s = _mix_sample(xs, mods_s, 0, state_gla, state_s5_re, state_s5_im,
                                              mix, s5, *lnp(0, 1))
    xs = _ffn_sample(xs, mods_s, 0, 6, *ffn2[0], *lnp(0, 2))
    xs = _ffn_sample(xs, mods_s, 1, 0, *ffn1[1], *lnp(1, 0))
    xs, shift_s, wkv_s = _rwkv_sample(xs, mods_s, 1, state_rwkv_shift, state_rwkv_wkv, rwkv, *lnp(1, 1))
    xs = _ffn_sample(xs, mods_s, 1, 6, *ffn2[1], *lnp(1, 2))
    y_sample = xs.reshape(ns, 1, D_MODEL)

    return (y_prompt, y_sample, gla_p, s5_re_p, s5_im_p, shift_p, wkv_p,
            gla_s, s5_re_s, s5_im_s, shift_s, wkv_s)
```

```python
import functools
import math

import jax
import jax.numpy as jnp
from jax import lax
from jax.experimental import pallas as pl
from jax.experimental.pallas import tpu as pltpu

F32 = jnp.float32
BF16 = jnp.bfloat16

D_MODEL = 1024
DEPTH = 2
DN_ALPHA = (2 * DEPTH) ** 0.25
LN_EPS = 1e-5
RMS_EPS = 1e-5
FFN_RES = 0.5
D_FF = 2752
N_MODS = 9

GLA_HEADS = 4
GLA_DK = 64
GLA_DV = 128
GLA_KEY = GLA_HEADS * GLA_DK
GLA_VAL = GLA_HEADS * GLA_DV
GLA_GATE_RANK = 16
GLA_GATE_NORM = 16.0

S5_GROUP = 16
S5_WIDTH = 512
S5_GROUPS = S5_WIDTH // S5_GROUP
S5_STATE = 64
S5_CH = S5_GROUPS * S5_STATE

RWKV_HEAD = 64
RWKV_HEADS = D_MODEL // RWKV_HEAD
RWKV_LNX_EPS = 64e-5
NORM_EPS = 1e-12

LANES = 128
SUBLANES = 8
MXU_DIM = 256
VMEM_LIMIT = 58 * 1024 * 1024

CHUNK = 64
HEADS_PER_GROUP = MXU_DIM // RWKV_HEAD
N_GROUPS = RWKV_HEADS // HEADS_PER_GROUP
FF_CHUNKS = ((0, 1024), (1024, 2048), (2048, D_FF))
LORA_PAD = 128
GATE_LORA_PAD = 256
PERM_STRIDE = CHUNK + SUBLANES
SAMPLE_BLOCK = 8


def _dot(a, b):
    return jnp.dot(a, b, preferred_element_type=F32)


def _dot_nt(a, b):
    return lax.dot_general(a, b, (((1,), (1,)), ((), ())), preferred_element_type=F32)


def _bdot(a, w_ref_or_val):
    return _dot(a.astype(BF16), w_ref_or_val)


def _sigmoid(x):
    return 1.0 / (1.0 + jnp.exp(-x))


def _silu(x):
    return x * _sigmoid(x)


def _softplus(x):
    return jnp.maximum(x, 0.0) + jnp.log(1.0 + jnp.exp(-jnp.abs(x)))


def _gelu_tanh(x):
    c = math.sqrt(2.0 / math.pi)
    return 0.5 * x * (1.0 + jnp.tanh(c * (x + 0.044715 * (x * x * x))))


def _layer_norm(x, g, b):
    mu = jnp.mean(x, axis=-1, keepdims=True)
    d = x - mu
    var = jnp.mean(d * d, axis=-1, keepdims=True)
    return d * lax.rsqrt(var + LN_EPS) * g + b


def _split3(x):
    hi = x.astype(BF16)
    r1 = x - hi.astype(F32)
    mid = r1.astype(BF16)
    lo = (r1 - mid.astype(F32)).astype(BF16)
    return hi, mid, lo


def _exact_dot_left01(m01, x):
    hi, mid, lo = _split3(x)
    return _dot(m01, hi) + _dot(m01, mid) + _dot(m01, lo)


def _dot_right01(x, m01):
    hi = x.astype(BF16)
    lo = (x - hi.astype(F32)).astype(BF16)
    return _dot(hi, m01) + _dot(lo, m01)


def _seg_ones(seg):
    r = lax.broadcasted_iota(jnp.int32, (MXU_DIM, MXU_DIM), 0) // seg
    c = lax.broadcasted_iota(jnp.int32, (MXU_DIM, MXU_DIM), 1) // seg
    return jnp.where(r == c, 1.0, 0.0).astype(BF16)


def _seg_sum(x, seg, split=False):
    ones = _seg_ones(seg)
    tiles = []
    for i in range(x.shape[1] // MXU_DIM):
        xt = x[:, i * MXU_DIM:(i + 1) * MXU_DIM]
        tiles.append(_dot_right01(xt, ones) if split else _dot(xt.astype(BF16), ones))
    return jnp.concatenate(tiles, axis=1)


def _tril_ones(n):
    r = lax.broadcasted_iota(jnp.int32, (n, n), 0)
    c = lax.broadcasted_iota(jnp.int32, (n, n), 1)
    return jnp.where(r >= c, 1.0, 0.0).astype(BF16)


def _chunk_cumsum(x, n_batch):
    tri = _tril_ones(CHUNK)
    parts = [_exact_dot_left01(tri, x[b * CHUNK:(b + 1) * CHUNK, :]) for b in range(n_batch)]
    return jnp.concatenate(parts, axis=0)


def _head_stack(z, n_heads, width):
    head = lax.broadcasted_iota(jnp.int32, z.shape, 1) // width
    return jnp.concatenate([jnp.where(head == h, z, 0.0) for h in range(n_heads)], axis=0)


def _block_diag_mask(rows, cols, rblk, cblk):
    r = lax.broadcasted_iota(jnp.int32, (rows, cols), 0) // rblk
    c = lax.broadcasted_iota(jnp.int32, (rows, cols), 1) // cblk
    return r == c


def _full_spec(shape, grid_rank):
    zeros = (0,) * len(shape)
    if grid_rank == 1:
        return pl.BlockSpec(shape, lambda i: zeros, pipeline_mode=pl.Buffered(1))
    return pl.BlockSpec(shape, lambda i, j: zeros, pipeline_mode=pl.Buffered(1))


def _params(semantics):
    return pltpu.CompilerParams(dimension_semantics=semantics, vmem_limit_bytes=VMEM_LIMIT)


ADA_TN = 1152


def _ada_body(cs_ref, cp_ref, w_ref, b_ref, os_ref, op_ref):
    w = w_ref[...].astype(BF16)
    os_ref[...] = _bdot(_silu(cs_ref[...]), w) + b_ref[...]
    op_ref[...] = _bdot(_silu(cp_ref[...]), w) + b_ref[...]


def _ada_mods(c_sample, c_prompt, ada_w, ada_b):
    ns, bp = c_sample.shape[0], c_prompt.shape[0]
    width = N_MODS * D_MODEL
    return pl.pallas_call(
        _ada_body,
        grid=(DEPTH, width // ADA_TN),
        in_specs=[
            pl.BlockSpec((ns, D_MODEL), lambda l, j: (0, 0)),
            pl.BlockSpec((bp, D_MODEL), lambda l, j: (0, 0)),
            pl.BlockSpec((None, D_MODEL, ADA_TN), lambda l, j: (l, 0, j)),
            pl.BlockSpec((None, 1, ADA_TN), lambda l, j: (l, 0, j)),
        ],
        out_specs=[pl.BlockSpec((None, ns, ADA_TN), lambda l, j: (l, 0, j)),
                   pl.BlockSpec((None, bp, ADA_TN), lambda l, j: (l, 0, j))],
        out_shape=[jax.ShapeDtypeStruct((DEPTH, ns, width), F32),
                   jax.ShapeDtypeStruct((DEPTH, bp, width), F32)],
        compiler_params=_params(("arbitrary", "arbitrary")),
        name="ada_mods",
    )(c_sample, c_prompt, ada_w, ada_b.reshape(DEPTH, 1, width))


FFN_TM = 512


def _ffn_body(x_ref, sh_ref, sc_ref, gt_ref, wg_ref, wu_ref, wd_ref, g_ref, b_ref, o_ref):
    x = x_ref[...]
    h = (x * (1.0 + sc_ref[...]) + sh_ref[...]).astype(BF16)
    acc = None
    for lo, hi in FF_CHUNKS:
        g = _dot_nt(h, wg_ref[lo:hi, :])
        u = _dot_nt(h, wu_ref[lo:hi, :])
        y = _bdot(_silu(g) * u, wd_ref[lo:hi, :])
        acc = y if acc is None else acc + y
    r = FFN_RES * (1.0 + gt_ref[...]) * acc
    o_ref[...] = _layer_norm(DN_ALPHA * x + r, g_ref[...], b_ref[...])


def _mod_specs_prompt(layer, first):
    return [
        pl.BlockSpec((None, None, 1, D_MODEL), functools.partial(
            lambda b, i, col: (layer, b, 0, col), col=first + k))
        for k in range(3)
    ]


def _ffn_weight_specs(layer, grid_rank):
    if grid_rank == 1:
        spec = pl.BlockSpec((None, D_FF, D_MODEL), lambda i: (layer, 0, 0), pipeline_mode=pl.Buffered(1))
    else:
        spec = pl.BlockSpec((None, D_FF, D_MODEL), lambda i, j: (layer, 0, 0), pipeline_mode=pl.Buffered(1))
    return [spec] * 3 + [_full_spec((1, D_MODEL), grid_rank)] * 2


def _ffn_prompt(x, mods_p, layer, first, wg, wu, wd, ln_g, ln_b):
    bsz, seq, _ = x.shape
    w_specs = _ffn_weight_specs(layer, 2)
    return pl.pallas_call(
        _ffn_body,
        grid=(bsz, seq // FFN_TM),
        in_specs=[pl.BlockSpec((None, FFN_TM, D_MODEL), lambda b, i: (b, i, 0))]
        + _mod_specs_prompt(layer, first) + w_specs,
        out_specs=pl.BlockSpec((None, FFN_TM, D_MODEL), lambda b, i: (b, i, 0)),
        out_shape=jax.ShapeDtypeStruct(x.shape, F32),
        compiler_params=_params(("arbitrary", "arbitrary")),
        name="ffn_prompt",
    )(x, mods_p, mods_p, mods_p, wg, wu, wd, ln_g, ln_b)


def _mod_specs_sample(layer, first, rows):
    return [
        pl.BlockSpec((None, rows, D_MODEL), functools.partial(
            lambda i, col: (layer, 0, col), col=first + k))
        for k in range(3)
    ]


def _ffn_sample(x, mods_s, layer, first, wg, wu, wd, ln_g, ln_b):
    n = x.shape[0]
    w_specs = _ffn_weight_specs(layer, 1)
    return pl.pallas_call(
        _ffn_body,
        grid=(1,),
        in_specs=[_full_spec((n, D_MODEL), 1)] + _mod_specs_sample(layer, first, n) + w_specs,
        out_specs=_full_spec((n, D_MODEL), 1),
        out_shape=jax.ShapeDtypeStruct(x.shape, F32),
        compiler_params=_params(("arbitrary",)),
        name="ffn_sample",
    )(x, mods_s, mods_s, mods_s, wg, wu, wd, ln_g, ln_b)


def _rwkv_project(h, prev, mu_ref, wr, wk, wv, w0, w1, w2, a0, a1, a2, kk_ref, ka_ref):
    xx = prev - h
    mu = mu_ref[...]

    def mix(i):
        return h + xx * mu[i:i + 1, :]

    r = _bdot(mix(0), wr[...])
    k = _bdot(mix(2), wk[...])
    v = _bdot(mix(3), wv[...])
    wl = w0[...] + _bdot(jnp.tanh(_bdot(mix(1), w1[...])), w2[...])
    log_decay = -math.exp(-0.5) * _sigmoid(wl)
    a_sig = _sigmoid(a0[...] + _bdot(_bdot(mix(4), a1[...]), a2[...]))
    kk = k * kk_ref[...]
    norm = jnp.sqrt(_seg_sum(kk * kk, RWKV_HEAD))
    kk = kk / jnp.maximum(norm, NORM_EPS)
    k2 = k * (1.0 + (a_sig - 1.0) * ka_ref[...])
    return r, log_decay, k2, v, -kk, kk * a_sig, mix(5).astype(BF16)


def _rwkv_bonus(r, k2, v, rk_ref):
    return _seg_sum(r * k2 * rk_ref[...], RWKV_HEAD) * v


def _rwkv_output(y, bonus, xg, x, gt, g1, g2, wo, lng, lnb, ln_g, ln_b):
    gate = _bdot(_sigmoid(_dot(xg, g1[...])), g2[...])
    inv_n = 1.0 / RWKV_HEAD
    mu_y = _seg_sum(y, RWKV_HEAD, split=True) * inv_n
    d = y - mu_y
    var = _seg_sum(d * d, RWKV_HEAD) * inv_n
    y = d * lax.rsqrt(var + RWKV_LNX_EPS) * lng[...] + lnb[...]
    out = _bdot((y + bonus) * gate, wo[...])
    return _layer_norm(DN_ALPHA * x + (1.0 + gt) * out, ln_g[...], ln_b[...])


def _rwkv_weight_list(p):
    return [p["mu"], p["wr"], p["wk"], p["wv"], p["w0"], p["w1"], p["w2"], p["a0"], p["a1"], p["a2"],
            p["k_k"], p["k_a"], p["r_k"]]


def _rwkv_out_weight_list(p, ln_g, ln_b):
    return [p["g1"], p["g2"], p["wo"], p["lnx_g"], p["lnx_b"], ln_g, ln_b]


NEUMANN_LEVELS = int(math.log2(CHUNK)) - 1
WKV_BATCH_UNROLL = 2


def _wkv_chunks(problems):
    c = CHUNK
    hp = HEADS_PER_GROUP
    w = hp * RWKV_HEAD
    n = hp * c
    t_idx = lax.broadcasted_iota(jnp.int32, (c, w), 0)
    s_idx = lax.broadcasted_iota(jnp.int32, (c, w), 1) % c
    strict = t_idx > s_idx
    incl = t_idx >= s_idx
    eye = jnp.where(t_idx == s_idx, 1.0, 0.0)
    bd_mask = _block_diag_mask(w, w, RWKV_HEAD, RWKV_HEAD)
    idx = range(len(problems))
    At, Rt, Kt, Bt, V, S0, dec = zip(*problems)

    def bf(x):
        return x.astype(BF16)

    def stack(x, width):
        return bf(_head_stack(x, hp, width))

    X2 = [jnp.concatenate([At[i], Rt[i]], axis=0) for i in idx]
    KB = [jnp.concatenate([stack(Kt[i], RWKV_HEAD), stack(Bt[i], RWKV_HEAD)], axis=0) for i in idx]
    sc = [_dot_nt(X2[i], KB[i]) for i in idx]
    XS = [_dot_nt(X2[i], bf(S0[i])) for i in idx]
    Vbd = [stack(V[i], RWKV_HEAD) for i in idx]
    Aab = [jnp.where(strict, sc[i][:c, n:], 0.0) for i in idx]
    rhs = [XS[i][:c] + _dot(bf(jnp.where(strict, sc[i][:c, :n], 0.0)), Vbd[i]) for i in idx]

    Q = Aab
    P = [eye + Aab[i] for i in idx]
    Qbd = [stack(Q[i], c) for i in idx]
    for _ in range(NEUMANN_LEVELS):
        Q = [_dot(bf(Q[i]), Qbd[i]) for i in idx]
        Qbd = [stack(Q[i], c) for i in idx]
        P = [P[i] + _dot(bf(P[i]), Qbd[i]) for i in idx]
    U = [_dot(bf(P[i]), stack(rhs[i], RWKV_HEAD)) for i in idx]

    Y = [XS[i][c:] + _dot(bf(jnp.where(incl, sc[i][c:, :n], 0.0)), Vbd[i])
         + _dot(bf(jnp.where(incl, sc[i][c:, n:], 0.0)), stack(U[i], RWKV_HEAD)) for i in idx]

    upd = [_dot(bf(jnp.concatenate([V[i], U[i]], axis=0).T),
                bf(jnp.concatenate([Kt[i], Bt[i]], axis=0))) for i in idx]
    S1 = [(S0[i] + jnp.where(bd_mask, upd[i], 0.0)) * dec[i] for i in idx]
    return Y, S1


def _rwkv_prompt_body(x_ref, sh_ref, sc_ref, gt_ref,
                      mu, wr, wk, wv, w0, w1, w2, a0, a1, a2, kk_ref, ka_ref, rk_ref,
                      g1, g2, wo, lng, lnb, ln_g, ln_b,
                      o_ref, shift_ref, state_ref,
                      carry, s_at, s_rt, s_kt, s_bt, s_v, s_dec, s_y, s_bonus, s_xg):
    nb = x_ref.shape[0]
    rows = nb * CHUNK
    gw = HEADS_PER_GROUP * RWKV_HEAD
    step = pl.program_id(0)

    @pl.when(step == 0)
    def _():
        carry[...] = jnp.zeros(carry.shape, F32)
        state_ref[...] = jnp.zeros(state_ref.shape, F32)

    h3 = x_ref[...] * (1.0 + sc_ref[...]) + sh_ref[...]
    h = h3.reshape(rows, D_MODEL)
    first = lax.broadcasted_iota(jnp.int32, (rows, D_MODEL), 0) % CHUNK == 0
    carried = jnp.broadcast_to(carry[...], (nb, CHUNK, D_MODEL)).reshape(rows, D_MODEL)
    prev = jnp.where(first, carried, pltpu.roll(h, 1, axis=0))
    carry[...] = h3[:, CHUNK - 1:CHUNK, :]
    shift_ref[...] = h3[:, CHUNK - 1:CHUNK, :]

    r, lw, k2, v, av, bv, xg = _rwkv_project(
        h, prev, mu, wr, wk, wv, w0, w1, w2, a0, a1, a2, kk_ref, ka_ref)
    s_xg[...] = xg
    s_bonus[...] = _rwkv_bonus(r, k2, v, rk_ref)

    g = _chunk_cumsum(lw, nb)
    g3 = g.reshape(nb, CHUNK, D_MODEL)
    e_neg = jnp.exp(-g)
    arrays = (
        (s_at, av * jnp.exp(g - lw)),
        (s_rt, r * jnp.exp(g)),
        (s_kt, k2 * e_neg),
        (s_bt, bv * e_neg),
        (s_v, v),
    )
    for ref, val in arrays:
        for gi in range(N_GROUPS):
            ref[gi] = val[:, gi * gw:(gi + 1) * gw].astype(ref.dtype)
    dec = jnp.exp(g3[:, CHUNK - 1:CHUNK, :])
    for gi in range(N_GROUPS):
        s_dec[gi] = dec[:, :, gi * gw:(gi + 1) * gw]

    def chunk_step(it, carry_val):
        keys = []
        for db in range(WKV_BATCH_UNROLL):
            b = it * WKV_BATCH_UNROLL + db
            sl = pl.ds(pl.multiple_of(b * CHUNK, CHUNK), CHUNK)
            keys += [(b, gi, sl) for gi in range(N_GROUPS)]
        Y, S1 = _wkv_chunks([
            (s_at[gi, sl, :], s_rt[gi, sl, :], s_kt[gi, sl, :], s_bt[gi, sl, :],
             s_v[gi, sl, :], state_ref[b, gi], s_dec[gi, b])
            for b, gi, sl in keys])
        for (b, gi, sl), y_val, s_val in zip(keys, Y, S1):
            s_y[gi, sl, :] = y_val
            state_ref[b, gi] = s_val
        return carry_val

    lax.fori_loop(0, nb // WKV_BATCH_UNROLL, chunk_step, 0)

    y = jnp.concatenate([s_y[gi] for gi in range(N_GROUPS)], axis=1)
    gt = jnp.broadcast_to(gt_ref[...], (nb, CHUNK, D_MODEL)).reshape(rows, D_MODEL)
    x = x_ref[...].reshape(rows, D_MODEL)
    out = _rwkv_output(y, s_bonus[...], s_xg[...], x, gt, g1, g2, wo, lng, lnb, ln_g, ln_b)
    o_ref[...] = out.reshape(nb, CHUNK, D_MODEL)


def _rwkv_prompt(x, mods_p, layer, p, ln_g, ln_b):
    bsz, seq, _ = x.shape
    rows = bsz * CHUNK
    gw = HEADS_PER_GROUP * RWKV_HEAD
    weights = _rwkv_weight_list(p) + _rwkv_out_weight_list(p, ln_g, ln_b)
    mod_specs = [
        pl.BlockSpec((None, bsz, 1, D_MODEL), functools.partial(lambda i, col: (layer, 0, 0, col), col=3 + k))
        for k in range(3)
    ]
    slab = pltpu.VMEM((N_GROUPS, rows, gw), F32)
    slab16 = pltpu.VMEM((N_GROUPS, rows, gw), BF16)
    out, shift, state = pl.pallas_call(
        _rwkv_prompt_body,
        grid=(seq // CHUNK,),
        in_specs=[pl.BlockSpec((bsz, CHUNK, D_MODEL), lambda i: (0, i, 0))] + mod_specs
        + [_full_spec(w.shape, 1) for w in weights],
        out_specs=[
            pl.BlockSpec((bsz, CHUNK, D_MODEL), lambda i: (0, i, 0)),
            _full_spec((bsz, 1, D_MODEL), 1),
            _full_spec((bsz, N_GROUPS, gw, gw), 1),
        ],
        out_shape=[
            jax.ShapeDtypeStruct(x.shape, F32),
            jax.ShapeDtypeStruct((bsz, 1, D_MODEL), F32),
            jax.ShapeDtypeStruct((bsz, N_GROUPS, gw, gw), F32),
        ],
        scratch_shapes=[pltpu.VMEM((bsz, 1, D_MODEL), F32), slab16, slab16, slab, slab, slab]
        + [pltpu.VMEM((N_GROUPS, bsz, 1, gw), F32), slab,
           pltpu.VMEM((rows, D_MODEL), F32), pltpu.VMEM((rows, D_MODEL), BF16)],
        compiler_params=_params(("arbitrary",)),
        name="rwkv_prompt",
    )(x, mods_p, mods_p, mods_p, *weights)
    s6 = state.reshape(bsz, N_GROUPS, HEADS_PER_GROUP, RWKV_HEAD, HEADS_PER_GROUP, RWKV_HEAD)
    wkv = jnp.stack([s6[:, :, hh, :, hh, :] for hh in range(HEADS_PER_GROUP)], axis=2)
    return out, shift.reshape(bsz, D_MODEL), wkv.reshape(bsz, RWKV_HEADS, RWKV_HEAD, RWKV_HEAD)


def _rwkv_sample_pre_body(x_ref, sh_ref, sc_ref, prev_ref,
                          mu, wr, wk, wv, w0, w1, w2, a0, a1, a2, kk_ref, ka_ref, rk_ref,
                          h_ref, rt_ref, wt_ref, kt_ref, vt_ref, at_ref, bt_ref, xg_ref, bonus_ref):
    h = x_ref[...] * (1.0 + sc_ref[...]) + sh_ref[...]
    r, lw, k2, v, av, bv, xg = _rwkv_project(
        h, prev_ref[...], mu, wr, wk, wv, w0, w1, w2, a0, a1, a2, kk_ref, ka_ref)
    h_ref[...] = h
    rt_ref[...] = r.T
    wt_ref[...] = jnp.exp(lw).T
    kt_ref[...] = k2.T
    vt_ref[...] = v.T
    at_ref[...] = av.T
    bt_ref[...] = bv.T
    xg_ref[...] = xg
    bonus_ref[...] = _rwkv_bonus(r, k2, v, rk_ref)


def _rwkv_sample_step_body(s_ref, r_ref, w_ref, k_ref, v_ref, a_ref, b_ref, so_ref, y_ref):
    a, w, b, k, r = a_ref[...], w_ref[...], b_ref[...], k_ref[...], r_ref[...]

    def row_step(i, carry_val):
        S = s_ref[i]
        sa = jnp.sum(S * a, axis=0, keepdims=True)
        S1 = S * w + sa * b + v_ref[pl.ds(i, 1), :] * k
        so_ref[i] = S1
        y_ref[pl.ds(i, 1), :] = jnp.sum(S1 * r, axis=0, keepdims=True)
        return carry_val

    lax.fori_loop(0, RWKV_HEAD, row_step, 0, unroll=2)


def _rwkv_sample_post_body(yt_ref, bonus_ref, xg_ref, x_ref, gt_ref,
                           g1, g2, wo, lng, lnb, ln_g, ln_b, o_ref):
    o_ref[...] = _rwkv_output(yt_ref[...].T, bonus_ref[...], xg_ref[...], x_ref[...], gt_ref[...],
                              g1, g2, wo, lng, lnb, ln_g, ln_b)


def _to_columns(a):
    n, f = a.shape
    return a.reshape(n // SAMPLE_BLOCK, SAMPLE_BLOCK, f).transpose(0, 2, 1)


def _rwkv_sample(x, mods_s, layer, s_shift, s_wkv, p, ln_g, ln_b):
    n = x.shape[0]
    row = jax.ShapeDtypeStruct((n, D_MODEL), F32)
    row_spec = _full_spec((n, D_MODEL), 1)
    pre_w = _rwkv_weight_list(p)
    ms = _mod_specs_sample(layer, 3, n)
    col = jax.ShapeDtypeStruct((D_MODEL, n), F32)
    col_spec = _full_spec((D_MODEL, n), 1)
    h, *vecs, xg, bonus = pl.pallas_call(
        _rwkv_sample_pre_body,
        grid=(1,),
        in_specs=[row_spec, ms[0], ms[1], row_spec] + [_full_spec(a.shape, 1) for a in pre_w],
        out_specs=[row_spec] + [col_spec] * 6 + [row_spec] * 2,
        out_shape=[row] + [col] * 6 + [jax.ShapeDtypeStruct((n, D_MODEL), BF16), row],
        compiler_params=_params(("arbitrary",)),
        name="rwkv_sample_pre",
    )(x, mods_s, mods_s, s_shift, *pre_w)

    st_spec = pl.BlockSpec((None, RWKV_HEAD, RWKV_HEAD, n), lambda hd: (hd, 0, 0, 0))
    vec_spec = pl.BlockSpec((RWKV_HEAD, n), lambda hd: (hd, 0))
    st_new, yt = pl.pallas_call(
        _rwkv_sample_step_body,
        grid=(RWKV_HEADS,),
        in_specs=[st_spec] + [vec_spec] * len(vecs),
        out_specs=[st_spec, vec_spec],
        out_shape=[jax.ShapeDtypeStruct((RWKV_HEADS, RWKV_HEAD, RWKV_HEAD, n), F32), col],
        compiler_params=_params(("arbitrary",)),
        name="rwkv_sample_step",
    )(jnp.transpose(s_wkv, (1, 2, 3, 0)), *vecs)
    s_new = jnp.transpose(st_new, (3, 0, 1, 2))

    post_w = _rwkv_out_weight_list(p, ln_g, ln_b)
    out = pl.pallas_call(
        _rwkv_sample_post_body,
        grid=(1,),
        in_specs=[col_spec] + [row_spec] * 3 + [ms[2]] + [_full_spec(a.shape, 1) for a in post_w],
        out_specs=row_spec,
        out_shape=row,
        compiler_params=_params(("arbitrary",)),
        name="rwkv_sample_post",
    )(yt, bonus, xg, x, mods_s, *post_w)
    return out, h, s_new


def _s5_prep_body(are_ref, aim_ref, ls_ref, bre_ref, bim_ref, abre_ref, abim_ref, bbre_ref, bbim_ref):
    a_re = are_ref[...]
    a_im = aim_ref[...]
    dt = jnp.exp(ls_ref[...])
    mag = jnp.exp(a_re * dt)
    ab_re = mag * jnp.cos(a_im * dt)
    ab_im = mag * jnp.sin(a_im * dt)
    den = a_re * a_re + a_im * a_im
    nr = ab_re - 1.0
    z_re = (nr * a_re + ab_im * a_im) / den
    z_im = (ab_im * a_re - nr * a_im) / den
    b_re = bre_ref[...]
    b_im = bim_ref[...]
    abre_ref[...] = ab_re
    abim_ref[...] = ab_im
    bbre_ref[...] = z_re * b_re - z_im * b_im
    bbim_ref[...] = z_re * b_im + z_im * b_re


def _s5_prepare(s5_a_re, s5_a_im, s5_log_step, s5_b_re, s5_b_im, s5_c_re, s5_c_im, s5_d):
    G, P, C = S5_GROUPS, S5_STATE, S5_GROUP
    small = jax.ShapeDtypeStruct((G, 1, P), F32)
    big = jax.ShapeDtypeStruct((G, C, P), F32)
    ls = jnp.broadcast_to(s5_log_step.reshape(G, 1, 1), (G, 1, P))
    ab_re, ab_im, bb_re, bb_im = pl.pallas_call(
        _s5_prep_body,
        grid=(1,),
        in_specs=[_full_spec((G, 1, P), 1)] * 3 + [_full_spec((G, C, P), 1)] * 2,
        out_specs=[_full_spec((G, 1, P), 1)] * 2 + [_full_spec((G, C, P), 1)] * 2,
        out_shape=[small, small, big, big],
        name="s5_prepare",
    )(s5_a_re.reshape(G, 1, P), s5_a_im.reshape(G, 1, P), ls,
      s5_b_re.transpose(0, 2, 1), s5_b_im.transpose(0, 2, 1))

    eye = jnp.eye(G, dtype=F32)
    gpt = LANES // C
    n_tiles = G // gpt

    def in_blocks(bb):
        full = (bb[:, :, None, :] * eye[:, None, :, None]).reshape(G * C, G * P)
        return jnp.stack([full[k * LANES:(k + 1) * LANES, k * gpt * P:(k + 1) * gpt * P]
                          for k in range(n_tiles)]).astype(BF16)

    def out_blocks(cc):
        full = (cc.transpose(0, 2, 1)[:, :, None, :] * eye[:, None, :, None]).reshape(G * P, G * C)
        return jnp.stack([full[k * gpt * P:(k + 1) * gpt * P, k * LANES:(k + 1) * LANES]
                          for k in range(n_tiles)]).astype(BF16)

    return dict(
        ab_re=ab_re.reshape(1, G * P), ab_im=ab_im.reshape(1, G * P),
        bb_re=in_blocks(bb_re), bb_im=in_blocks(bb_im),
        c_re=out_blocks(s5_c_re), c_im=out_blocks(s5_c_im),
        d=s5_d.reshape(1, G * C),
    )


def _s5_input(u, bb_re_ref, bb_im_ref):
    n_tiles = bb_re_ref.shape[0]
    ub = u.astype(BF16)
    re = [_dot(ub[:, k * LANES:(k + 1) * LANES], bb_re_ref[k]) for k in range(n_tiles)]
    im = [_dot(ub[:, k * LANES:(k + 1) * LANES], bb_im_ref[k]) for k in range(n_tiles)]
    return jnp.concatenate(re, axis=1), jnp.concatenate(im, axis=1)


def _s5_output(h_re, h_im, c_re_ref, c_im_ref):
    n_tiles = c_re_ref.shape[0]
    w = h_re.shape[1] // n_tiles
    hr = h_re.astype(BF16)
    hi = h_im.astype(BF16)
    ys = [_dot(hr[:, k * w:(k + 1) * w], c_re_ref[k]) - _dot(hi[:, k * w:(k + 1) * w], c_im_ref[k])
          for k in range(n_tiles)]
    return jnp.concatenate(ys, axis=1)


C_Q, C_K, C_V, C_G, C_U, C_GK, C_END = 0, 256, 512, 1024, 1536, 2048, 2176


def _mix_project(h, w_in, w_gk, b_gk):
    p = _bdot(h, w_in[...])
    q = p[:, C_Q:C_K] * (GLA_DK ** -0.5)
    k = p[:, C_K:C_V]
    v = p[:, C_V:C_G]
    gg = p[:, C_G:C_U]
    u = p[:, C_U:C_GK]
    z = _bdot(p[:, C_GK:C_END], w_gk[...]) + b_gk[...]
    gk = -_softplus(-z) * (1.0 / GLA_GATE_NORM)
    return q, k, v, gg, u, gk


def _gla_finish(o, gg, norm_g):
    parts = []
    for hh in range(GLA_HEADS):
        oh = o[:, hh * GLA_DV:(hh + 1) * GLA_DV]
        parts.append(oh * lax.rsqrt(jnp.mean(oh * oh, axis=-1, keepdims=True) + RMS_EPS) * norm_g[...])
    return jnp.concatenate(parts, axis=1) * _silu(gg)


def _mix_output(o_gla, y_s5, x, gt, w_glu, b_glu, w_out, ln_g, ln_b):
    z = _gelu_tanh(y_s5)
    o_s5 = z * _sigmoid(_bdot(z, w_glu[...]) + b_glu[...])
    out = _bdot(o_gla, w_out[0:GLA_VAL, :]) + _bdot(o_s5, w_out[GLA_VAL:GLA_VAL + S5_WIDTH, :])
    return _layer_norm(DN_ALPHA * x + (1.0 + gt) * out, ln_g[...], ln_b[...])


def _gla_chunk(q, k, v, bc, ST):
    c = CHUNK
    mid = c // 2 - 1
    b_mid = bc[mid:mid + 1, :]
    b_last = bc[c - 1:c, :]
    q_in = q * jnp.exp(bc - b_mid)
    k_in = k * jnp.exp(b_mid - bc)
    q_full = q * jnp.exp(bc)
    k_st = k * jnp.exp(b_last - bc)
    t_idx = lax.broadcasted_iota(jnp.int32, (c, GLA_KEY), 0)
    s_idx = lax.broadcasted_iota(jnp.int32, (c, GLA_KEY), 1) % c
    scores = _dot_nt(q_in.astype(BF16), _head_stack(k_in, GLA_HEADS, GLA_DK).astype(BF16))
    scores = jnp.where(t_idx >= s_idx, scores, 0.0)
    o = _dot(scores.astype(BF16), _head_stack(v, GLA_HEADS, GLA_DV).astype(BF16))
    o = o + _dot_nt(q_full.astype(BF16), ST.astype(BF16))
    upd = _dot(v.T.astype(BF16), k_st.astype(BF16))
    ST1 = ST * jnp.exp(b_last) + jnp.where(
        _block_diag_mask(GLA_VAL, GLA_KEY, GLA_DV, GLA_DK), upd, 0.0)
    return o, ST1


def _mix_prompt_body(x_ref, sh_ref, sc_ref, gt_ref,
                     w_in, w_gk, b_gk, norm_g, ab_re_ref, ab_im_ref, bb_re, bb_im, c_re, c_im, d_ref,
                     w_glu, b_glu, w_out, ln_g, ln_b,
                     o_ref, gla_ref, hre_ref, him_ref,
                     s_q, s_k, s_v, s_bc, s_o, s_perm, s_re, s_im):
    nb = x_ref.shape[0]
    rows = nb * CHUNK
    step = pl.program_id(0)

    @pl.when(step == 0)
    def _():
        gla_ref[...] = jnp.zeros(gla_ref.shape, F32)
        hre_ref[...] = jnp.zeros(hre_ref.shape, F32)
        him_ref[...] = jnp.zeros(him_ref.shape, F32)

    x3 = x_ref[...]
    h3 = x3 * (1.0 + sc_ref[...]) + sh_ref[...]
    x = x3.reshape(rows, D_MODEL)
    h = h3.reshape(rows, D_MODEL)
    q, k, v, gg, u, gk = _mix_project(h, w_in, w_gk, b_gk)

    s_q[...] = q
    s_k[...] = k
    s_v[...] = v
    s_bc[...] = _chunk_cumsum(gk, nb)

    def gla_step(b, carry_val):
        sl = pl.ds(pl.multiple_of(b * CHUNK, CHUNK), CHUNK)
        o, ST1 = _gla_chunk(s_q[sl, :], s_k[sl, :], s_v[sl, :], s_bc[sl, :], gla_ref[b])
        s_o[sl, :] = o
        gla_ref[b] = ST1
        return carry_val

    lax.fori_loop(0, nb, gla_step, 0)
    o_gla = _gla_finish(s_o[...], gg, norm_g)

    n_lt = S5_WIDTH // LANES
    for b in range(nb):
        for lt in range(n_lt):
            s_perm[lt, b * PERM_STRIDE:b * PERM_STRIDE + CHUNK, :] = (
                u[b * CHUNK:(b + 1) * CHUNK, lt * LANES:(lt + 1) * LANES])
    u_tb = jnp.concatenate(
        [jnp.concatenate([s_perm[lt, pl.ds(t, nb, stride=PERM_STRIDE), :] for lt in range(n_lt)], axis=1)
         for t in range(CHUNK)], axis=0)
    bu_re, bu_im = _s5_input(u_tb, bb_re, bb_im)
    s_re[...] = bu_re
    s_im[...] = bu_im

    n_ch = 4
    cw = S5_CH // n_ch
    for ci in range(n_ch):
        lanes = slice(ci * cw, (ci + 1) * cw)
        a_re = jnp.broadcast_to(ab_re_ref[:, lanes], (nb, cw))
        a_im = jnp.broadcast_to(ab_im_ref[:, lanes], (nb, cw))

        def scan_step(t, hc, lanes=lanes, a_re=a_re, a_im=a_im):
            h_re, h_im = hc
            sl = pl.ds(pl.multiple_of(t * nb, nb), nb)
            n_re = a_re * h_re - a_im * h_im + s_re[sl, lanes]
            n_im = a_re * h_im + a_im * h_re + s_im[sl, lanes]
            s_re[sl, lanes] = n_re
            s_im[sl, lanes] = n_im
            return n_re, n_im

        f_re, f_im = lax.fori_loop(0, CHUNK, scan_step, (hre_ref[:, lanes], him_ref[:, lanes]))
        hre_ref[:, lanes] = f_re
        him_ref[:, lanes] = f_im

    y_tb = _s5_output(s_re[...], s_im[...], c_re, c_im)
    for t in range(CHUNK):
        for lt in range(n_lt):
            s_perm[lt, pl.ds(t, nb, stride=PERM_STRIDE), :] = (
                y_tb[t * nb:(t + 1) * nb, lt * LANES:(lt + 1) * LANES])
    y_s5 = jnp.concatenate(
        [jnp.concatenate([s_perm[lt, b * PERM_STRIDE:b * PERM_STRIDE + CHUNK, :] for lt in range(n_lt)],
                         axis=1) for b in range(nb)], axis=0)
    y_s5 = y_s5 + d_ref[...] * u

    gt = jnp.broadcast_to(gt_ref[...], (nb, CHUNK, D_MODEL)).reshape(rows, D_MODEL)
    out = _mix_output(o_gla, y_s5, x, gt, w_glu, b_glu, w_out, ln_g, ln_b)
    o_ref[...] = out.reshape(nb, CHUNK, D_MODEL)


def _mix_weight_list(m, s5):
    return [m["w_in"], m["w_gk"], m["b_gk"], m["norm_g"], s5["ab_re"], s5["ab_im"], s5["bb_re"], s5["bb_im"],
            s5["c_re"], s5["c_im"], s5["d"], m["w_glu"], m["b_glu"], m["w_out"]]


def _mix_prompt(x, mods_p, layer, m, s5, ln_g, ln_b):
    bsz, seq, _ = x.shape
    rows = bsz * CHUNK
    weights = _mix_weight_list(m, s5) + [ln_g, ln_b]
    mod_specs = [
        pl.BlockSpec((None, bsz, 1, D_MODEL), functools.partial(lambda i, col: (layer, 0, 0, col), col=3 + k))
        for k in range(3)
    ]
    out, gla_t, h_re, h_im = pl.pallas_call(
        _mix_prompt_body,
        grid=(seq // CHUNK,),
        in_specs=[pl.BlockSpec((bsz, CHUNK, D_MODEL), lambda i: (0, i, 0))] + mod_specs
        + [_full_spec(w.shape, 1) for w in weights],
        out_specs=[
            pl.BlockSpec((bsz, CHUNK, D_MODEL), lambda i: (0, i, 0)),
            _full_spec((bsz, GLA_VAL, GLA_KEY), 1),
            _full_spec((bsz, S5_CH), 1),
            _full_spec((bsz, S5_CH), 1),
        ],
        out_shape=[
            jax.ShapeDtypeStruct(x.shape, F32),
            jax.ShapeDtypeStruct((bsz, GLA_VAL, GLA_KEY), F32),
            jax.ShapeDtypeStruct((bsz, S5_CH), F32),
            jax.ShapeDtypeStruct((bsz, S5_CH), F32),
        ],
        scratch_shapes=[
            pltpu.VMEM((rows, GLA_KEY), F32), pltpu.VMEM((rows, GLA_KEY), F32),
            pltpu.VMEM((rows, GLA_VAL), F32), pltpu.VMEM((rows, GLA_KEY), F32),
            pltpu.VMEM((rows, GLA_VAL), F32),
            pltpu.VMEM((S5_WIDTH // LANES, bsz * PERM_STRIDE, LANES), F32),
            pltpu.VMEM((rows, S5_CH), F32), pltpu.VMEM((rows, S5_CH), F32),
        ],
        compiler_params=_params(("arbitrary",)),
        name="gla_s5_prompt",
    )(x, mods_p, mods_p, mods_p, *weights)
    s6 = gla_t.reshape(bsz, GLA_HEADS, GLA_DV, GLA_HEADS, GLA_DK)
    gla = jnp.stack([s6[:, hh, :, hh, :] for hh in range(GLA_HEADS)], axis=1).transpose(0, 1, 3, 2)
    return (out, gla, h_re.reshape(bsz, S5_GROUPS, S5_STATE), h_im.reshape(bsz, S5_GROUPS, S5_STATE))


def _mix_sample_pre_body(x_ref, sh_ref, sc_ref, hre_ref, him_ref,
                         w_in, w_gk, b_gk, ab_re_ref, ab_im_ref, bb_re, bb_im, c_re, c_im, d_ref,
                         q_ref, k_ref, v_ref, gg_ref, dec_ref, y_ref, nre_ref, nim_ref):
    h = x_ref[...] * (1.0 + sc_ref[...]) + sh_ref[...]
    q, k, v, gg, u, gk = _mix_project(h, w_in, w_gk, b_gk)
    q_ref[...] = q
    k_ref[...] = k
    v_ref[...] = v
    gg_ref[...] = gg
    dec_ref[...] = jnp.exp(gk)
    bu_re, bu_im = _s5_input(u, bb_re, bb_im)
    a_re = ab_re_ref[...]
    a_im = ab_im_ref[...]
    h_re = hre_ref[...]
    h_im = him_ref[...]
    n_re = a_re * h_re - a_im * h_im + bu_re
    n_im = a_re * h_im + a_im * h_re + bu_im
    nre_ref[...] = n_re
    nim_ref[...] = n_im
    y_ref[...] = _s5_output(n_re, n_im, c_re, c_im) + d_ref[...] * u


def _gla_sample_step_body(s_ref, qc_ref, kc_ref, dc_ref, v_ref, so_ref, o_ref):
    for j in range(SAMPLE_BLOCK):
        for hh in range(GLA_HEADS):
            rows = slice(hh * GLA_DK, (hh + 1) * GLA_DK)
            lanes = slice(hh * GLA_DV, (hh + 1) * GLA_DV)
            S1 = dc_ref[rows, j:j + 1] * s_ref[j, hh] + kc_ref[rows, j:j + 1] * v_ref[j:j + 1, lanes]
            so_ref[j, hh] = S1
            o_ref[j:j + 1, lanes] = jnp.sum(qc_ref[rows, j:j + 1] * S1, axis=0, keepdims=True)


def _mix_sample_post_body(o_ref_in, gg_ref, y_ref, x_ref, gt_ref, norm_g, w_glu, b_glu, w_out, ln_g, ln_b,
                          o_ref):
    o_gla = _gla_finish(o_ref_in[...], gg_ref[...], norm_g)
    o_ref[...] = _mix_output(o_gla, y_ref[...], x_ref[...], gt_ref[...], w_glu, b_glu, w_out, ln_g, ln_b)


def _mix_sample(x, mods_s, layer, s_gla, s_re, s_im, m, s5, ln_g, ln_b):
    n = x.shape[0]
    ms = _mod_specs_sample(layer, 3, n)
    row_spec = _full_spec((n, D_MODEL), 1)

    def sds(w):
        return jax.ShapeDtypeStruct((n, w), F32)

    def fs(w):
        return _full_spec((n, w), 1)

    pre_w = [m["w_in"], m["w_gk"], m["b_gk"], s5["ab_re"], s5["ab_im"], s5["bb_re"], s5["bb_im"],
             s5["c_re"], s5["c_im"], s5["d"]]
    widths = [GLA_KEY, GLA_KEY, GLA_VAL, GLA_VAL, GLA_KEY, S5_WIDTH, S5_CH, S5_CH]
    q, k, v, gg, dec, y_s5, n_re, n_im = pl.pallas_call(
        _mix_sample_pre_body,
        grid=(1,),
        in_specs=[row_spec, ms[0], ms[1], fs(S5_CH), fs(S5_CH)] + [_full_spec(a.shape, 1) for a in pre_w],
        out_specs=[fs(w) for w in widths],
        out_shape=[sds(w) for w in widths],
        compiler_params=_params(("arbitrary",)),
        name="gla_s5_sample_pre",
    )(x, mods_s, mods_s, s_re.reshape(n, S5_CH), s_im.reshape(n, S5_CH), *pre_w)

    nblk = n // SAMPLE_BLOCK
    st_spec = pl.BlockSpec((SAMPLE_BLOCK, GLA_HEADS, GLA_DK, GLA_DV), lambda i: (i, 0, 0, 0))
    col_spec = pl.BlockSpec((None, GLA_KEY, SAMPLE_BLOCK), lambda i: (i, 0, 0))
    v_spec = pl.BlockSpec((SAMPLE_BLOCK, GLA_VAL), lambda i: (i, 0))
    s_new, o = pl.pallas_call(
        _gla_sample_step_body,
        grid=(nblk,),
        in_specs=[st_spec, col_spec, col_spec, col_spec, v_spec],
        out_specs=[st_spec, v_spec],
        out_shape=[jax.ShapeDtypeStruct(s_gla.shape, F32), sds(GLA_VAL)],
        compiler_params=_params(("arbitrary",)),
        name="gla_sample_step",
    )(s_gla, _to_columns(q), _to_columns(k), _to_columns(dec), v)

    post_w = [m["norm_g"], m["w_glu"], m["b_glu"], m["w_out"], ln_g, ln_b]
    out = pl.pallas_call(
        _mix_sample_post_body,
        grid=(1,),
        in_specs=[fs(GLA_VAL), fs(GLA_VAL), fs(S5_WIDTH), row_spec, ms[2]]
        + [_full_spec(a.shape, 1) for a in post_w],
        out_specs=row_spec,
        out_shape=sds(D_MODEL),
        compiler_params=_params(("arbitrary",)),
        name="gla_s5_sample_post",
    )(o, gg, y_s5, x, mods_s, *post_w)
    return (out, s_new, n_re.reshape(n, S5_GROUPS, S5_STATE), n_im.reshape(n, S5_GROUPS, S5_STATE))


def _pad_to(a, axis, size):
    pad = [(0, 0)] * a.ndim
    pad[axis] = (0, size - a.shape[axis])
    return jnp.pad(a, pad)


def _row(a):
    return a.reshape(1, -1).astype(F32)


def _prep_ffn(wg, wu, wd):
    return (jnp.transpose(wg, (0, 2, 1)).astype(BF16), jnp.transpose(wu, (0, 2, 1)).astype(BF16),
            wd.astype(BF16))


def _prep_mix(w_in, w_out, gla_w_gk, gla_b_gk, gla_norm_g, s5_w_glu, s5_b_glu):
    q, k, v, g, gk_low, u = jnp.split(
        w_in, (GLA_KEY, 2 * GLA_KEY, 2 * GLA_KEY + GLA_VAL, 2 * GLA_KEY + 2 * GLA_VAL,
               2 * GLA_KEY + 2 * GLA_VAL + GLA_GATE_RANK), axis=1)
    w_in_r = jnp.concatenate([q, k, v, g, u, _pad_to(gk_low, 1, LANES)], axis=1)
    return dict(
        w_in=w_in_r.astype(BF16), w_out=w_out.astype(BF16),
        w_gk=_pad_to(gla_w_gk, 0, LANES).astype(BF16), b_gk=_row(gla_b_gk), norm_g=_row(gla_norm_g),
        w_glu=s5_w_glu.astype(BF16), b_glu=_row(s5_b_glu),
    )


def _prep_rwkv(mu, w_r, w_k, w_v, w_o, w0, w1, w2, a0, a1, a2, g1, g2, k_k, k_a, r_k, lnx_g, lnx_b):
    return dict(
        mu=_pad_to(mu, 0, SUBLANES), wr=w_r.astype(BF16), wk=w_k.astype(BF16), wv=w_v.astype(BF16),
        wo=w_o.astype(BF16), w0=_row(w0),
        w1=_pad_to(w1, 1, LORA_PAD).astype(BF16), w2=_pad_to(w2, 0, LORA_PAD).astype(BF16),
        a0=_row(a0),
        a1=_pad_to(a1, 1, LORA_PAD).astype(BF16), a2=_pad_to(a2, 0, LORA_PAD).astype(BF16),
        g1=_pad_to(g1, 1, GATE_LORA_PAD).astype(BF16), g2=_pad_to(g2, 0, GATE_LORA_PAD).astype(BF16),
        k_k=_row(k_k), k_a=_row(k_a), r_k=_row(r_k), lnx_g=_row(lnx_g), lnx_b=_row(lnx_b),
    )


def kernel(x_prompt, x_sample, state_gla, state_s5_re, state_s5_im, state_rwkv_shift, state_rwkv_wkv,
           c_prompt, c_sample, ada_w, ada_b, ln_g, ln_b,
           ffn1_wg, ffn1_wu, ffn1_wd, ffn2_wg, ffn2_wu, ffn2_wd,
           w_in, w_out, gla_w_gk, gla_b_gk, gla_norm_g,
           s5_a_re, s5_a_im, s5_log_step, s5_b_re, s5_b_im, s5_c_re, s5_c_im, s5_d, s5_w_glu, s5_b_glu,
           rwkv_mu, rwkv_w_r, rwkv_w_k, rwkv_w_v, rwkv_w_o, rwkv_w0, rwkv_w1, rwkv_w2,
           rwkv_a0, rwkv_a1, rwkv_a2, rwkv_g1, rwkv_g2, rwkv_k_k, rwkv_k_a, rwkv_r_k,
           rwkv_lnx_g, rwkv_lnx_b):
    bp = x_prompt.shape[0]
    ns = x_sample.shape[0]

    mods_s, mods_p = _ada_mods(c_sample, c_prompt, ada_w, ada_b)
    mods_p = mods_p.reshape(DEPTH, bp, 1, N_MODS * D_MODEL)

    ffn1 = [_prep_ffn(ffn1_wg, ffn1_wu, ffn1_wd)] * DEPTH
    ffn2 = [_prep_ffn(ffn2_wg, ffn2_wu, ffn2_wd)] * DEPTH
    mix = _prep_mix(w_in, w_out, gla_w_gk, gla_b_gk, gla_norm_g, s5_w_glu, s5_b_glu)
    s5 = _s5_prepare(s5_a_re, s5_a_im, s5_log_step, s5_b_re, s5_b_im, s5_c_re, s5_c_im, s5_d)
    rwkv = _prep_rwkv(rwkv_mu, rwkv_w_r, rwkv_w_k, rwkv_w_v, rwkv_w_o, rwkv_w0, rwkv_w1, rwkv_w2,
                      rwkv_a0, rwkv_a1, rwkv_a2, rwkv_g1, rwkv_g2, rwkv_k_k, rwkv_k_a, rwkv_r_k,
                      rwkv_lnx_g, rwkv_lnx_b)

    def lnp(layer, idx):
        return _row(ln_g[layer, idx]), _row(ln_b[layer, idx])

    x = x_prompt
    x = _ffn_prompt(x, mods_p, 0, 0, *ffn1[0], *lnp(0, 0))
    x, gla_p, s5_re_p, s5_im_p = _mix_prompt(x, mods_p, 0, mix, s5, *lnp(0, 1))
    x = _ffn_prompt(x, mods_p, 0, 6, *ffn2[0], *lnp(0, 2))
    x = _ffn_prompt(x, mods_p, 1, 0, *ffn1[1], *lnp(1, 0))
    x, shift_p, wkv_p = _rwkv_prompt(x, mods_p, 1, rwkv, *lnp(1, 1))
    y_prompt = _ffn_prompt(x, mods_p, 1, 6, *ffn2[1], *lnp(1, 2))

    xs = x_sample.reshape(ns, D_MODEL)
    xs = _ffn_sample(xs, mods_s, 0, 0, *ffn1[0], *lnp(0, 0))
    xs, gla_s, s5_re_s, s5_im_s = _mix_sample(xs, mods_s, 0, state_gla, state_s5_re, state_s5_im,
                                              mix, s5, *lnp(0, 1))
    xs = _ffn_sample(xs, mods_s, 0, 6, *ffn2[0], *lnp(0, 2))
    xs = _ffn_sample(xs, mods_s, 1, 0, *ffn1[1], *lnp(1, 0))
    xs, shift_s, wkv_s = _rwkv_sample(xs, mods_s, 1, state_rwkv_shift, state_rwkv_wkv, rwkv, *lnp(1, 1))
    xs = _ffn_sample(xs, mods_s, 1, 6, *ffn2[1], *lnp(1, 2))
    y_sample = xs.reshape(ns, 1, D_MODEL)

    return (y_prompt, y_sample, gla_p, s5_re_p, s5_im_p, shift_p, wkv_p,
            gla_s, s5_re_s, s5_im_s, shift_s, wkv_s)
```

```python
import functools
import math

import jax
import jax.numpy as jnp
from jax import lax
from jax.experimental import pallas as pl
from jax.experimental.pallas import tpu as pltpu

F32 = jnp.float32
BF16 = jnp.bfloat16

D_MODEL = 1024
DEPTH = 2
DN_ALPHA = (2 * DEPTH) ** 0.25
LN_EPS = 1e-5
RMS_EPS = 1e-5
FFN_RES = 0.5
D_FF = 2752
N_MODS = 9

GLA_HEADS = 4
GLA_DK = 64
GLA_DV = 128
GLA_KEY = GLA_HEADS * GLA_DK
GLA_VAL = GLA_HEADS * GLA_DV
GLA_GATE_RANK = 16
GLA_GATE_NORM = 16.0

S5_GROUP = 16
S5_WIDTH = 512
S5_GROUPS = S5_WIDTH // S5_GROUP
S5_STATE = 64
S5_CH = S5_GROUPS * S5_STATE

RWKV_HEAD = 64
RWKV_HEADS = D_MODEL // RWKV_HEAD
RWKV_LNX_EPS = 64e-5
NORM_EPS = 1e-12

LANES = 128
SUBLANES = 8
MXU_DIM = 256
VMEM_LIMIT = 58 * 1024 * 1024

CHUNK = 64
HEADS_PER_GROUP = MXU_DIM // RWKV_HEAD
N_GROUPS = RWKV_HEADS // HEADS_PER_GROUP
FF_CHUNKS = ((0, 1024), (1024, 2048), (2048, D_FF))
LORA_PAD = 128
GATE_LORA_PAD = 256
PERM_STRIDE = CHUNK + SUBLANES
SAMPLE_BLOCK = 8


def _dot(a, b):
    return jnp.dot(a, b, preferred_element_type=F32)


def _dot_nt(a, b):
    return lax.dot_general(a, b, (((1,), (1,)), ((), ())), preferred_element_type=F32)


def _bdot(a, w_ref_or_val):
    return _dot(a.astype(BF16), w_ref_or_val)


def _sigmoid(x):
    return 1.0 / (1.0 + jnp.exp(-x))


def _silu(x):
    return x * _sigmoid(x)


def _softplus(x):
    return jnp.maximum(x, 0.0) + jnp.log(1.0 + jnp.exp(-jnp.abs(x)))


def _gelu_tanh(x):
    c = math.sqrt(2.0 / math.pi)
    return 0.5 * x * (1.0 + jnp.tanh(c * (x + 0.044715 * (x * x * x))))


def _layer_norm(x, g, b):
    mu = jnp.mean(x, axis=-1, keepdims=True)
    d = x - mu
    var = jnp.mean(d * d, axis=-1, keepdims=True)
    return d * lax.rsqrt(var + LN_EPS) * g + b


def _split3(x):
    hi = x.astype(BF16)
    r1 = x - hi.astype(F32)
    mid = r1.astype(BF16)
    lo = (r1 - mid.astype(F32)).astype(BF16)
    return hi, mid, lo


def _exact_dot_left01(m01, x):
    hi, mid, lo = _split3(x)
    return _dot(m01, hi) + _dot(m01, mid) + _dot(m01, lo)


def _dot_right01(x, m01):
    hi = x.astype(BF16)
    lo = (x - hi.astype(F32)).astype(BF16)
    return _dot(hi, m01) + _dot(lo, m01)


def _seg_ones(seg):
    r = lax.broadcasted_iota(jnp.int32, (MXU_DIM, MXU_DIM), 0) // seg
    c = lax.broadcasted_iota(jnp.int32, (MXU_DIM, MXU_DIM), 1) // seg
    return jnp.where(r == c, 1.0, 0.0).astype(BF16)


def _seg_sum(x, seg, split=False):
    ones = _seg_ones(seg)
    tiles = []
    for i in range(x.shape[1] // MXU_DIM):
        xt = x[:, i * MXU_DIM:(i + 1) * MXU_DIM]
        tiles.append(_dot_right01(xt, ones) if split else _dot(xt.astype(BF16), ones))
    return jnp.concatenate(tiles, axis=1)


def _tril_ones(n):
    r = lax.broadcasted_iota(jnp.int32, (n, n), 0)
    c = lax.broadcasted_iota(jnp.int32, (n, n), 1)
    return jnp.where(r >= c, 1.0, 0.0).astype(BF16)


def _chunk_cumsum(x, n_batch):
    tri = _tril_ones(CHUNK)
    parts = [_exact_dot_left01(tri, x[b * CHUNK:(b + 1) * CHUNK, :]) for b in range(n_batch)]
    return jnp.concatenate(parts, axis=0)


def _head_stack(z, n_heads, width):
    head = lax.broadcasted_iota(jnp.int32, z.shape, 1) // width
    return jnp.concatenate([jnp.where(head == h, z, 0.0) for h in range(n_heads)], axis=0)


def _block_diag_mask(rows, cols, rblk, cblk):
    r = lax.broadcasted_iota(jnp.int32, (rows, cols), 0) // rblk
    c = lax.broadcasted_iota(jnp.int32, (rows, cols), 1) // cblk
    return r == c


def _full_spec(shape, grid_rank):
    zeros = (0,) * len(shape)
    if grid_rank == 1:
        return pl.BlockSpec(shape, lambda i: zeros, pipeline_mode=pl.Buffered(1))
    return pl.BlockSpec(shape, lambda i, j: zeros, pipeline_mode=pl.Buffered(1))


def _params(semantics):
    return pltpu.CompilerParams(dimension_semantics=semantics, vmem_limit_bytes=VMEM_LIMIT)


ADA_TN = 1152


def _ada_body(cs_ref, cp_ref, w_ref, b_ref, os_ref, op_ref):
    w = w_ref[...].astype(BF16)
    os_ref[...] = _bdot(_silu(cs_ref[...]), w) + b_ref[...]
    op_ref[...] = _bdot(_silu(cp_ref[...]), w) + b_ref[...]


def _ada_mods(c_sample, c_prompt, ada_w, ada_b):
    ns, bp = c_sample.shape[0], c_prompt.shape[0]
    width = N_MODS * D_MODEL
    return pl.pallas_call(
        _ada_body,
        grid=(DEPTH, width // ADA_TN),
        in_specs=[
            pl.BlockSpec((ns, D_MODEL), lambda l, j: (0, 0)),
            pl.BlockSpec((bp, D_MODEL), lambda l, j: (0, 0)),
            pl.BlockSpec((None, D_MODEL, ADA_TN), lambda l, j: (l, 0, j)),
            pl.BlockSpec((None, 1, ADA_TN), lambda l, j: (l, 0, j)),
        ],
        out_specs=[pl.BlockSpec((None, ns, ADA_TN), lambda l, j: (l, 0, j)),
                   pl.BlockSpec((None, bp, ADA_TN), lambda l, j: (l, 0, j))],
        out_shape=[jax.ShapeDtypeStruct((DEPTH, ns, width), F32),
                   jax.ShapeDtypeStruct((DEPTH, bp, width), F32)],
        compiler_params=_params(("arbitrary", "arbitrary")),
        name="ada_mods",
    )(c_sample, c_prompt, ada_w, ada_b.reshape(DEPTH, 1, width))


FFN_TM = 512


def _ffn_tile(x, sh, sc, gt, wg_ref, wu_ref, wd_ref, g_ref, b_ref):
    h = (x * (1.0 + sc) + sh).astype(BF16)
    acc = None
    for lo, hi in FF_CHUNKS:
        g = _dot_nt(h, wg_ref[lo:hi, :])
        u = _dot_nt(h, wu_ref[lo:hi, :])
        y = _bdot(_silu(g) * u, wd_ref[lo:hi, :])
        acc = y if acc is None else acc + y
    r = FFN_RES * (1.0 + gt) * acc
    return _layer_norm(DN_ALPHA * x + r, g_ref[...], b_ref[...])


def _ffn_body(x_ref, sh_ref, sc_ref, gt_ref, xs_ref, shs_ref, scs_ref, gts_ref,
              wg_ref, wu_ref, wd_ref, g_ref, b_ref, o_ref, os_ref):
    last = pl.num_programs(0) - 1

    @pl.when(pl.program_id(0) < last)
    def _():
        o_ref[...] = _ffn_tile(x_ref[...], sh_ref[...], sc_ref[...], gt_ref[...],
                               wg_ref, wu_ref, wd_ref, g_ref, b_ref)

    @pl.when(pl.program_id(0) == last)
    def _():
        os_ref[...] = _ffn_tile(xs_ref[...], shs_ref[...], scs_ref[...], gts_ref[...],
                                wg_ref, wu_ref, wd_ref, g_ref, b_ref)


def _mod_specs_sample(layer, first, rows):
    return [
        pl.BlockSpec((None, rows, D_MODEL), functools.partial(
            lambda i, col: (layer, 0, col), col=first + k))
        for k in range(3)
    ]


def _ffn(x, xs, mods_p, mods_s, layer, first, wg, wu, wd, ln_g, ln_b):
    bsz, seq, _ = x.shape
    n = xs.shape[0]
    per_b = seq // FFN_TM
    n_tiles = bsz * per_b

    def tile_idx(t):
        tc = jnp.minimum(t, n_tiles - 1)
        return tc // per_b, tc % per_b

    mod_specs = [
        pl.BlockSpec((None, None, 1, D_MODEL), functools.partial(
            lambda t, col: (layer, tile_idx(t)[0], 0, col), col=first + k))
        for k in range(3)
    ]
    w_spec = pl.BlockSpec((None, D_FF, D_MODEL), lambda t: (layer, 0, 0), pipeline_mode=pl.Buffered(1))
    tile = pl.BlockSpec((None, FFN_TM, D_MODEL), lambda t: (*tile_idx(t), 0))
    return pl.pallas_call(
        _ffn_body,
        grid=(n_tiles + 1,),
        in_specs=[tile] + mod_specs + [_full_spec((n, D_MODEL), 1)] + _mod_specs_sample(layer, first, n)
        + [w_spec] * 3 + [_full_spec((1, D_MODEL), 1)] * 2,
        out_specs=[tile, _full_spec((n, D_MODEL), 1)],
        out_shape=[jax.ShapeDtypeStruct(x.shape, F32), jax.ShapeDtypeStruct(xs.shape, F32)],
        compiler_params=_params(("arbitrary",)),
        name="ffn",
    )(x, mods_p, mods_p, mods_p, xs, mods_s, mods_s, mods_s, wg, wu, wd, ln_g, ln_b)


def _rwkv_project(h, prev, mu_ref, wr, wk, wv, w0, w1, w2, a0, a1, a2, kk_ref, ka_ref):
    xx = prev - h
    mu = mu_ref[...]

    def mix(i):
        return h + xx * mu[i:i + 1, :]

    r = _bdot(mix(0), wr[...])
    k = _bdot(mix(2), wk[...])
    v = _bdot(mix(3), wv[...])
    wl = w0[...] + _bdot(jnp.tanh(_bdot(mix(1), w1[...])), w2[...])
    log_decay = -math.exp(-0.5) * _sigmoid(wl)
    a_sig = _sigmoid(a0[...] + _bdot(_bdot(mix(4), a1[...]), a2[...]))
    kk = k * kk_ref[...]
    norm = jnp.sqrt(_seg_sum(kk * kk, RWKV_HEAD))
    kk = kk / jnp.maximum(norm, NORM_EPS)
    k2 = k * (1.0 + (a_sig - 1.0) * ka_ref[...])
    return r, log_decay, k2, v, -kk, kk * a_sig, mix(5).astype(BF16)


def _rwkv_bonus(r, k2, v, rk_ref):
    return _seg_sum(r * k2 * rk_ref[...], RWKV_HEAD) * v


def _rwkv_output(y, bonus, xg, x, gt, g1, g2, wo, lng, lnb, ln_g, ln_b):
    gate = _bdot(_sigmoid(_dot(xg, g1[...])), g2[...])
    inv_n = 1.0 / RWKV_HEAD
    mu_y = _seg_sum(y, RWKV_HEAD, split=True) * inv_n
    d = y - mu_y
    var = _seg_sum(d * d, RWKV_HEAD) * inv_n
    y = d * lax.rsqrt(var + RWKV_LNX_EPS) * lng[...] + lnb[...]
    out = _bdot((y + bonus) * gate, wo[...])
    return _layer_norm(DN_ALPHA * x + (1.0 + gt) * out, ln_g[...], ln_b[...])


def _rwkv_weight_list(p):
    return [p["mu"], p["wr"], p["wk"], p["wv"], p["w0"], p["w1"], p["w2"], p["a0"], p["a1"], p["a2"],
            p["k_k"], p["k_a"], p["r_k"]]


def _rwkv_out_weight_list(p, ln_g, ln_b):
    return [p["g1"], p["g2"], p["wo"], p["lnx_g"], p["lnx_b"], ln_g, ln_b]


NEUMANN_LEVELS = int(math.log2(CHUNK)) - 1
WKV_BATCH_UNROLL = 2


def _wkv_chunks(problems):
    c = CHUNK
    hp = HEADS_PER_GROUP
    w = hp * RWKV_HEAD
    n = hp * c
    t_idx = lax.broadcasted_iota(jnp.int32, (c, w), 0)
    s_idx = lax.broadcasted_iota(jnp.int32, (c, w), 1) % c
    strict = t_idx > s_idx
    incl = t_idx >= s_idx
    eye = jnp.where(t_idx == s_idx, 1.0, 0.0)
    bd_mask = _block_diag_mask(w, w, RWKV_HEAD, RWKV_HEAD)
    idx = range(len(problems))
    At, Rt, Kt, Bt, V, S0, dec = zip(*problems)

    def bf(x):
        return x.astype(BF16)

    def stack(x, width):
        return bf(_head_stack(x, hp, width))

    X2 = [jnp.concatenate([At[i], Rt[i]], axis=0) for i in idx]
    KB = [jnp.concatenate([stack(Kt[i], RWKV_HEAD), stack(Bt[i], RWKV_HEAD)], axis=0) for i in idx]
    sc = [_dot_nt(X2[i], KB[i]) for i in idx]
    XS = [_dot_nt(X2[i], bf(S0[i])) for i in idx]
    Vbd = [stack(V[i], RWKV_HEAD) for i in idx]
    Aab = [jnp.where(strict, sc[i][:c, n:], 0.0) for i in idx]
    rhs = [XS[i][:c] + _dot(bf(jnp.where(strict, sc[i][:c, :n], 0.0)), Vbd[i]) for i in idx]

    Q = Aab
    P = [eye + Aab[i] for i in idx]
    Qbd = [stack(Q[i], c) for i in idx]
    for _ in range(NEUMANN_LEVELS):
        Q = [_dot(bf(Q[i]), Qbd[i]) for i in idx]
        Qbd = [stack(Q[i], c) for i in idx]
        P = [P[i] + _dot(bf(P[i]), Qbd[i]) for i in idx]
    U = [_dot(bf(P[i]), stack(rhs[i], RWKV_HEAD)) for i in idx]

    Y = [XS[i][c:] + _dot(bf(jnp.where(incl, sc[i][c:, :n], 0.0)), Vbd[i])
         + _dot(bf(jnp.where(incl, sc[i][c:, n:], 0.0)), stack(U[i], RWKV_HEAD)) for i in idx]

    upd = [_dot(bf(jnp.concatenate([V[i], U[i]], axis=0).T),
                bf(jnp.concatenate([Kt[i], Bt[i]], axis=0))) for i in idx]
    S1 = [(S0[i] + jnp.where(bd_mask, upd[i], 0.0)) * dec[i] for i in idx]
    return Y, S1


def _rwkv_prompt_body(x_ref, sh_ref, sc_ref, gt_ref,
                      mu, wr, wk, wv, w0, w1, w2, a0, a1, a2, kk_ref, ka_ref, rk_ref,
                      g1, g2, wo, lng, lnb, ln_g, ln_b,
                      o_ref, shift_ref, state_ref,
                      carry, s_at, s_rt, s_kt, s_bt, s_v, s_dec, s_y, s_bonus, s_xg):
    nb = x_ref.shape[0]
    rows = nb * CHUNK
    gw = HEADS_PER_GROUP * RWKV_HEAD
    step = pl.program_id(0)

    @pl.when(step == 0)
    def _():
        carry[...] = jnp.zeros(carry.shape, F32)
        state_ref[...] = jnp.zeros(state_ref.shape, F32)

    h3 = x_ref[...] * (1.0 + sc_ref[...]) + sh_ref[...]
    h = h3.reshape(rows, D_MODEL)
    first = lax.broadcasted_iota(jnp.int32, (rows, D_MODEL), 0) % CHUNK == 0
    carried = jnp.broadcast_to(carry[...], (nb, CHUNK, D_MODEL)).reshape(rows, D_MODEL)
    prev = jnp.where(first, carried, pltpu.roll(h, 1, axis=0))
    carry[...] = h3[:, CHUNK - 1:CHUNK, :]
    shift_ref[...] = h3[:, CHUNK - 1:CHUNK, :]

    r, lw, k2, v, av, bv, xg = _rwkv_project(
        h, prev, mu, wr, wk, wv, w0, w1, w2, a0, a1, a2, kk_ref, ka_ref)
    s_xg[...] = xg
    s_bonus[...] = _rwkv_bonus(r, k2, v, rk_ref)

    g = _chunk_cumsum(lw, nb)
    g3 = g.reshape(nb, CHUNK, D_MODEL)
    e_neg = jnp.exp(-g)
    arrays = (
        (s_at, av * jnp.exp(g - lw)),
        (s_rt, r * jnp.exp(g)),
        (s_kt, k2 * e_neg),
        (s_bt, bv * e_neg),
        (s_v, v),
    )
    for ref, val in arrays:
        for gi in range(N_GROUPS):
            ref[gi] = val[:, gi * gw:(gi + 1) * gw].astype(ref.dtype)
    dec = jnp.exp(g3[:, CHUNK - 1:CHUNK, :])
    for gi in range(N_GROUPS):
        s_dec[gi] = dec[:, :, gi * gw:(gi + 1) * gw]

    def chunk_step(it, carry_val):
        keys = []
        for db in range(WKV_BATCH_UNROLL):
            b = it * WKV_BATCH_UNROLL + db
            sl = pl.ds(pl.multiple_of(b * CHUNK, CHUNK), CHUNK)
            keys += [(b, gi, sl) for gi in range(N_GROUPS)]
        Y, S1 = _wkv_chunks([
            (s_at[gi, sl, :], s_rt[gi, sl, :], s_kt[gi, sl, :], s_bt[gi, sl, :],
             s_v[gi, sl, :], state_ref[b, gi], s_dec[gi, b])
            for b, gi, sl in keys])
        for (b, gi, sl), y_val, s_val in zip(keys, Y, S1):
            s_y[gi, sl, :] = y_val
            state_ref[b, gi] = s_val
        return carry_val

    lax.fori_loop(0, nb // WKV_BATCH_UNROLL, chunk_step, 0)

    y = jnp.concatenate([s_y[gi] for gi in range(N_GROUPS)], axis=1)
    gt = jnp.broadcast_to(gt_ref[...], (nb, CHUNK, D_MODEL)).reshape(rows, D_MODEL)
    x = x_ref[...].reshape(rows, D_MODEL)
    out = _rwkv_output(y, s_bonus[...], s_xg[...], x, gt, g1, g2, wo, lng, lnb, ln_g, ln_b)
    o_ref[...] = out.reshape(nb, CHUNK, D_MODEL)


def _rwkv_prompt(x, mods_p, layer, p, ln_g, ln_b):
    bsz, seq, _ = x.shape
    rows = bsz * CHUNK
    gw = HEADS_PER_GROUP * RWKV_HEAD
    weights = _rwkv_weight_list(p) + _rwkv_out_weight_list(p, ln_g, ln_b)
    mod_specs = [
        pl.BlockSpec((None, bsz, 1, D_MODEL), functools.partial(lambda i, col: (layer, 0, 0, col), col=3 + k))
        for k in range(3)
    ]
    slab = pltpu.VMEM((N_GROUPS, rows, gw), F32)
    slab16 = pltpu.VMEM((N_GROUPS, rows, gw), BF16)
    out, shift, state = pl.pallas_call(
        _rwkv_prompt_body,
        grid=(seq // CHUNK,),
        in_specs=[pl.BlockSpec((bsz, CHUNK, D_MODEL), lambda i: (0, i, 0))] + mod_specs
        + [_full_spec(w.shape, 1) for w in weights],
        out_specs=[
            pl.BlockSpec((bsz, CHUNK, D_MODEL), lambda i: (0, i, 0)),
            _full_spec((bsz, 1, D_MODEL), 1),
            _full_spec((bsz, N_GROUPS, gw, gw), 1),
        ],
        out_shape=[
            jax.ShapeDtypeStruct(x.shape, F32),
            jax.ShapeDtypeStruct((bsz, 1, D_MODEL), F32),
            jax.ShapeDtypeStruct((bsz, N_GROUPS, gw, gw), F32),
        ],
        scratch_shapes=[pltpu.VMEM((bsz, 1, D_MODEL), F32), slab16, slab16, slab, slab, slab]
        + [pltpu.VMEM((N_GROUPS, bsz, 1, gw), F32), slab,
           pltpu.VMEM((rows, D_MODEL), F32), pltpu.VMEM((rows, D_MODEL), BF16)],
        compiler_params=_params(("arbitrary",)),
        name="rwkv_prompt",
    )(x, mods_p, mods_p, mods_p, *weights)
    s5d = state.reshape(bsz, N_GROUPS, HEADS_PER_GROUP, RWKV_HEAD, gw)
    wkv = jnp.stack([s5d[:, :, hh, :, hh * RWKV_HEAD:(hh + 1) * RWKV_HEAD]
                     for hh in range(HEADS_PER_GROUP)], axis=2)
    return out, shift.reshape(bsz, D_MODEL), wkv.reshape(bsz, RWKV_HEADS, RWKV_HEAD, RWKV_HEAD)


def _rwkv_sample_pre_body(x_ref, sh_ref, sc_ref, prev_ref,
                          mu, wr, wk, wv, w0, w1, w2, a0, a1, a2, kk_ref, ka_ref, rk_ref,
                          h_ref, rt_ref, wt_ref, kt_ref, vt_ref, at_ref, bt_ref, xg_ref, bonus_ref):
    h = x_ref[...] * (1.0 + sc_ref[...]) + sh_ref[...]
    r, lw, k2, v, av, bv, xg = _rwkv_project(
        h, prev_ref[...], mu, wr, wk, wv, w0, w1, w2, a0, a1, a2, kk_ref, ka_ref)
    h_ref[...] = h
    rt_ref[...] = r.T
    wt_ref[...] = jnp.exp(lw).T
    kt_ref[...] = k2.T
    vt_ref[...] = v.T
    at_ref[...] = av.T
    bt_ref[...] = bv.T
    xg_ref[...] = xg
    bonus_ref[...] = _rwkv_bonus(r, k2, v, rk_ref)


def _rwkv_sample_step_body(s_ref, r_ref, w_ref, k_ref, v_ref, a_ref, b_ref, so_ref, y_ref):
    a, w, b, k, r = a_ref[...], w_ref[...], b_ref[...], k_ref[...], r_ref[...]

    def row_step(i, carry_val):
        S = s_ref[i]
        sa = jnp.sum(S * a, axis=0, keepdims=True)
        S1 = S * w + sa * b + v_ref[pl.ds(i, 1), :] * k
        so_ref[i] = S1
        y_ref[pl.ds(i, 1), :] = jnp.sum(S1 * r, axis=0, keepdims=True)
        return carry_val

    lax.fori_loop(0, RWKV_HEAD, row_step, 0, unroll=2)


def _rwkv_sample_post_body(yt_ref, bonus_ref, xg_ref, x_ref, gt_ref,
                           g1, g2, wo, lng, lnb, ln_g, ln_b, o_ref):
    o_ref[...] = _rwkv_output(yt_ref[...].T, bonus_ref[...], xg_ref[...], x_ref[...], gt_ref[...],
                              g1, g2, wo, lng, lnb, ln_g, ln_b)


def _to_columns(a):
    n, f = a.shape
    return a.reshape(n // SAMPLE_BLOCK, SAMPLE_BLOCK, f).transpose(0, 2, 1)


def _rwkv_sample(x, mods_s, layer, s_shift, s_wkv, p, ln_g, ln_b):
    n = x.shape[0]
    row = jax.ShapeDtypeStruct((n, D_MODEL), F32)
    row_spec = _full_spec((n, D_MODEL), 1)
    pre_w = _rwkv_weight_list(p)
    ms = _mod_specs_sample(layer, 3, n)
    col = jax.ShapeDtypeStruct((D_MODEL, n), F32)
    col_spec = _full_spec((D_MODEL, n), 1)
    h, *vecs, xg, bonus = pl.pallas_call(
        _rwkv_sample_pre_body,
        grid=(1,),
        in_specs=[row_spec, ms[0], ms[1], row_spec] + [_full_spec(a.shape, 1) for a in pre_w],
        out_specs=[row_spec] + [col_spec] * 6 + [row_spec] * 2,
        out_shape=[row] + [col] * 6 + [jax.ShapeDtypeStruct((n, D_MODEL), BF16), row],
        compiler_params=_params(("arbitrary",)),
        name="rwkv_sample_pre",
    )(x, mods_s, mods_s, s_shift, *pre_w)

    st_spec = pl.BlockSpec((None, RWKV_HEAD, RWKV_HEAD, n), lambda hd: (hd, 0, 0, 0))
    vec_spec = pl.BlockSpec((RWKV_HEAD, n), lambda hd: (hd, 0))
    st_new, yt = pl.pallas_call(
        _rwkv_sample_step_body,
        grid=(RWKV_HEADS,),
        in_specs=[st_spec] + [vec_spec] * len(vecs),
        out_specs=[st_spec, vec_spec],
        out_shape=[jax.ShapeDtypeStruct((RWKV_HEADS, RWKV_HEAD, RWKV_HEAD, n), F32), col],
        compiler_params=_params(("arbitrary",)),
        name="rwkv_sample_step",
    )(jnp.transpose(s_wkv, (1, 2, 3, 0)), *vecs)
    s_new = jnp.transpose(st_new, (3, 0, 1, 2))

    post_w = _rwkv_out_weight_list(p, ln_g, ln_b)
    out = pl.pallas_call(
        _rwkv_sample_post_body,
        grid=(1,),
        in_specs=[col_spec] + [row_spec] * 3 + [ms[2]] + [_full_spec(a.shape, 1) for a in post_w],
        out_specs=row_spec,
        out_shape=row,
        compiler_params=_params(("arbitrary",)),
        name="rwkv_sample_post",
    )(yt, bonus, xg, x, mods_s, *post_w)
    return out, h, s_new


def _s5_prep_body(are_ref, aim_ref, ls_ref, bre_ref, bim_ref, abre_ref, abim_ref, bbre_ref, bbim_ref):
    a_re = are_ref[...]
    a_im = aim_ref[...]
    dt = jnp.exp(ls_ref[...])
    mag = jnp.exp(a_re * dt)
    ab_re = mag * jnp.cos(a_im * dt)
    ab_im = mag * jnp.sin(a_im * dt)
    den = a_re * a_re + a_im * a_im
    nr = ab_re - 1.0
    z_re = (nr * a_re + ab_im * a_im) / den
    z_im = (ab_im * a_re - nr * a_im) / den
    b_re = bre_ref[...]
    b_im = bim_ref[...]
    abre_ref[...] = ab_re
    abim_ref[...] = ab_im
    bbre_ref[...] = z_re * b_re - z_im * b_im
    bbim_ref[...] = z_re * b_im + z_im * b_re


def _s5_prepare(s5_a_re, s5_a_im, s5_log_step, s5_b_re, s5_b_im, s5_c_re, s5_c_im, s5_d):
    G, P, C = S5_GROUPS, S5_STATE, S5_GROUP
    small = jax.ShapeDtypeStruct((G, 1, P), F32)
    big = jax.ShapeDtypeStruct((G, C, P), F32)
    ls = jnp.broadcast_to(s5_log_step.reshape(G, 1, 1), (G, 1, P))
    ab_re, ab_im, bb_re, bb_im = pl.pallas_call(
        _s5_prep_body,
        grid=(1,),
        in_specs=[_full_spec((G, 1, P), 1)] * 3 + [_full_spec((G, C, P), 1)] * 2,
        out_specs=[_full_spec((G, 1, P), 1)] * 2 + [_full_spec((G, C, P), 1)] * 2,
        out_shape=[small, small, big, big],
        name="s5_prepare",
    )(s5_a_re.reshape(G, 1, P), s5_a_im.reshape(G, 1, P), ls,
      s5_b_re.transpose(0, 2, 1), s5_b_im.transpose(0, 2, 1))

    eye = jnp.eye(G, dtype=F32)
    gpt = LANES // C
    n_tiles = G // gpt

    def in_blocks(bb):
        full = (bb[:, :, None, :] * eye[:, None, :, None]).reshape(G * C, G * P)
        return jnp.stack([full[k * LANES:(k + 1) * LANES, k * gpt * P:(k + 1) * gpt * P]
                          for k in range(n_tiles)]).astype(BF16)

    def out_blocks(cc):
        full = (cc.transpose(0, 2, 1)[:, :, None, :] * eye[:, None, :, None]).reshape(G * P, G * C)
        return jnp.stack([full[k * gpt * P:(k + 1) * gpt * P, k * LANES:(k + 1) * LANES]
                          for k in range(n_tiles)]).astype(BF16)

    return dict(
        ab_re=ab_re.reshape(1, G * P), ab_im=ab_im.reshape(1, G * P),
        bb_re=in_blocks(bb_re), bb_im=in_blocks(bb_im),
        c_re=out_blocks(s5_c_re), c_im=out_blocks(s5_c_im),
        d=s5_d.reshape(1, G * C),
    )


def _s5_input(u, bb_re_ref, bb_im_ref):
    n_tiles = bb_re_ref.shape[0]
    ub = u.astype(BF16)
    re = [_dot(ub[:, k * LANES:(k + 1) * LANES], bb_re_ref[k]) for k in range(n_tiles)]
    im = [_dot(ub[:, k * LANES:(k + 1) * LANES], bb_im_ref[k]) for k in range(n_tiles)]
    return jnp.concatenate(re, axis=1), jnp.concatenate(im, axis=1)


def _s5_output(h_re, h_im, c_re_ref, c_im_ref):
    n_tiles = c_re_ref.shape[0]
    w = h_re.shape[1] // n_tiles
    hr = h_re.astype(BF16)
    hi = h_im.astype(BF16)
    ys = [_dot(hr[:, k * w:(k + 1) * w], c_re_ref[k]) - _dot(hi[:, k * w:(k + 1) * w], c_im_ref[k])
          for k in range(n_tiles)]
    return jnp.concatenate(ys, axis=1)


C_Q, C_K, C_V, C_G, C_U, C_GK, C_END = 0, 256, 512, 1024, 1536, 2048, 2176


def _mix_project(h, w_in, w_gk, b_gk):
    p = _bdot(h, w_in[...])
    q = p[:, C_Q:C_K] * (GLA_DK ** -0.5)
    k = p[:, C_K:C_V]
    v = p[:, C_V:C_G]
    gg = p[:, C_G:C_U]
    u = p[:, C_U:C_GK]
    z = _bdot(p[:, C_GK:C_END], w_gk[...]) + b_gk[...]
    gk = -_softplus(-z) * (1.0 / GLA_GATE_NORM)
    return q, k, v, gg, u, gk


def _gla_finish(o, gg, norm_g):
    parts = []
    for hh in range(GLA_HEADS):
        oh = o[:, hh * GLA_DV:(hh + 1) * GLA_DV]
        parts.append(oh * lax.rsqrt(jnp.mean(oh * oh, axis=-1, keepdims=True) + RMS_EPS) * norm_g[...])
    return jnp.concatenate(parts, axis=1) * _silu(gg)


def _mix_output(o_gla, y_s5, x, gt, w_glu, b_glu, w_out, ln_g, ln_b):
    z = _gelu_tanh(y_s5)
    o_s5 = z * _sigmoid(_bdot(z, w_glu[...]) + b_glu[...])
    out = _bdot(o_gla, w_out[0:GLA_VAL, :]) + _bdot(o_s5, w_out[GLA_VAL:GLA_VAL + S5_WIDTH, :])
    return _layer_norm(DN_ALPHA * x + (1.0 + gt) * out, ln_g[...], ln_b[...])


GLA_BATCH_UNROLL = 4


def _gla_chunks(problems):
    c = CHUNK
    mid = c // 2 - 1
    idx = range(len(problems))
    q, k, v, bc, ST = zip(*problems)
    t_idx = lax.broadcasted_iota(jnp.int32, (c, GLA_KEY), 0)
    s_idx = lax.broadcasted_iota(jnp.int32, (c, GLA_KEY), 1) % c
    causal = t_idx >= s_idx
    bd_mask = _block_diag_mask(GLA_VAL, GLA_KEY, GLA_DV, GLA_DK)

    def bf(x):
        return x.astype(BF16)

    b_mid = [bc[i][mid:mid + 1, :] for i in idx]
    b_last = [bc[i][c - 1:c, :] for i in idx]
    q_in = [bf(q[i] * jnp.exp(bc[i] - b_mid[i])) for i in idx]
    k_in = [bf(_head_stack(k[i] * jnp.exp(b_mid[i] - bc[i]), GLA_HEADS, GLA_DK)) for i in idx]
    scores = [jnp.where(causal, _dot_nt(q_in[i], k_in[i]), 0.0) for i in idx]
    o_inter = [_dot_nt(bf(q[i] * jnp.exp(bc[i])), bf(ST[i])) for i in idx]
    upd = [_dot(bf(v[i].T), bf(k[i] * jnp.exp(b_last[i] - bc[i]))) for i in idx]
    o = [o_inter[i] + _dot(bf(scores[i]), bf(_head_stack(v[i], GLA_HEADS, GLA_DV))) for i in idx]
    ST1 = [ST[i] * jnp.exp(b_last[i]) + jnp.where(bd_mask, upd[i], 0.0) for i in idx]
    return o, ST1


def _mix_prompt_body(x_ref, sh_ref, sc_ref, gt_ref,
                     w_in, w_gk, b_gk, norm_g, ab_re_ref, ab_im_ref, bb_re, bb_im, c_re, c_im, d_ref,
                     w_glu, b_glu, w_out, ln_g, ln_b,
                     o_ref, gla_ref, hre_ref, him_ref,
                     s_q, s_k, s_v, s_bc, s_o, s_perm, s_re, s_im):
    nb = x_ref.shape[0]
    rows = nb * CHUNK
    step = pl.program_id(0)

    @pl.when(step == 0)
    def _():
        gla_ref[...] = jnp.zeros(gla_ref.shape, F32)
        hre_ref[...] = jnp.zeros(hre_ref.shape, F32)
        him_ref[...] = jnp.zeros(him_ref.shape, F32)

    x3 = x_ref[...]
    h3 = x3 * (1.0 + sc_ref[...]) + sh_ref[...]
    x = x3.reshape(rows, D_MODEL)
    h = h3.reshape(rows, D_MODEL)
    q, k, v, gg, u, gk = _mix_project(h, w_in, w_gk, b_gk)

    s_q[...] = q
    s_k[...] = k
    s_v[...] = v
    s_bc[...] = _chunk_cumsum(gk, nb)

    def gla_step(it, carry_val):
        keys = []
        for db in range(GLA_BATCH_UNROLL):
            b = it * GLA_BATCH_UNROLL + db
            keys.append((b, pl.ds(pl.multiple_of(b * CHUNK, CHUNK), CHUNK)))
        o, ST1 = _gla_chunks([(s_q[sl, :], s_k[sl, :], s_v[sl, :], s_bc[sl, :], gla_ref[b])
                              for b, sl in keys])
        for (b, sl), o_val, st_val in zip(keys, o, ST1):
            s_o[sl, :] = o_val
            gla_ref[b] = st_val
        return carry_val

    lax.fori_loop(0, nb // GLA_BATCH_UNROLL, gla_step, 0)
    o_gla = _gla_finish(s_o[...], gg, norm_g)

    n_lt = S5_WIDTH // LANES
    for b in range(nb):
        for lt in range(n_lt):
            s_perm[lt, b * PERM_STRIDE:b * PERM_STRIDE + CHUNK, :] = (
                u[b * CHUNK:(b + 1) * CHUNK, lt * LANES:(lt + 1) * LANES])
    u_tb = jnp.concatenate(
        [jnp.concatenate([s_perm[lt, pl.ds(t, nb, stride=PERM_STRIDE), :] for lt in range(n_lt)], axis=1)
         for t in range(CHUNK)], axis=0)
    bu_re, bu_im = _s5_input(u_tb, bb_re, bb_im)
    s_re[...] = bu_re
    s_im[...] = bu_im

    n_ch = 4
    cw = S5_CH // n_ch
    for ci in range(n_ch):
        lanes = slice(ci * cw, (ci + 1) * cw)
        a_re = jnp.broadcast_to(ab_re_ref[:, lanes], (nb, cw))
        a_im = jnp.broadcast_to(ab_im_ref[:, lanes], (nb, cw))

        def scan_step(t, hc, lanes=lanes, a_re=a_re, a_im=a_im):
            h_re, h_im = hc
            sl = pl.ds(pl.multiple_of(t * nb, nb), nb)
            n_re = a_re * h_re - a_im * h_im + s_re[sl, lanes]
            n_im = a_re * h_im + a_im * h_re + s_im[sl, lanes]
            s_re[sl, lanes] = n_re
            s_im[sl, lanes] = n_im
            return n_re, n_im

        f_re, f_im = lax.fori_loop(0, CHUNK, scan_step, (hre_ref[:, lanes], him_ref[:, lanes]))
        hre_ref[:, lanes] = f_re
        him_ref[:, lanes] = f_im

    y_tb = _s5_output(s_re[...], s_im[...], c_re, c_im)
    for t in range(CHUNK):
        for lt in range(n_lt):
            s_perm[lt, pl.ds(t, nb, stride=PERM_STRIDE), :] = (
                y_tb[t * nb:(t + 1) * nb, lt * LANES:(lt + 1) * LANES])
    y_s5 = jnp.concatenate(
        [jnp.concatenate([s_perm[lt, b * PERM_STRIDE:b * PERM_STRIDE + CHUNK, :] for lt in range(n_lt)],
                         axis=1) for b in range(nb)], axis=0)
    y_s5 = y_s5 + d_ref[...] * u

    gt = jnp.broadcast_to(gt_ref[...], (nb, CHUNK, D_MODEL)).reshape(rows, D_MODEL)
    out = _mix_output(o_gla, y_s5, x, gt, w_glu, b_glu, w_out, ln_g, ln_b)
    o_ref[...] = out.reshape(nb, CHUNK, D_MODEL)


def _mix_weight_list(m, s5):
    return [m["w_in"], m["w_gk"], m["b_gk"], m["norm_g"], s5["ab_re"], s5["ab_im"], s5["bb_re"], s5["bb_im"],
            s5["c_re"], s5["c_im"], s5["d"], m["w_glu"], m["b_glu"], m["w_out"]]


def _mix_prompt(x, mods_p, layer, m, s5, ln_g, ln_b):
    bsz, seq, _ = x.shape
    rows = bsz * CHUNK
    weights = _mix_weight_list(m, s5) + [ln_g, ln_b]
    mod_specs = [
        pl.BlockSpec((None, bsz, 1, D_MODEL), functools.partial(lambda i, col: (layer, 0, 0, col), col=3 + k))
        for k in range(3)
    ]
    out, gla_t, h_re, h_im = pl.pallas_call(
        _mix_prompt_body,
        grid=(seq // CHUNK,),
        in_specs=[pl.BlockSpec((bsz, CHUNK, D_MODEL), lambda i: (0, i, 0))] + mod_specs
        + [_full_spec(w.shape, 1) for w in weights],
        out_specs=[
            pl.BlockSpec((bsz, CHUNK, D_MODEL), lambda i: (0, i, 0)),
            _full_spec((bsz, GLA_VAL, GLA_KEY), 1),
            _full_spec((bsz, S5_CH), 1),
            _full_spec((bsz, S5_CH), 1),
        ],
        out_shape=[
            jax.ShapeDtypeStruct(x.shape, F32),
            jax.ShapeDtypeStruct((bsz, GLA_VAL, GLA_KEY), F32),
            jax.ShapeDtypeStruct((bsz, S5_CH), F32),
            jax.ShapeDtypeStruct((bsz, S5_CH), F32),
        ],
        scratch_shapes=[
            pltpu.VMEM((rows, GLA_KEY), F32), pltpu.VMEM((rows, GLA_KEY), F32),
            pltpu.VMEM((rows, GLA_VAL), F32), pltpu.VMEM((rows, GLA_KEY), F32),
            pltpu.VMEM((rows, GLA_VAL), F32),
            pltpu.VMEM((S5_WIDTH // LANES, bsz * PERM_STRIDE, LANES), F32),
            pltpu.VMEM((rows, S5_CH), F32), pltpu.VMEM((rows, S5_CH), F32),
        ],
        compiler_params=_params(("arbitrary",)),
        name="gla_s5_prompt",
    )(x, mods_p, mods_p, mods_p, *weights)
    s4 = gla_t.reshape(bsz, GLA_HEADS, GLA_DV, GLA_KEY)
    gla = jnp.stack([s4[:, hh, :, hh * GLA_DK:(hh + 1) * GLA_DK] for hh in range(GLA_HEADS)],
                    axis=1).transpose(0, 1, 3, 2)
    return (out, gla, h_re.reshape(bsz, S5_GROUPS, S5_STATE), h_im.reshape(bsz, S5_GROUPS, S5_STATE))


def _mix_sample_pre_body(x_ref, sh_ref, sc_ref, hre_ref, him_ref,
                         w_in, w_gk, b_gk, ab_re_ref, ab_im_ref, bb_re, bb_im, c_re, c_im, d_ref,
                         q_ref, k_ref, v_ref, gg_ref, dec_ref, y_ref, nre_ref, nim_ref):
    h = x_ref[...] * (1.0 + sc_ref[...]) + sh_ref[...]
    q, k, v, gg, u, gk = _mix_project(h, w_in, w_gk, b_gk)
    q_ref[...] = q
    k_ref[...] = k
    v_ref[...] = v
    gg_ref[...] = gg
    dec_ref[...] = jnp.exp(gk)
    bu_re, bu_im = _s5_input(u, bb_re, bb_im)
    a_re = ab_re_ref[...]
    a_im = ab_im_ref[...]
    h_re = hre_ref[...]
    h_im = him_ref[...]
    n_re = a_re * h_re - a_im * h_im + bu_re
    n_im = a_re * h_im + a_im * h_re + bu_im
    nre_ref[...] = n_re
    nim_ref[...] = n_im
    y_ref[...] = _s5_output(n_re, n_im, c_re, c_im) + d_ref[...] * u


def _gla_sample_step_body(s_ref, qc_ref, kc_ref, dc_ref, v_ref, so_ref, o_ref):
    for j in range(SAMPLE_BLOCK):
        for hh in range(GLA_HEADS):
            rows = slice(hh * GLA_DK, (hh + 1) * GLA_DK)
            lanes = slice(hh * GLA_DV, (hh + 1) * GLA_DV)
            S1 = dc_ref[rows, j:j + 1] * s_ref[j, hh] + kc_ref[rows, j:j + 1] * v_ref[j:j + 1, lanes]
            so_ref[j, hh] = S1
            o_ref[j:j + 1, lanes] = jnp.sum(qc_ref[rows, j:j + 1] * S1, axis=0, keepdims=True)


def _mix_sample_post_body(o_ref_in, gg_ref, y_ref, x_ref, gt_ref, norm_g, w_glu, b_glu, w_out, ln_g, ln_b,
                          o_ref):
    o_gla = _gla_finish(o_ref_in[...], gg_ref[...], norm_g)
    o_ref[...] = _mix_output(o_gla, y_ref[...], x_ref[...], gt_ref[...], w_glu, b_glu, w_out, ln_g, ln_b)


def _mix_sample(x, mods_s, layer, s_gla, s_re, s_im, m, s5, ln_g, ln_b):
    n = x.shape[0]
    ms = _mod_specs_sample(layer, 3, n)
    row_spec = _full_spec((n, D_MODEL), 1)

    def sds(w):
        return jax.ShapeDtypeStruct((n, w), F32)

    def fs(w):
        return _full_spec((n, w), 1)

    pre_w = [m["w_in"], m["w_gk"], m["b_gk"], s5["ab_re"], s5["ab_im"], s5["bb_re"], s5["bb_im"],
             s5["c_re"], s5["c_im"], s5["d"]]
    widths = [GLA_KEY, GLA_KEY, GLA_VAL, GLA_VAL, GLA_KEY, S5_WIDTH, S5_CH, S5_CH]
    q, k, v, gg, dec, y_s5, n_re, n_im = pl.pallas_call(
        _mix_sample_pre_body,
        grid=(1,),
        in_specs=[row_spec, ms[0], ms[1], fs(S5_CH), fs(S5_CH)] + [_full_spec(a.shape, 1) for a in pre_w],
        out_specs=[fs(w) for w in widths],
        out_shape=[sds(w) for w in widths],
        compiler_params=_params(("arbitrary",)),
        name="gla_s5_sample_pre",
    )(x, mods_s, mods_s, s_re.reshape(n, S5_CH), s_im.reshape(n, S5_CH), *pre_w)

    nblk = n // SAMPLE_BLOCK
    st_spec = pl.BlockSpec((SAMPLE_BLOCK, GLA_HEADS, GLA_DK, GLA_DV), lambda i: (i, 0, 0, 0))
    col_spec = pl.BlockSpec((None, GLA_KEY, SAMPLE_BLOCK), lambda i: (i, 0, 0))
    v_spec = pl.BlockSpec((SAMPLE_BLOCK, GLA_VAL), lambda i: (i, 0))
    s_new, o = pl.pallas_call(
        _gla_sample_step_body,
        grid=(nblk,),
        in_specs=[st_spec, col_spec, col_spec, col_spec, v_spec],
        out_specs=[st_spec, v_spec],
        out_shape=[jax.ShapeDtypeStruct(s_gla.shape, F32), sds(GLA_VAL)],
        compiler_params=_params(("arbitrary",)),
        name="gla_sample_step",
    )(s_gla, _to_columns(q), _to_columns(k), _to_columns(dec), v)

    post_w = [m["norm_g"], m["w_glu"], m["b_glu"], m["w_out"], ln_g, ln_b]
    out = pl.pallas_call(
        _mix_sample_post_body,
        grid=(1,),
        in_specs=[fs(GLA_VAL), fs(GLA_VAL), fs(S5_WIDTH), row_spec, ms[2]]
        + [_full_spec(a.shape, 1) for a in post_w],
        out_specs=row_spec,
        out_shape=sds(D_MODEL),
        compiler_params=_params(("arbitrary",)),
        name="gla_s5_sample_post",
    )(o, gg, y_s5, x, mods_s, *post_w)
    return (out, s_new, n_re.reshape(n, S5_GROUPS, S5_STATE), n_im.reshape(n, S5_GROUPS, S5_STATE))


def _pad_to(a, axis, size):
    pad = [(0, 0)] * a.ndim
    pad[axis] = (0, size - a.shape[axis])
    return jnp.pad(a, pad)


def _row(a):
    return a.reshape(1, -1).astype(F32)


def _prep_ffn(wg, wu, wd):
    return (jnp.transpose(wg, (0, 2, 1)).astype(BF16), jnp.transpose(wu, (0, 2, 1)).astype(BF16),
            wd.astype(BF16))


def _prep_mix(w_in, w_out, gla_w_gk, gla_b_gk, gla_norm_g, s5_w_glu, s5_b_glu):
    q, k, v, g, gk_low, u = jnp.split(
        w_in, (GLA_KEY, 2 * GLA_KEY, 2 * GLA_KEY + GLA_VAL, 2 * GLA_KEY + 2 * GLA_VAL,
               2 * GLA_KEY + 2 * GLA_VAL + GLA_GATE_RANK), axis=1)
    w_in_r = jnp.concatenate([q, k, v, g, u, _pad_to(gk_low, 1, LANES)], axis=1)
    return dict(
        w_in=w_in_r.astype(BF16), w_out=w_out.astype(BF16),
        w_gk=_pad_to(gla_w_gk, 0, LANES).astype(BF16), b_gk=_row(gla_b_gk), norm_g=_row(gla_norm_g),
        w_glu=s5_w_glu.astype(BF16), b_glu=_row(s5_b_glu),
    )


def _prep_rwkv(mu, w_r, w_k, w_v, w_o, w0, w1, w2, a0, a1, a2, g1, g2, k_k, k_a, r_k, lnx_g, lnx_b):
    return dict(
        mu=_pad_to(mu, 0, SUBLANES), wr=w_r.astype(BF16), wk=w_k.astype(BF16), wv=w_v.astype(BF16),
        wo=w_o.astype(BF16), w0=_row(w0),
        w1=_pad_to(w1, 1, LORA_PAD).astype(BF16), w2=_pad_to(w2, 0, LORA_PAD).astype(BF16),
        a0=_row(a0),
        a1=_pad_to(a1, 1, LORA_PAD).astype(BF16), a2=_pad_to(a2, 0, LORA_PAD).astype(BF16),
        g1=_pad_to(g1, 1, GATE_LORA_PAD).astype(BF16), g2=_pad_to(g2, 0, GATE_LORA_PAD).astype(BF16),
        k_k=_row(k_k), k_a=_row(k_a), r_k=_row(r_k), lnx_g=_row(lnx_g), lnx_b=_row(lnx_b),
    )


def kernel(x_prompt, x_sample, state_gla, state_s5_re, state_s5_im, state_rwkv_shift, state_rwkv_wkv,
           c_prompt, c_sample, ada_w, ada_b, ln_g, ln_b,
           ffn1_wg, ffn1_wu, ffn1_wd, ffn2_wg, ffn2_wu, ffn2_wd,
           w_in, w_out, gla_w_gk, gla_b_gk, gla_norm_g,
           s5_a_re, s5_a_im, s5_log_step, s5_b_re, s5_b_im, s5_c_re, s5_c_im, s5_d, s5_w_glu, s5_b_glu,
           rwkv_mu, rwkv_w_r, rwkv_w_k, rwkv_w_v, rwkv_w_o, rwkv_w0, rwkv_w1, rwkv_w2,
           rwkv_a0, rwkv_a1, rwkv_a2, rwkv_g1, rwkv_g2, rwkv_k_k, rwkv_k_a, rwkv_r_k,
           rwkv_lnx_g, rwkv_lnx_b):
    bp = x_prompt.shape[0]
    ns = x_sample.shape[0]

    mods_s, mods_p = _ada_mods(c_sample, c_prompt, ada_w, ada_b)
    mods_p = mods_p.reshape(DEPTH, bp, 1, N_MODS * D_MODEL)

    ffn1 = _prep_ffn(ffn1_wg, ffn1_wu, ffn1_wd)
    ffn2 = _prep_ffn(ffn2_wg, ffn2_wu, ffn2_wd)
    mix = _prep_mix(w_in, w_out, gla_w_gk, gla_b_gk, gla_norm_g, s5_w_glu, s5_b_glu)
    s5 = _s5_prepare(s5_a_re, s5_a_im, s5_log_step, s5_b_re, s5_b_im, s5_c_re, s5_c_im, s5_d)
    rwkv = _prep_rwkv(rwkv_mu, rwkv_w_r, rwkv_w_k, rwkv_w_v, rwkv_w_o, rwkv_w0, rwkv_w1, rwkv_w2,
                      rwkv_a0, rwkv_a1, rwkv_a2, rwkv_g1, rwkv_g2, rwkv_k_k, rwkv_k_a, rwkv_r_k,
                      rwkv_lnx_g, rwkv_lnx_b)

    def lnp(layer, idx):
        return _row(ln_g[layer, idx]), _row(ln_b[layer, idx])

    x = x_prompt
    xs = x_sample.reshape(ns, D_MODEL)
    x, xs = _ffn(x, xs, mods_p, mods_s, 0, 0, *ffn1, *lnp(0, 0))
    x, gla_p, s5_re_p, s5_im_p = _mix_prompt(x, mods_p, 0, mix, s5, *lnp(0, 1))
    xs, gla_s, s5_re_s, s5_im_s = _mix_sample(xs, mods_s, 0, state_gla, state_s5_re, state_s5_im,
                                              mix, s5, *lnp(0, 1))
    x, xs = _ffn(x, xs, mods_p, mods_s, 0, 6, *ffn2, *lnp(0, 2))
    x, xs = _ffn(x, xs, mods_p, mods_s, 1, 0, *ffn1, *lnp(1, 0))
    x, shift_p, wkv_p = _rwkv_prompt(x, mods_p, 1, rwkv, *lnp(1, 1))
    xs, shift_s, wkv_s = _rwkv_sample(xs, mods_s, 1, state_rwkv_shift, state_rwkv_wkv, rwkv, *lnp(1, 1))
    y_prompt, xs = _ffn(x, xs, mods_p, mods_s, 1, 6, *ffn2, *lnp(1, 2))
    y_sample = xs.reshape(ns, 1, D_MODEL)

    return (y_prompt, y_sample, gla_p, s5_re_p, s5_im_p, shift_p, wkv_p,
            gla_s, s5_re_s, s5_im_s, shift_s, wkv_s)
```

```python
import functools
import math

import jax
import jax.numpy as jnp
from jax import lax
from jax.experimental import pallas as pl
from jax.experimental.pallas import tpu as pltpu

F32 = jnp.float32
BF16 = jnp.bfloat16

D_MODEL = 1024
DEPTH = 2
DN_ALPHA = (2 * DEPTH) ** 0.25
LN_EPS = 1e-5
RMS_EPS = 1e-5
FFN_RES = 0.5
D_FF = 2752
N_MODS = 9

GLA_HEADS = 4
GLA_DK = 64
GLA_DV = 128
GLA_KEY = GLA_HEADS * GLA_DK
GLA_VAL = GLA_HEADS * GLA_DV
GLA_GATE_RANK = 16
GLA_GATE_NORM = 16.0

S5_GROUP = 16
S5_WIDTH = 512
S5_GROUPS = S5_WIDTH // S5_GROUP
S5_STATE = 64
S5_CH = S5_GROUPS * S5_STATE

RWKV_HEAD = 64
RWKV_HEADS = D_MODEL // RWKV_HEAD
RWKV_LNX_EPS = 64e-5
NORM_EPS = 1e-12

LANES = 128
SUBLANES = 8
MXU_DIM = 256
VMEM_LIMIT = 58 * 1024 * 1024

CHUNK = 64
HEADS_PER_GROUP = MXU_DIM // RWKV_HEAD
N_GROUPS = RWKV_HEADS // HEADS_PER_GROUP
FF_CHUNKS = ((0, 1024), (1024, 2048), (2048, D_FF))
LORA_PAD = 128
GATE_LORA_PAD = 256
PERM_STRIDE = CHUNK + SUBLANES
SAMPLE_BLOCK = 8


def _dot(a, b):
    return jnp.dot(a, b, preferred_element_type=F32)


def _dot_nt(a, b):
    return lax.dot_general(a, b, (((1,), (1,)), ((), ())), preferred_element_type=F32)


def _bdot(a, w_ref_or_val):
    return _dot(a.astype(BF16), w_ref_or_val)


def _sigmoid(x):
    return 1.0 / (1.0 + jnp.exp(-x))


def _silu(x):
    return x * _sigmoid(x)


def _softplus(x):
    return jnp.maximum(x, 0.0) + jnp.log(1.0 + jnp.exp(-jnp.abs(x)))


def _gelu_tanh(x):
    c = math.sqrt(2.0 / math.pi)
    return 0.5 * x * (1.0 + jnp.tanh(c * (x + 0.044715 * (x * x * x))))


def _layer_norm(x, g, b):
    mu = jnp.mean(x, axis=-1, keepdims=True)
    d = x - mu
    var = jnp.mean(d * d, axis=-1, keepdims=True)
    return d * lax.rsqrt(var + LN_EPS) * g + b


def _split3(x):
    hi = x.astype(BF16)
    r1 = x - hi.astype(F32)
    mid = r1.astype(BF16)
    lo = (r1 - mid.astype(F32)).astype(BF16)
    return hi, mid, lo


def _exact_dot_left01(m01, x):
    hi, mid, lo = _split3(x)
    return _dot(m01, hi) + _dot(m01, mid) + _dot(m01, lo)


def _dot_right01(x, m01):
    hi = x.astype(BF16)
    lo = (x - hi.astype(F32)).astype(BF16)
    return _dot(hi, m01) + _dot(lo, m01)


def _seg_ones(seg):
    r = lax.broadcasted_iota(jnp.int32, (MXU_DIM, MXU_DIM), 0) // seg
    c = lax.broadcasted_iota(jnp.int32, (MXU_DIM, MXU_DIM), 1) // seg
    return jnp.where(r == c, 1.0, 0.0).astype(BF16)


def _seg_sum(x, seg, split=False):
    ones = _seg_ones(seg)
    tiles = []
    for i in range(x.shape[1] // MXU_DIM):
        xt = x[:, i * MXU_DIM:(i + 1) * MXU_DIM]
        tiles.append(_dot_right01(xt, ones) if split else _dot(xt.astype(BF16), ones))
    return jnp.concatenate(tiles, axis=1)


def _tril_ones(n):
    r = lax.broadcasted_iota(jnp.int32, (n, n), 0)
    c = lax.broadcasted_iota(jnp.int32, (n, n), 1)
    return jnp.where(r >= c, 1.0, 0.0).astype(BF16)


def _chunk_cumsum(x, n_batch):
    tri = _tril_ones(CHUNK)
    parts = [_exact_dot_left01(tri, x[b * CHUNK:(b + 1) * CHUNK, :]) for b in range(n_batch)]
    return jnp.concatenate(parts, axis=0)


def _head_stack(z, n_heads, width):
    head = lax.broadcasted_iota(jnp.int32, z.shape, 1) // width
    return jnp.concatenate([jnp.where(head == h, z, 0.0) for h in range(n_heads)], axis=0)


def _block_diag_mask(rows, cols, rblk, cblk):
    r = lax.broadcasted_iota(jnp.int32, (rows, cols), 0) // rblk
    c = lax.broadcasted_iota(jnp.int32, (rows, cols), 1) // cblk
    return r == c


def _full_spec(shape, grid_rank):
    zeros = (0,) * len(shape)
    if grid_rank == 1:
        return pl.BlockSpec(shape, lambda i: zeros, pipeline_mode=pl.Buffered(1))
    return pl.BlockSpec(shape, lambda i, j: zeros, pipeline_mode=pl.Buffered(1))


def _params(semantics):
    return pltpu.CompilerParams(dimension_semantics=semantics, vmem_limit_bytes=VMEM_LIMIT)


ADA_TN = 1152


def _ada_body(cs_ref, cp_ref, w_ref, b_ref, os_ref, op_ref):
    w = w_ref[...].astype(BF16)
    os_ref[...] = _bdot(_silu(cs_ref[...]), w) + b_ref[...]
    op_ref[...] = _bdot(_silu(cp_ref[...]), w) + b_ref[...]


def _ada_mods(c_sample, c_prompt, ada_w, ada_b):
    ns, bp = c_sample.shape[0], c_prompt.shape[0]
    width = N_MODS * D_MODEL
    return pl.pallas_call(
        _ada_body,
        grid=(DEPTH, width // ADA_TN),
        in_specs=[
            pl.BlockSpec((ns, D_MODEL), lambda l, j: (0, 0)),
            pl.BlockSpec((bp, D_MODEL), lambda l, j: (0, 0)),
            pl.BlockSpec((None, D_MODEL, ADA_TN), lambda l, j: (l, 0, j)),
            pl.BlockSpec((None, 1, ADA_TN), lambda l, j: (l, 0, j)),
        ],
        out_specs=[pl.BlockSpec((None, ns, ADA_TN), lambda l, j: (l, 0, j)),
                   pl.BlockSpec((None, bp, ADA_TN), lambda l, j: (l, 0, j))],
        out_shape=[jax.ShapeDtypeStruct((DEPTH, ns, width), F32),
                   jax.ShapeDtypeStruct((DEPTH, bp, width), F32)],
        compiler_params=_params(("arbitrary", "arbitrary")),
        name="ada_mods",
    )(c_sample, c_prompt, ada_w, ada_b.reshape(DEPTH, 1, width))


FFN_TM = 512


def _ffn_tile(x, sh, sc, gt, wg_ref, wu_ref, wd_ref, g_ref, b_ref):
    h = (x * (1.0 + sc) + sh).astype(BF16)
    acc = None
    for lo, hi in FF_CHUNKS:
        g = _dot_nt(h, wg_ref[lo:hi, :])
        u = _dot_nt(h, wu_ref[lo:hi, :])
        y = _bdot(_silu(g) * u, wd_ref[lo:hi, :])
        acc = y if acc is None else acc + y
    r = FFN_RES * (1.0 + gt) * acc
    return _layer_norm(DN_ALPHA * x + r, g_ref[...], b_ref[...])


def _ffn_body(x_ref, sh_ref, sc_ref, gt_ref, xs_ref, shs_ref, scs_ref, gts_ref,
              wg_ref, wu_ref, wd_ref, g_ref, b_ref, o_ref, os_ref):
    last = pl.num_programs(0) - 1

    @pl.when(pl.program_id(0) < last)
    def _():
        o_ref[...] = _ffn_tile(x_ref[...], sh_ref[...], sc_ref[...], gt_ref[...],
                               wg_ref, wu_ref, wd_ref, g_ref, b_ref)

    @pl.when(pl.program_id(0) == last)
    def _():
        os_ref[...] = _ffn_tile(xs_ref[...], shs_ref[...], scs_ref[...], gts_ref[...],
                                wg_ref, wu_ref, wd_ref, g_ref, b_ref)


def _mod_specs_sample(layer, first, rows):
    return [
        pl.BlockSpec((None, rows, D_MODEL), functools.partial(
            lambda i, col: (layer, 0, col), col=first + k))
        for k in range(3)
    ]


def _ffn(x, xs, mods_p, mods_s, layer, first, wg, wu, wd, ln_g, ln_b):
    bsz, seq, _ = x.shape
    n = xs.shape[0]
    per_b = seq // FFN_TM
    n_tiles = bsz * per_b

    def tile_idx(t):
        tc = jnp.minimum(t, n_tiles - 1)
        return tc // per_b, tc % per_b

    mod_specs = [
        pl.BlockSpec((None, None, 1, D_MODEL), functools.partial(
            lambda t, col: (layer, tile_idx(t)[0], 0, col), col=first + k))
        for k in range(3)
    ]
    w_spec = pl.BlockSpec((None, D_FF, D_MODEL), lambda t: (layer, 0, 0), pipeline_mode=pl.Buffered(1))
    tile = pl.BlockSpec((None, FFN_TM, D_MODEL), lambda t: (*tile_idx(t), 0))
    return pl.pallas_call(
        _ffn_body,
        grid=(n_tiles + 1,),
        in_specs=[tile] + mod_specs + [_full_spec((n, D_MODEL), 1)] + _mod_specs_sample(layer, first, n)
        + [w_spec] * 3 + [_full_spec((1, D_MODEL), 1)] * 2,
        out_specs=[tile, _full_spec((n, D_MODEL), 1)],
        out_shape=[jax.ShapeDtypeStruct(x.shape, F32), jax.ShapeDtypeStruct(xs.shape, F32)],
        compiler_params=_params(("arbitrary",)),
        name="ffn",
    )(x, mods_p, mods_p, mods_p, xs, mods_s, mods_s, mods_s, wg, wu, wd, ln_g, ln_b)


def _col_groups(width):
    return [slice(lo, lo + width) for lo in range(0, D_MODEL, width)]


def _rwkv_mixes(h, prev, mu_ref):
    xx = prev - h
    mu = mu_ref[...]
    return [(h + xx * mu[i:i + 1, :]).astype(BF16) for i in range(6)]


def _rwkv_lora_in(mixes, w1, a1):
    return jnp.tanh(_dot(mixes[1], w1[...])).astype(BF16), _dot(mixes[4], a1[...]).astype(BF16)


def _rwkv_project_cols(cols, mixes, tw, ta, wr, wk, wv, w0, w2, a0, a2, kk_ref, ka_ref):
    r = _dot(mixes[0], wr[:, cols])
    k = _dot(mixes[2], wk[:, cols])
    v = _dot(mixes[3], wv[:, cols])
    wl = w0[:, cols] + _dot(tw, w2[:, cols])
    log_decay = -math.exp(-0.5) * _sigmoid(wl)
    a_sig = _sigmoid(a0[:, cols] + _dot(ta, a2[:, cols]))
    kk = k * kk_ref[:, cols]
    norm = jnp.sqrt(_seg_sum(kk * kk, RWKV_HEAD))
    kk = kk / jnp.maximum(norm, NORM_EPS)
    k2 = k * (1.0 + (a_sig - 1.0) * ka_ref[:, cols])
    return r, log_decay, k2, v, -kk, kk * a_sig


def _rwkv_bonus(r, k2, v, rk):
    return _seg_sum(r * k2 * rk, RWKV_HEAD) * v


def _rwkv_gate_in(xg, g1):
    return _sigmoid(_dot(xg, g1[...])).astype(BF16)


def _rwkv_out_cols(cols, y, bonus, sg, g2, wo, lng, lnb):
    gate = _dot(sg, g2[:, cols])
    inv_n = 1.0 / RWKV_HEAD
    mu_y = _seg_sum(y, RWKV_HEAD, split=True) * inv_n
    d = y - mu_y
    var = _seg_sum(d * d, RWKV_HEAD) * inv_n
    y = d * lax.rsqrt(var + RWKV_LNX_EPS) * lng[:, cols] + lnb[:, cols]
    return _bdot((y + bonus) * gate, wo[cols, :])


def _rwkv_weight_list(p):
    return [p["mu"], p["wr"], p["wk"], p["wv"], p["w0"], p["w1"], p["w2"], p["a0"], p["a1"], p["a2"],
            p["k_k"], p["k_a"], p["r_k"]]


def _rwkv_out_weight_list(p, ln_g, ln_b):
    return [p["g1"], p["g2"], p["wo"], p["lnx_g"], p["lnx_b"], ln_g, ln_b]


NEUMANN_LEVELS = int(math.log2(CHUNK)) - 1
WKV_BATCH_UNROLL = 2


def _wkv_chunks(problems):
    c = CHUNK
    hp = HEADS_PER_GROUP
    w = hp * RWKV_HEAD
    n = hp * c
    t_idx = lax.broadcasted_iota(jnp.int32, (c, w), 0)
    s_idx = lax.broadcasted_iota(jnp.int32, (c, w), 1) % c
    strict = t_idx > s_idx
    incl = t_idx >= s_idx
    eye = jnp.where(t_idx == s_idx, 1.0, 0.0)
    bd_mask = _block_diag_mask(w, w, RWKV_HEAD, RWKV_HEAD)
    idx = range(len(problems))
    At, Rt, Kt, Bt, V, S0, dec = zip(*problems)

    def bf(x):
        return x.astype(BF16)

    def stack(x, width):
        return bf(_head_stack(x, hp, width))

    X2 = [jnp.concatenate([At[i], Rt[i]], axis=0) for i in idx]
    KB = [jnp.concatenate([stack(Kt[i], RWKV_HEAD), stack(Bt[i], RWKV_HEAD)], axis=0) for i in idx]
    sc = [_dot_nt(X2[i], KB[i]) for i in idx]
    XS = [_dot_nt(X2[i], bf(S0[i])) for i in idx]
    Vbd = [stack(V[i], RWKV_HEAD) for i in idx]
    Aab = [jnp.where(strict, sc[i][:c, n:], 0.0) for i in idx]
    rhs = [XS[i][:c] + _dot(bf(jnp.where(strict, sc[i][:c, :n], 0.0)), Vbd[i]) for i in idx]

    Q = Aab
    P = [eye + Aab[i] for i in idx]
    Qbd = [stack(Q[i], c) for i in idx]
    for _ in range(NEUMANN_LEVELS):
        Q = [_dot(bf(Q[i]), Qbd[i]) for i in idx]
        Qbd = [stack(Q[i], c) for i in idx]
        P = [P[i] + _dot(bf(P[i]), Qbd[i]) for i in idx]
    U = [_dot(bf(P[i]), stack(rhs[i], RWKV_HEAD)) for i in idx]

    Y = [XS[i][c:] + _dot(bf(jnp.where(incl, sc[i][c:, :n], 0.0)), Vbd[i])
         + _dot(bf(jnp.where(incl, sc[i][c:, n:], 0.0)), stack(U[i], RWKV_HEAD)) for i in idx]

    upd = [_dot(bf(jnp.concatenate([V[i], U[i]], axis=0).T),
                bf(jnp.concatenate([Kt[i], Bt[i]], axis=0))) for i in idx]
    S1 = [(S0[i] + jnp.where(bd_mask, upd[i], 0.0)) * dec[i] for i in idx]
    return Y, S1


def _rwkv_prompt_body(x_ref, sh_ref, sc_ref, gt_ref,
                      mu, wr, wk, wv, w0, w1, w2, a0, a1, a2, kk_ref, ka_ref, rk_ref,
                      g1, g2, wo, lng, lnb, ln_g, ln_b,
                      o_ref, shift_ref, state_ref,
                      carry, s_at, s_rt, s_kt, s_bt, s_v, s_dec, s_y, s_bonus, s_xg):
    nb = x_ref.shape[0]
    rows = nb * CHUNK
    gw = HEADS_PER_GROUP * RWKV_HEAD
    step = pl.program_id(0)

    @pl.when(step == 0)
    def _():
        carry[...] = jnp.zeros(carry.shape, F32)
        state_ref[...] = jnp.zeros(state_ref.shape, F32)

    h3 = x_ref[...] * (1.0 + sc_ref[...]) + sh_ref[...]
    h = h3.reshape(rows, D_MODEL)
    first = lax.broadcasted_iota(jnp.int32, (rows, D_MODEL), 0) % CHUNK == 0
    carried = jnp.broadcast_to(carry[...], (nb, CHUNK, D_MODEL)).reshape(rows, D_MODEL)
    prev = jnp.where(first, carried, pltpu.roll(h, 1, axis=0))
    carry[...] = h3[:, CHUNK - 1:CHUNK, :]
    shift_ref[...] = h3[:, CHUNK - 1:CHUNK, :]

    mixes = _rwkv_mixes(h, prev, mu)
    s_xg[...] = mixes[5]
    tw, ta = _rwkv_lora_in(mixes, w1, a1)
    for gi, cols in enumerate(_col_groups(gw)):
        r, lw, k2, v, av, bv = _rwkv_project_cols(
            cols, mixes, tw, ta, wr, wk, wv, w0, w2, a0, a2, kk_ref, ka_ref)
        s_bonus[:, cols] = _rwkv_bonus(r, k2, v, rk_ref[:, cols])
        g = _chunk_cumsum(lw, nb)
        e_neg = jnp.exp(-g)
        s_at[gi] = (av * jnp.exp(g - lw)).astype(BF16)
        s_rt[gi] = (r * jnp.exp(g)).astype(BF16)
        s_kt[gi] = k2 * e_neg
        s_bt[gi] = bv * e_neg
        s_v[gi] = v
        s_dec[gi] = jnp.exp(g.reshape(nb, CHUNK, gw)[:, CHUNK - 1:CHUNK, :])

    def chunk_step(it, carry_val):
        keys = []
        for db in range(WKV_BATCH_UNROLL):
            b = it * WKV_BATCH_UNROLL + db
            sl = pl.ds(pl.multiple_of(b * CHUNK, CHUNK), CHUNK)
            keys += [(b, gi, sl) for gi in range(N_GROUPS)]
        Y, S1 = _wkv_chunks([
            (s_at[gi, sl, :], s_rt[gi, sl, :], s_kt[gi, sl, :], s_bt[gi, sl, :],
             s_v[gi, sl, :], state_ref[b, gi], s_dec[gi, b])
            for b, gi, sl in keys])
        for (b, gi, sl), y_val, s_val in zip(keys, Y, S1):
            s_y[gi, sl, :] = y_val
            state_ref[b, gi] = s_val
        return carry_val

    lax.fori_loop(0, nb // WKV_BATCH_UNROLL, chunk_step, 0)

    sg = _rwkv_gate_in(s_xg[...], g1)
    out = None
    for gi, cols in enumerate(_col_groups(gw)):
        part = _rwkv_out_cols(cols, s_y[gi], s_bonus[:, cols], sg, g2, wo, lng, lnb)
        out = part if out is None else out + part
    gt = jnp.broadcast_to(gt_ref[...], (nb, CHUNK, D_MODEL)).reshape(rows, D_MODEL)
    x = x_ref[...].reshape(rows, D_MODEL)
    o_ref[...] = _layer_norm(DN_ALPHA * x + (1.0 + gt) * out,
                             ln_g[...], ln_b[...]).reshape(nb, CHUNK, D_MODEL)


def _rwkv_prompt(x, mods_p, layer, p, ln_g, ln_b):
    bsz, seq, _ = x.shape
    rows = bsz * CHUNK
    gw = HEADS_PER_GROUP * RWKV_HEAD
    weights = _rwkv_weight_list(p) + _rwkv_out_weight_list(p, ln_g, ln_b)
    mod_specs = [
        pl.BlockSpec((None, bsz, 1, D_MODEL), functools.partial(lambda i, col: (layer, 0, 0, col), col=3 + k))
        for k in range(3)
    ]
    slab = pltpu.VMEM((N_GROUPS, rows, gw), F32)
    slab16 = pltpu.VMEM((N_GROUPS, rows, gw), BF16)
    out, shift, state = pl.pallas_call(
        _rwkv_prompt_body,
        grid=(seq // CHUNK,),
        in_specs=[pl.BlockSpec((bsz, CHUNK, D_MODEL), lambda i: (0, i, 0))] + mod_specs
        + [_full_spec(w.shape, 1) for w in weights],
        out_specs=[
            pl.BlockSpec((bsz, CHUNK, D_MODEL), lambda i: (0, i, 0)),
            _full_spec((bsz, 1, D_MODEL), 1),
            _full_spec((bsz, N_GROUPS, gw, gw), 1),
        ],
        out_shape=[
            jax.ShapeDtypeStruct(x.shape, F32),
            jax.ShapeDtypeStruct((bsz, 1, D_MODEL), F32),
            jax.ShapeDtypeStruct((bsz, N_GROUPS, gw, gw), F32),
        ],
        scratch_shapes=[pltpu.VMEM((bsz, 1, D_MODEL), F32), slab16, slab16, slab, slab, slab]
        + [pltpu.VMEM((N_GROUPS, bsz, 1, gw), F32), slab,
           pltpu.VMEM((rows, D_MODEL), F32), pltpu.VMEM((rows, D_MODEL), BF16)],
        compiler_params=_params(("arbitrary",)),
        name="rwkv_prompt",
    )(x, mods_p, mods_p, mods_p, *weights)
    s5d = state.reshape(bsz, N_GROUPS, HEADS_PER_GROUP, RWKV_HEAD, gw)
    wkv = jnp.stack([s5d[:, :, hh, :, hh * RWKV_HEAD:(hh + 1) * RWKV_HEAD]
                     for hh in range(HEADS_PER_GROUP)], axis=2)
    return out, shift.reshape(bsz, D_MODEL), wkv.reshape(bsz, RWKV_HEADS, RWKV_HEAD, RWKV_HEAD)


def _rwkv_sample_pre_body(x_ref, sh_ref, sc_ref, prev_ref,
                          mu, wr, wk, wv, w0, w1, w2, a0, a1, a2, kk_ref, ka_ref, rk_ref,
                          h_ref, rt_ref, wt_ref, kt_ref, vt_ref, at_ref, bt_ref, xg_ref, bonus_ref):
    h = x_ref[...] * (1.0 + sc_ref[...]) + sh_ref[...]
    h_ref[...] = h
    mixes = _rwkv_mixes(h, prev_ref[...], mu)
    xg_ref[...] = mixes[5]
    tw, ta = _rwkv_lora_in(mixes, w1, a1)
    for cols in _col_groups(MXU_DIM):
        r, lw, k2, v, av, bv = _rwkv_project_cols(
            cols, mixes, tw, ta, wr, wk, wv, w0, w2, a0, a2, kk_ref, ka_ref)
        bonus_ref[:, cols] = _rwkv_bonus(r, k2, v, rk_ref[:, cols])
        rt_ref[cols, :] = r.T
        wt_ref[cols, :] = jnp.exp(lw).T
        kt_ref[cols, :] = k2.T
        vt_ref[cols, :] = v.T
        at_ref[cols, :] = av.T
        bt_ref[cols, :] = bv.T


def _rwkv_sample_step_body(s_ref, r_ref, w_ref, k_ref, v_ref, a_ref, b_ref, so_ref, y_ref):
    a, w, b, k, r = a_ref[...], w_ref[...], b_ref[...], k_ref[...], r_ref[...]

    def row_step(i, carry_val):
        S = s_ref[i]
        sa = jnp.sum(S * a, axis=0, keepdims=True)
        S1 = S * w + sa * b + v_ref[pl.ds(i, 1), :] * k
        so_ref[i] = S1
        y_ref[pl.ds(i, 1), :] = jnp.sum(S1 * r, axis=0, keepdims=True)
        return carry_val

    lax.fori_loop(0, RWKV_HEAD, row_step, 0, unroll=2)


def _rwkv_sample_post_body(yt_ref, bonus_ref, xg_ref, x_ref, gt_ref,
                           g1, g2, wo, lng, lnb, ln_g, ln_b, o_ref):
    sg = _rwkv_gate_in(xg_ref[...], g1)
    out = None
    for cols in _col_groups(MXU_DIM):
        part = _rwkv_out_cols(cols, yt_ref[cols, :].T, bonus_ref[:, cols], sg, g2, wo, lng, lnb)
        out = part if out is None else out + part
    o_ref[...] = _layer_norm(DN_ALPHA * x_ref[...] + (1.0 + gt_ref[...]) * out, ln_g[...], ln_b[...])


def _to_columns(a):
    n, f = a.shape
    return a.reshape(n // SAMPLE_BLOCK, SAMPLE_BLOCK, f).transpose(0, 2, 1)


def _rwkv_sample(x, mods_s, layer, s_shift, s_wkv, p, ln_g, ln_b):
    n = x.shape[0]
    row = jax.ShapeDtypeStruct((n, D_MODEL), F32)
    row_spec = _full_spec((n, D_MODEL), 1)
    pre_w = _rwkv_weight_list(p)
    ms = _mod_specs_sample(layer, 3, n)
    col = jax.ShapeDtypeStruct((D_MODEL, n), F32)
    col_spec = _full_spec((D_MODEL, n), 1)
    h, *vecs, xg, bonus = pl.pallas_call(
        _rwkv_sample_pre_body,
        grid=(1,),
        in_specs=[row_spec, ms[0], ms[1], row_spec] + [_full_spec(a.shape, 1) for a in pre_w],
        out_specs=[row_spec] + [col_spec] * 6 + [row_spec] * 2,
        out_shape=[row] + [col] * 6 + [jax.ShapeDtypeStruct((n, D_MODEL), BF16), row],
        compiler_params=_params(("arbitrary",)),
        name="rwkv_sample_pre",
    )(x, mods_s, mods_s, s_shift, *pre_w)

    st_spec = pl.BlockSpec((None, RWKV_HEAD, RWKV_HEAD, n), lambda hd: (hd, 0, 0, 0))
    vec_spec = pl.BlockSpec((RWKV_HEAD, n), lambda hd: (hd, 0))
    st_new, yt = pl.pallas_call(
        _rwkv_sample_step_body,
        grid=(RWKV_HEADS,),
        in_specs=[st_spec] + [vec_spec] * len(vecs),
        out_specs=[st_spec, vec_spec],
        out_shape=[jax.ShapeDtypeStruct((RWKV_HEADS, RWKV_HEAD, RWKV_HEAD, n), F32), col],
        compiler_params=_params(("arbitrary",)),
        name="rwkv_sample_step",
    )(jnp.transpose(s_wkv, (1, 2, 3, 0)), *vecs)
    s_new = jnp.transpose(st_new, (3, 0, 1, 2))

    post_w = _rwkv_out_weight_list(p, ln_g, ln_b)
    out = pl.pallas_call(
        _rwkv_sample_post_body,
        grid=(1,),
        in_specs=[col_spec] + [row_spec] * 3 + [ms[2]] + [_full_spec(a.shape, 1) for a in post_w],
        out_specs=row_spec,
        out_shape=row,
        compiler_params=_params(("arbitrary",)),
        name="rwkv_sample_post",
    )(yt, bonus, xg, x, mods_s, *post_w)
    return out, h, s_new


S5_TILE_GROUPS = LANES // S5_GROUP
S5_TILES = S5_GROUPS // S5_TILE_GROUPS


def _s5_prep_body(are_ref, aim_ref, ls_ref, bre_ref, bim_ref, cre_ref, cim_ref,
                  abre_ref, abim_ref, bbre_ref, bbim_ref, cdre_ref, cdim_ref):
    a_re = are_ref[...]
    a_im = aim_ref[...]
    dt = jnp.exp(ls_ref[...])
    mag = jnp.exp(a_re * dt)
    ab_re = mag * jnp.cos(a_im * dt)
    ab_im = mag * jnp.sin(a_im * dt)
    den = a_re * a_re + a_im * a_im
    nr = ab_re - 1.0
    z_re = (nr * a_re + ab_im * a_im) / den
    z_im = (ab_im * a_re - nr * a_im) / den
    b_re = bre_ref[...]
    b_im = bim_ref[...]
    abre_ref[...] = ab_re
    abim_ref[...] = ab_im
    bb_re = (z_re * b_re - z_im * b_im).astype(BF16)
    bb_im = (z_re * b_im + z_im * b_re).astype(BF16)
    for ref in (bbre_ref, bbim_ref, cdre_ref, cdim_ref):
        ref[...] = jnp.zeros(ref.shape, BF16)
    c, p = S5_GROUP, S5_STATE
    for g in range(S5_GROUPS):
        k, j = divmod(g, S5_TILE_GROUPS)
        bbre_ref[k, j * c:(j + 1) * c, j * p:(j + 1) * p] = bb_re[g]
        bbim_ref[k, j * c:(j + 1) * c, j * p:(j + 1) * p] = bb_im[g]
        cdre_ref[k, j * p:(j + 1) * p, j * c:(j + 1) * c] = cre_ref[g].astype(BF16)
        cdim_ref[k, j * p:(j + 1) * p, j * c:(j + 1) * c] = cim_ref[g].astype(BF16)


def _s5_prepare(s5_a_re, s5_a_im, s5_log_step, s5_b_re, s5_b_im, s5_c_re, s5_c_im, s5_d):
    G, P, C = S5_GROUPS, S5_STATE, S5_GROUP
    small = jax.ShapeDtypeStruct((G, 1, P), F32)
    bb_shape = (S5_TILES, LANES, S5_TILE_GROUPS * P)
    cd_shape = (S5_TILES, S5_TILE_GROUPS * P, LANES)
    ls = jnp.broadcast_to(s5_log_step.reshape(G, 1, 1), (G, 1, P))
    ab_re, ab_im, bb_re, bb_im, cd_re, cd_im = pl.pallas_call(
        _s5_prep_body,
        grid=(1,),
        in_specs=[_full_spec((G, 1, P), 1)] * 3 + [_full_spec((G, C, P), 1)] * 2
        + [_full_spec((G, P, C), 1)] * 2,
        out_specs=[_full_spec((G, 1, P), 1)] * 2 + [_full_spec(bb_shape, 1)] * 2
        + [_full_spec(cd_shape, 1)] * 2,
        out_shape=[small, small] + [jax.ShapeDtypeStruct(bb_shape, BF16)] * 2
        + [jax.ShapeDtypeStruct(cd_shape, BF16)] * 2,
        name="s5_prepare",
    )(s5_a_re.reshape(G, 1, P), s5_a_im.reshape(G, 1, P), ls,
      s5_b_re.transpose(0, 2, 1), s5_b_im.transpose(0, 2, 1),
      s5_c_re.transpose(0, 2, 1), s5_c_im.transpose(0, 2, 1))
    return dict(
        ab_re=ab_re.reshape(1, G * P), ab_im=ab_im.reshape(1, G * P),
        bb_re=bb_re, bb_im=bb_im, c_re=cd_re, c_im=cd_im, d=s5_d.reshape(1, G * C),
    )


def _s5_input(u, bb_re_ref, bb_im_ref):
    n_tiles = bb_re_ref.shape[0]
    ub = u.astype(BF16)
    re = [_dot(ub[:, k * LANES:(k + 1) * LANES], bb_re_ref[k]) for k in range(n_tiles)]
    im = [_dot(ub[:, k * LANES:(k + 1) * LANES], bb_im_ref[k]) for k in range(n_tiles)]
    return jnp.concatenate(re, axis=1), jnp.concatenate(im, axis=1)


def _s5_output(h_re, h_im, c_re_ref, c_im_ref):
    n_tiles = c_re_ref.shape[0]
    w = h_re.shape[1] // n_tiles
    hr = h_re.astype(BF16)
    hi = h_im.astype(BF16)
    ys = [_dot(hr[:, k * w:(k + 1) * w], c_re_ref[k]) - _dot(hi[:, k * w:(k + 1) * w], c_im_ref[k])
          for k in range(n_tiles)]
    return jnp.concatenate(ys, axis=1)


C_Q, C_K, C_V, C_G, C_U, C_GK, C_END = 0, 256, 512, 1024, 1536, 2048, 2176


def _mix_project(h, w_in, w_gk, b_gk):
    p = _bdot(h, w_in[...])
    q = p[:, C_Q:C_K] * (GLA_DK ** -0.5)
    k = p[:, C_K:C_V]
    v = p[:, C_V:C_G]
    gg = p[:, C_G:C_U]
    u = p[:, C_U:C_GK]
    z = _bdot(p[:, C_GK:C_END], w_gk[...]) + b_gk[...]
    gk = -_softplus(-z) * (1.0 / GLA_GATE_NORM)
    return q, k, v, gg, u, gk


def _gla_finish(o, gg, norm_g):
    parts = []
    for hh in range(GLA_HEADS):
        oh = o[:, hh * GLA_DV:(hh + 1) * GLA_DV]
        parts.append(oh * lax.rsqrt(jnp.mean(oh * oh, axis=-1, keepdims=True) + RMS_EPS) * norm_g[...])
    return jnp.concatenate(parts, axis=1) * _silu(gg)


def _mix_output(o_gla, y_s5, x, gt, w_glu, b_glu, w_out, ln_g, ln_b):
    z = _gelu_tanh(y_s5)
    o_s5 = z * _sigmoid(_bdot(z, w_glu[...]) + b_glu[...])
    out = _bdot(o_gla, w_out[0:GLA_VAL, :]) + _bdot(o_s5, w_out[GLA_VAL:GLA_VAL + S5_WIDTH, :])
    return _layer_norm(DN_ALPHA * x + (1.0 + gt) * out, ln_g[...], ln_b[...])


GLA_BATCH_UNROLL = 4


def _gla_chunks(problems):
    c = CHUNK
    mid = c // 2 - 1
    idx = range(len(problems))
    q, k, v, bc, ST = zip(*problems)
    t_idx = lax.broadcasted_iota(jnp.int32, (c, GLA_KEY), 0)
    s_idx = lax.broadcasted_iota(jnp.int32, (c, GLA_KEY), 1) % c
    causal = t_idx >= s_idx
    bd_mask = _block_diag_mask(GLA_VAL, GLA_KEY, GLA_DV, GLA_DK)

    def bf(x):
        return x.astype(BF16)

    b_mid = [bc[i][mid:mid + 1, :] for i in idx]
    b_last = [bc[i][c - 1:c, :] for i in idx]
    q_in = [bf(q[i] * jnp.exp(bc[i] - b_mid[i])) for i in idx]
    k_in = [bf(_head_stack(k[i] * jnp.exp(b_mid[i] - bc[i]), GLA_HEADS, GLA_DK)) for i in idx]
    scores = [jnp.where(causal, _dot_nt(q_in[i], k_in[i]), 0.0) for i in idx]
    o_inter = [_dot_nt(bf(q[i] * jnp.exp(bc[i])), bf(ST[i])) for i in idx]
    upd = [_dot(bf(v[i].T), bf(k[i] * jnp.exp(b_last[i] - bc[i]))) for i in idx]
    o = [o_inter[i] + _dot(bf(scores[i]), bf(_head_stack(v[i], GLA_HEADS, GLA_DV))) for i in idx]
    ST1 = [ST[i] * jnp.exp(b_last[i]) + jnp.where(bd_mask, upd[i], 0.0) for i in idx]
    return o, ST1


def _mix_prompt_body(x_ref, sh_ref, sc_ref, gt_ref,
                     w_in, w_gk, b_gk, norm_g, ab_re_ref, ab_im_ref, bb_re, bb_im, c_re, c_im, d_ref,
                     w_glu, b_glu, w_out, ln_g, ln_b,
                     o_ref, gla_ref, hre_ref, him_ref,
                     s_q, s_k, s_v, s_bc, s_o, s_perm, s_re, s_im):
    nb = x_ref.shape[0]
    rows = nb * CHUNK
    step = pl.program_id(0)

    @pl.when(step == 0)
    def _():
        gla_ref[...] = jnp.zeros(gla_ref.shape, F32)
        hre_ref[...] = jnp.zeros(hre_ref.shape, F32)
        him_ref[...] = jnp.zeros(him_ref.shape, F32)

    x3 = x_ref[...]
    h3 = x3 * (1.0 + sc_ref[...]) + sh_ref[...]
    x = x3.reshape(rows, D_MODEL)
    h = h3.reshape(rows, D_MODEL)
    q, k, v, gg, u, gk = _mix_project(h, w_in, w_gk, b_gk)

    s_q[...] = q
    s_k[...] = k
    s_v[...] = v
    s_bc[...] = _chunk_cumsum(gk, nb)

    def gla_step(it, carry_val):
        keys = []
        for db in range(GLA_BATCH_UNROLL):
            b = it * GLA_BATCH_UNROLL + db
            keys.append((b, pl.ds(pl.multiple_of(b * CHUNK, CHUNK), CHUNK)))
        o, ST1 = _gla_chunks([(s_q[sl, :], s_k[sl, :], s_v[sl, :], s_bc[sl, :], gla_ref[b])
                              for b, sl in keys])
        for (b, sl), o_val, st_val in zip(keys, o, ST1):
            s_o[sl, :] = o_val
            gla_ref[b] = st_val
        return carry_val

    lax.fori_loop(0, nb // GLA_BATCH_UNROLL, gla_step, 0)
    o_gla = _gla_finish(s_o[...], gg, norm_g)

    n_lt = S5_WIDTH // LANES
    for b in range(nb):
        for lt in range(n_lt):
            s_perm[lt, b * PERM_STRIDE:b * PERM_STRIDE + CHUNK, :] = (
                u[b * CHUNK:(b + 1) * CHUNK, lt * LANES:(lt + 1) * LANES])
    u_tb = jnp.concatenate(
        [jnp.concatenate([s_perm[lt, pl.ds(t, nb, stride=PERM_STRIDE), :] for lt in range(n_lt)], axis=1)
         for t in range(CHUNK)], axis=0)
    bu_re, bu_im = _s5_input(u_tb, bb_re, bb_im)
    s_re[...] = bu_re
    s_im[...] = bu_im

    n_ch = 4
    cw = S5_CH // n_ch
    for ci in range(n_ch):
        lanes = slice(ci * cw, (ci + 1) * cw)
        a_re = jnp.broadcast_to(ab_re_ref[:, lanes], (nb, cw))
        a_im = jnp.broadcast_to(ab_im_ref[:, lanes], (nb, cw))

        def scan_step(t, hc, lanes=lanes, a_re=a_re, a_im=a_im):
            h_re, h_im = hc
            sl = pl.ds(pl.multiple_of(t * nb, nb), nb)
            n_re = a_re * h_re - a_im * h_im + s_re[sl, lanes]
            n_im = a_re * h_im + a_im * h_re + s_im[sl, lanes]
            s_re[sl, lanes] = n_re
            s_im[sl, lanes] = n_im
            return n_re, n_im

        f_re, f_im = lax.fori_loop(0, CHUNK, scan_step, (hre_ref[:, lanes], him_ref[:, lanes]))
        hre_ref[:, lanes] = f_re
        him_ref[:, lanes] = f_im

    y_tb = _s5_output(s_re[...], s_im[...], c_re, c_im)
    for t in range(CHUNK):
        for lt in range(n_lt):
            s_perm[lt, pl.ds(t, nb, stride=PERM_STRIDE), :] = (
                y_tb[t * nb:(t + 1) * nb, lt * LANES:(lt + 1) * LANES])
    y_s5 = jnp.concatenate(
        [jnp.concatenate([s_perm[lt, b * PERM_STRIDE:b * PERM_STRIDE + CHUNK, :] for lt in range(n_lt)],
                         axis=1) for b in range(nb)], axis=0)
    y_s5 = y_s5 + d_ref[...] * u

    gt = jnp.broadcast_to(gt_ref[...], (nb, CHUNK, D_MODEL)).reshape(rows, D_MODEL)
    out = _mix_output(o_gla, y_s5, x, gt, w_glu, b_glu, w_out, ln_g, ln_b)
    o_ref[...] = out.reshape(nb, CHUNK, D_MODEL)


def _mix_weight_list(m, s5):
    return [m["w_in"], m["w_gk"], m["b_gk"], m["norm_g"], s5["ab_re"], s5["ab_im"], s5["bb_re"], s5["bb_im"],
            s5["c_re"], s5["c_im"], s5["d"], m["w_glu"], m["b_glu"], m["w_out"]]


def _mix_prompt(x, mods_p, layer, m, s5, ln_g, ln_b):
    bsz, seq, _ = x.shape
    rows = bsz * CHUNK
    weights = _mix_weight_list(m, s5) + [ln_g, ln_b]
    mod_specs = [
        pl.BlockSpec((None, bsz, 1, D_MODEL), functools.partial(lambda i, col: (layer, 0, 0, col), col=3 + k))
        for k in range(3)
    ]
    out, gla_t, h_re, h_im = pl.pallas_call(
        _mix_prompt_body,
        grid=(seq // CHUNK,),
        in_specs=[pl.BlockSpec((bsz, CHUNK, D_MODEL), lambda i: (0, i, 0))] + mod_specs
        + [_full_spec(w.shape, 1) for w in weights],
        out_specs=[
            pl.BlockSpec((bsz, CHUNK, D_MODEL), lambda i: (0, i, 0)),
            _full_spec((bsz, GLA_VAL, GLA_KEY), 1),
            _full_spec((bsz, S5_CH), 1),
            _full_spec((bsz, S5_CH), 1),
        ],
        out_shape=[
            jax.ShapeDtypeStruct(x.shape, F32),
            jax.ShapeDtypeStruct((bsz, GLA_VAL, GLA_KEY), F32),
            jax.ShapeDtypeStruct((bsz, S5_CH), F32),
            jax.ShapeDtypeStruct((bsz, S5_CH), F32),
        ],
        scratch_shapes=[
            pltpu.VMEM((rows, GLA_KEY), F32), pltpu.VMEM((rows, GLA_KEY), F32),
            pltpu.VMEM((rows, GLA_VAL), F32), pltpu.VMEM((rows, GLA_KEY), F32),
            pltpu.VMEM((rows, GLA_VAL), F32),
            pltpu.VMEM((S5_WIDTH // LANES, bsz * PERM_STRIDE, LANES), F32),
            pltpu.VMEM((rows, S5_CH), F32), pltpu.VMEM((rows, S5_CH), F32),
        ],
        compiler_params=_params(("arbitrary",)),
        name="gla_s5_prompt",
    )(x, mods_p, mods_p, mods_p, *weights)
    s4 = gla_t.reshape(bsz, GLA_HEADS, GLA_DV, GLA_KEY)
    gla = jnp.stack([s4[:, hh, :, hh * GLA_DK:(hh + 1) * GLA_DK] for hh in range(GLA_HEADS)],
                    axis=1).transpose(0, 1, 3, 2)
    return (out, gla, h_re.reshape(bsz, S5_GROUPS, S5_STATE), h_im.reshape(bsz, S5_GROUPS, S5_STATE))


def _mix_sample_pre_body(x_ref, sh_ref, sc_ref, hre_ref, him_ref,
                         w_in, w_gk, b_gk, ab_re_ref, ab_im_ref, bb_re, bb_im, c_re, c_im, d_ref,
                         q_ref, k_ref, v_ref, gg_ref, dec_ref, y_ref, nre_ref, nim_ref):
    h = x_ref[...] * (1.0 + sc_ref[...]) + sh_ref[...]
    q, k, v, gg, u, gk = _mix_project(h, w_in, w_gk, b_gk)
    q_ref[...] = q
    k_ref[...] = k
    v_ref[...] = v
    gg_ref[...] = gg
    dec_ref[...] = jnp.exp(gk)
    bu_re, bu_im = _s5_input(u, bb_re, bb_im)
    a_re = ab_re_ref[...]
    a_im = ab_im_ref[...]
    h_re = hre_ref[...]
    h_im = him_ref[...]
    n_re = a_re * h_re - a_im * h_im + bu_re
    n_im = a_re * h_im + a_im * h_re + bu_im
    nre_ref[...] = n_re
    nim_ref[...] = n_im
    y_ref[...] = _s5_output(n_re, n_im, c_re, c_im) + d_ref[...] * u


def _gla_sample_step_body(s_ref, qc_ref, kc_ref, dc_ref, v_ref, so_ref, o_ref):
    for j in range(SAMPLE_BLOCK):
        for hh in range(GLA_HEADS):
            rows = slice(hh * GLA_DK, (hh + 1) * GLA_DK)
            lanes = slice(hh * GLA_DV, (hh + 1) * GLA_DV)
            S1 = dc_ref[rows, j:j + 1] * s_ref[j, hh] + kc_ref[rows, j:j + 1] * v_ref[j:j + 1, lanes]
            so_ref[j, hh] = S1
            o_ref[j:j + 1, lanes] = jnp.sum(qc_ref[rows, j:j + 1] * S1, axis=0, keepdims=True)


def _mix_sample_post_body(o_ref_in, gg_ref, y_ref, x_ref, gt_ref, norm_g, w_glu, b_glu, w_out, ln_g, ln_b,
                          o_ref):
    o_gla = _gla_finish(o_ref_in[...], gg_ref[...], norm_g)
    o_ref[...] = _mix_output(o_gla, y_ref[...], x_ref[...], gt_ref[...], w_glu, b_glu, w_out, ln_g, ln_b)


def _mix_sample(x, mods_s, layer, s_gla, s_re, s_im, m, s5, ln_g, ln_b):
    n = x.shape[0]
    ms = _mod_specs_sample(layer, 3, n)
    row_spec = _full_spec((n, D_MODEL), 1)

    def sds(w):
        return jax.ShapeDtypeStruct((n, w), F32)

    def fs(w):
        return _full_spec((n, w), 1)

    pre_w = [m["w_in"], m["w_gk"], m["b_gk"], s5["ab_re"], s5["ab_im"], s5["bb_re"], s5["bb_im"],
             s5["c_re"], s5["c_im"], s5["d"]]
    widths = [GLA_KEY, GLA_KEY, GLA_VAL, GLA_VAL, GLA_KEY, S5_WIDTH, S5_CH, S5_CH]
    q, k, v, gg, dec, y_s5, n_re, n_im = pl.pallas_call(
        _mix_sample_pre_body,
        grid=(1,),
        in_specs=[row_spec, ms[0], ms[1], fs(S5_CH), fs(S5_CH)] + [_full_spec(a.shape, 1) for a in pre_w],
        out_specs=[fs(w) for w in widths],
        out_shape=[sds(w) for w in widths],
        compiler_params=_params(("arbitrary",)),
        name="gla_s5_sample_pre",
    )(x, mods_s, mods_s, s_re.reshape(n, S5_CH), s_im.reshape(n, S5_CH), *pre_w)

    nblk = n // SAMPLE_BLOCK
    st_spec = pl.BlockSpec((SAMPLE_BLOCK, GLA_HEADS, GLA_DK, GLA_DV), lambda i: (i, 0, 0, 0))
    col_spec = pl.BlockSpec((None, GLA_KEY, SAMPLE_BLOCK), lambda i: (i, 0, 0))
    v_spec = pl.BlockSpec((SAMPLE_BLOCK, GLA_VAL), lambda i: (i, 0))
    s_new, o = pl.pallas_call(
        _gla_sample_step_body,
        grid=(nblk,),
        in_specs=[st_spec, col_spec, col_spec, col_spec, v_spec],
        out_specs=[st_spec, v_spec],
        out_shape=[jax.ShapeDtypeStruct(s_gla.shape, F32), sds(GLA_VAL)],
        compiler_params=_params(("arbitrary",)),
        name="gla_sample_step",
    )(s_gla, _to_columns(q), _to_columns(k), _to_columns(dec), v)

    post_w = [m["norm_g"], m["w_glu"], m["b_glu"], m["w_out"], ln_g, ln_b]
    out = pl.pallas_call(
        _mix_sample_post_body,
        grid=(1,),
        in_specs=[fs(GLA_VAL), fs(GLA_VAL), fs(S5_WIDTH), row_spec, ms[2]]
        + [_full_spec(a.shape, 1) for a in post_w],
        out_specs=row_spec,
        out_shape=sds(D_MODEL),
        compiler_params=_params(("arbitrary",)),
        name="gla_s5_sample_post",
    )(o, gg, y_s5, x, mods_s, *post_w)
    return (out, s_new, n_re.reshape(n, S5_GROUPS, S5_STATE), n_im.reshape(n, S5_GROUPS, S5_STATE))


def _pad_to(a, axis, size):
    pad = [(0, 0)] * a.ndim
    pad[axis] = (0, size - a.shape[axis])
    return jnp.pad(a, pad)


def _row(a):
    return a.reshape(1, -1).astype(F32)


def _prep_ffn(wg, wu, wd):
    return (jnp.transpose(wg, (0, 2, 1)).astype(BF16), jnp.transpose(wu, (0, 2, 1)).astype(BF16),
            wd.astype(BF16))


def _prep_mix(w_in, w_out, gla_w_gk, gla_b_gk, gla_norm_g, s5_w_glu, s5_b_glu):
    q, k, v, g, gk_low, u = jnp.split(
        w_in, (GLA_KEY, 2 * GLA_KEY, 2 * GLA_KEY + GLA_VAL, 2 * GLA_KEY + 2 * GLA_VAL,
               2 * GLA_KEY + 2 * GLA_VAL + GLA_GATE_RANK), axis=1)
    w_in_r = jnp.concatenate([q, k, v, g, u, _pad_to(gk_low, 1, LANES)], axis=1)
    return dict(
        w_in=w_in_r.astype(BF16), w_out=w_out.astype(BF16),
        w_gk=_pad_to(gla_w_gk, 0, LANES).astype(BF16), b_gk=_row(gla_b_gk), norm_g=_row(gla_norm_g),
        w_glu=s5_w_glu.astype(BF16), b_glu=_row(s5_b_glu),
    )


def _prep_rwkv(mu, w_r, w_k, w_v, w_o, w0, w1, w2, a0, a1, a2, g1, g2, k_k, k_a, r_k, lnx_g, lnx_b):
    return dict(
        mu=_pad_to(mu, 0, SUBLANES), wr=w_r.astype(BF16), wk=w_k.astype(BF16), wv=w_v.astype(BF16),
        wo=w_o.astype(BF16), w0=_row(w0),
        w1=_pad_to(w1, 1, LORA_PAD).astype(BF16), w2=_pad_to(w2, 0, LORA_PAD).astype(BF16),
        a0=_row(a0),
        a1=_pad_to(a1, 1, LORA_PAD).astype(BF16), a2=_pad_to(a2, 0, LORA_PAD).astype(BF16),
        g1=_pad_to(g1, 1, GATE_LORA_PAD).astype(BF16), g2=_pad_to(g2, 0, GATE_LORA_PAD).astype(BF16),
        k_k=_row(k_k), k_a=_row(k_a), r_k=_row(r_k), lnx_g=_row(lnx_g), lnx_b=_row(lnx_b),
    )


def kernel(x_prompt, x_sample, state_gla, state_s5_re, state_s5_im, state_rwkv_shift, state_rwkv_wkv,
           c_prompt, c_sample, ada_w, ada_b, ln_g, ln_b,
           ffn1_wg, ffn1_wu, ffn1_wd, ffn2_wg, ffn2_wu, ffn2_wd,
           w_in, w_out, gla_w_gk, gla_b_gk, gla_norm_g,
           s5_a_re, s5_a_im, s5_log_step, s5_b_re, s5_b_im, s5_c_re, s5_c_im, s5_d, s5_w_glu, s5_b_glu,
           rwkv_mu, rwkv_w_r, rwkv_w_k, rwkv_w_v, rwkv_w_o, rwkv_w0, rwkv_w1, rwkv_w2,
           rwkv_a0, rwkv_a1, rwkv_a2, rwkv_g1, rwkv_g2, rwkv_k_k, rwkv_k_a, rwkv_r_k,
           rwkv_lnx_g, rwkv_lnx_b):
    bp = x_prompt.shape[0]
    ns = x_sample.shape[0]

    mods_s, mods_p = _ada_mods(c_sample, c_prompt, ada_w, ada_b)
    mods_p = mods_p.reshape(DEPTH, bp, 1, N_MODS * D_MODEL)

    ffn1 = _prep_ffn(ffn1_wg, ffn1_wu, ffn1_wd)
    ffn2 = _prep_ffn(ffn2_wg, ffn2_wu, ffn2_wd)
    mix = _prep_mix(w_in, w_out, gla_w_gk, gla_b_gk, gla_norm_g, s5_w_glu, s5_b_glu)
    s5 = _s5_prepare(s5_a_re, s5_a_im, s5_log_step, s5_b_re, s5_b_im, s5_c_re, s5_c_im, s5_d)
    rwkv = _prep_rwkv(rwkv_mu, rwkv_w_r, rwkv_w_k, rwkv_w_v, rwkv_w_o, rwkv_w0, rwkv_w1, rwkv_w2,
                      rwkv_a0, rwkv_a1, rwkv_a2, rwkv_g1, rwkv_g2, rwkv_k_k, rwkv_k_a, rwkv_r_k,
                      rwkv_lnx_g, rwkv_lnx_b)

    def lnp(layer, idx):
        return _row(ln_g[layer, idx]), _row(ln_b[layer, idx])

    x = x_prompt
    xs = x_sample.reshape(ns, D_MODEL)
    x, xs = _ffn(x, xs, mods_p, mods_s, 0, 0, *ffn1, *lnp(0, 0))
    x, gla_p, s5_re_p, s5_im_p = _mix_prompt(x, mods_p, 0, mix, s5, *lnp(0, 1))
    xs, gla_s, s5_re_s, s5_im_s = _mix_sample(xs, mods_s, 0, state_gla, state_s5_re, state_s5_im,
                                              mix, s5, *lnp(0, 1))
    x, xs = _ffn(x, xs, mods_p, mods_s, 0, 6, *ffn2, *lnp(0, 2))
    x, xs = _ffn(x, xs, mods_p, mods_s, 1, 0, *ffn1, *lnp(1, 0))
    x, shift_p, wkv_p = _rwkv_prompt(x, mods_p, 1, rwkv, *lnp(1, 1))
    xs, shift_s, wkv_s = _rwkv_sample(xs, mods_s, 1, state_rwkv_shift, state_rwkv_wkv, rwkv, *lnp(1, 1))
    y_prompt, xs = _ffn(x, xs, mods_p, mods_s, 1, 6, *ffn2, *lnp(1, 2))
    y_sample = xs.reshape(ns, 1, D_MODEL)

    return (y_prompt, y_sample, gla_p, s5_re_p, s5_im_p, shift_p, wkv_p,
            gla_s, s5_re_s, s5_im_s, shift_s, wkv_s)
```

```python
import functools
import math

import jax
import jax.numpy as jnp
from jax import lax
from jax.experimental import pallas as pl
from jax.experimental.pallas import tpu as pltpu

F32 = jnp.float32
BF16 = jnp.bfloat16

D_MODEL = 1024
DEPTH = 2
DN_ALPHA = (2 * DEPTH) ** 0.25
LN_EPS = 1e-5
RMS_EPS = 1e-5
FFN_RES = 0.5
D_FF = 2752
N_MODS = 9

GLA_HEADS = 4
GLA_DK = 64
GLA_DV = 128
GLA_KEY = GLA_HEADS * GLA_DK
GLA_VAL = GLA_HEADS * GLA_DV
GLA_GATE_RANK = 16
GLA_GATE_NORM = 16.0

S5_GROUP = 16
S5_WIDTH = 512
S5_GROUPS = S5_WIDTH // S5_GROUP
S5_STATE = 64
S5_CH = S5_GROUPS * S5_STATE

RWKV_HEAD = 64
RWKV_HEADS = D_MODEL // RWKV_HEAD
RWKV_LNX_EPS = 64e-5
NORM_EPS = 1e-12

LANES = 128
SUBLANES = 8
MXU_DIM = 256
VMEM_LIMIT = 58 * 1024 * 1024

CHUNK = 64
HEADS_PER_GROUP = MXU_DIM // RWKV_HEAD
N_GROUPS = RWKV_HEADS // HEADS_PER_GROUP
FF_CHUNKS = ((0, 1024), (1024, 2048), (2048, D_FF))
LORA_PAD = 128
GATE_LORA_PAD = 256
PERM_STRIDE = CHUNK + SUBLANES
SAMPLE_BLOCK = 8


def _dot(a, b):
    return jnp.dot(a, b, preferred_element_type=F32)


def _dot_nt(a, b):
    return lax.dot_general(a, b, (((1,), (1,)), ((), ())), preferred_element_type=F32)


def _bdot(a, w_ref_or_val):
    return _dot(a.astype(BF16), w_ref_or_val)


def _sigmoid(x):
    return 1.0 / (1.0 + jnp.exp(-x))


def _silu(x):
    return x * _sigmoid(x)


def _softplus(x):
    return jnp.maximum(x, 0.0) + jnp.log(1.0 + jnp.exp(-jnp.abs(x)))


def _gelu_tanh(x):
    c = math.sqrt(2.0 / math.pi)
    return 0.5 * x * (1.0 + jnp.tanh(c * (x + 0.044715 * (x * x * x))))


def _layer_norm(x, g, b):
    mu = jnp.mean(x, axis=-1, keepdims=True)
    d = x - mu
    var = jnp.mean(d * d, axis=-1, keepdims=True)
    return d * lax.rsqrt(var + LN_EPS) * g + b


def _split3(x):
    hi = x.astype(BF16)
    r1 = x - hi.astype(F32)
    mid = r1.astype(BF16)
    lo = (r1 - mid.astype(F32)).astype(BF16)
    return hi, mid, lo


def _exact_dot_left01(m01, x):
    hi, mid, lo = _split3(x)
    return _dot(m01, hi) + _dot(m01, mid) + _dot(m01, lo)


def _dot_right01(x, m01):
    hi = x.astype(BF16)
    lo = (x - hi.astype(F32)).astype(BF16)
    return _dot(hi, m01) + _dot(lo, m01)


def _seg_ones(seg):
    r = lax.broadcasted_iota(jnp.int32, (MXU_DIM, MXU_DIM), 0) // seg
    c = lax.broadcasted_iota(jnp.int32, (MXU_DIM, MXU_DIM), 1) // seg
    return jnp.where(r == c, 1.0, 0.0).astype(BF16)


def _seg_sum(x, seg, split=False):
    ones = _seg_ones(seg)
    tiles = []
    for i in range(x.shape[1] // MXU_DIM):
        xt = x[:, i * MXU_DIM:(i + 1) * MXU_DIM]
        tiles.append(_dot_right01(xt, ones) if split else _dot(xt.astype(BF16), ones))
    return jnp.concatenate(tiles, axis=1)


def _tril_ones(n):
    r = lax.broadcasted_iota(jnp.int32, (n, n), 0)
    c = lax.broadcasted_iota(jnp.int32, (n, n), 1)
    return jnp.where(r >= c, 1.0, 0.0).astype(BF16)


def _chunk_cumsum(x, n_batch):
    tri = _tril_ones(CHUNK)
    parts = [_exact_dot_left01(tri, x[b * CHUNK:(b + 1) * CHUNK, :]) for b in range(n_batch)]
    return jnp.concatenate(parts, axis=0)


def _head_stack(z, n_heads, width):
    head = lax.broadcasted_iota(jnp.int32, z.shape, 1) // width
    return jnp.concatenate([jnp.where(head == h, z, 0.0) for h in range(n_heads)], axis=0)


def _block_diag_mask(rows, cols, rblk, cblk):
    r = lax.broadcasted_iota(jnp.int32, (rows, cols), 0) // rblk
    c = lax.broadcasted_iota(jnp.int32, (rows, cols), 1) // cblk
    return r == c


def _full_spec(shape, grid_rank):
    zeros = (0,) * len(shape)
    if grid_rank == 1:
        return pl.BlockSpec(shape, lambda i: zeros, pipeline_mode=pl.Buffered(1))
    return pl.BlockSpec(shape, lambda i, j: zeros, pipeline_mode=pl.Buffered(1))


def _params(semantics):
    return pltpu.CompilerParams(dimension_semantics=semantics, vmem_limit_bytes=VMEM_LIMIT)


ADA_TN = 1152


def _ada_body(cs_ref, cp_ref, w_ref, b_ref, os_ref, op_ref):
    w = w_ref[...].astype(BF16)
    os_ref[...] = _bdot(_silu(cs_ref[...]), w) + b_ref[...]
    op_ref[...] = _bdot(_silu(cp_ref[...]), w) + b_ref[...]


def _ada_mods(c_sample, c_prompt, ada_w, ada_b):
    ns, bp = c_sample.shape[0], c_prompt.shape[0]
    width = N_MODS * D_MODEL
    return pl.pallas_call(
        _ada_body,
        grid=(DEPTH, width // ADA_TN),
        in_specs=[
            pl.BlockSpec((ns, D_MODEL), lambda l, j: (0, 0)),
            pl.BlockSpec((bp, D_MODEL), lambda l, j: (0, 0)),
            pl.BlockSpec((None, D_MODEL, ADA_TN), lambda l, j: (l, 0, j)),
            pl.BlockSpec((None, 1, ADA_TN), lambda l, j: (l, 0, j)),
        ],
        out_specs=[pl.BlockSpec((None, ns, ADA_TN), lambda l, j: (l, 0, j)),
                   pl.BlockSpec((None, bp, ADA_TN), lambda l, j: (l, 0, j))],
        out_shape=[jax.ShapeDtypeStruct((DEPTH, ns, width), F32),
                   jax.ShapeDtypeStruct((DEPTH, bp, width), F32)],
        compiler_params=_params(("arbitrary", "arbitrary")),
        name="ada_mods",
    )(c_sample, c_prompt, ada_w, ada_b.reshape(DEPTH, 1, width))


FFN_TM = 512


def _ffn_tile(x, sh, sc, gt, wg_ref, wu_ref, wd_ref, g_ref, b_ref):
    h = (x * (1.0 + sc) + sh).astype(BF16)
    acc = None
    for lo, hi in FF_CHUNKS:
        g = _dot_nt(h, wg_ref[lo:hi, :])
        u = _dot_nt(h, wu_ref[lo:hi, :])
        y = _bdot(_silu(g) * u, wd_ref[lo:hi, :])
        acc = y if acc is None else acc + y
    r = FFN_RES * (1.0 + gt) * acc
    return _layer_norm(DN_ALPHA * x + r, g_ref[...], b_ref[...])


def _ffn_body(x_ref, sh_ref, sc_ref, gt_ref, xs_ref, shs_ref, scs_ref, gts_ref,
              wg_ref, wu_ref, wd_ref, g_ref, b_ref, o_ref, os_ref):
    last = pl.num_programs(0) - 1

    @pl.when(pl.program_id(0) < last)
    def _():
        o_ref[...] = _ffn_tile(x_ref[...], sh_ref[...], sc_ref[...], gt_ref[...],
                               wg_ref, wu_ref, wd_ref, g_ref, b_ref)

    @pl.when(pl.program_id(0) == last)
    def _():
        os_ref[...] = _ffn_tile(xs_ref[...], shs_ref[...], scs_ref[...], gts_ref[...],
                                wg_ref, wu_ref, wd_ref, g_ref, b_ref)


def _mod_specs_sample(layer, first, rows):
    return [
        pl.BlockSpec((None, rows, D_MODEL), functools.partial(
            lambda i, col: (layer, 0, col), col=first + k))
        for k in range(3)
    ]


def _ffn(x, xs, mods_p, mods_s, layer, first, wg, wu, wd, ln_g, ln_b):
    bsz, seq, _ = x.shape
    n = xs.shape[0]
    per_b = seq // FFN_TM
    n_tiles = bsz * per_b

    def tile_idx(t):
        tc = jnp.minimum(t, n_tiles - 1)
        return tc // per_b, tc % per_b

    mod_specs = [
        pl.BlockSpec((None, None, 1, D_MODEL), functools.partial(
            lambda t, col: (layer, tile_idx(t)[0], 0, col), col=first + k))
        for k in range(3)
    ]
    w_spec = pl.BlockSpec((None, D_FF, D_MODEL), lambda t: (layer, 0, 0), pipeline_mode=pl.Buffered(1))
    tile = pl.BlockSpec((None, FFN_TM, D_MODEL), lambda t: (*tile_idx(t), 0))
    return pl.pallas_call(
        _ffn_body,
        grid=(n_tiles + 1,),
        in_specs=[tile] + mod_specs + [_full_spec((n, D_MODEL), 1)] + _mod_specs_sample(layer, first, n)
        + [w_spec] * 3 + [_full_spec((1, D_MODEL), 1)] * 2,
        out_specs=[tile, _full_spec((n, D_MODEL), 1)],
        out_shape=[jax.ShapeDtypeStruct(x.shape, F32), jax.ShapeDtypeStruct(xs.shape, F32)],
        compiler_params=_params(("arbitrary",)),
        name="ffn",
    )(x, mods_p, mods_p, mods_p, xs, mods_s, mods_s, mods_s, wg, wu, wd, ln_g, ln_b)


def _col_groups(width):
    return [slice(lo, lo + width) for lo in range(0, D_MODEL, width)]


def _rwkv_mixes(h, prev, mu_ref):
    xx = prev - h
    mu = mu_ref[...]
    return [(h + xx * mu[i:i + 1, :]).astype(BF16) for i in range(6)]


def _rwkv_lora_in(mixes, w1, a1):
    return jnp.tanh(_dot(mixes[1], w1[...])).astype(BF16), _dot(mixes[4], a1[...]).astype(BF16)


def _rwkv_project_cols(cols, mixes, tw, ta, wr, wk, wv, w0, w2, a0, a2, kk_ref, ka_ref):
    r = _dot(mixes[0], wr[:, cols])
    k = _dot(mixes[2], wk[:, cols])
    v = _dot(mixes[3], wv[:, cols])
    wl = w0[:, cols] + _dot(tw, w2[:, cols])
    log_decay = -math.exp(-0.5) * _sigmoid(wl)
    a_sig = _sigmoid(a0[:, cols] + _dot(ta, a2[:, cols]))
    kk = k * kk_ref[:, cols]
    norm = jnp.sqrt(_seg_sum(kk * kk, RWKV_HEAD))
    kk = kk / jnp.maximum(norm, NORM_EPS)
    k2 = k * (1.0 + (a_sig - 1.0) * ka_ref[:, cols])
    return r, log_decay, k2, v, -kk, kk * a_sig


def _rwkv_bonus(r, k2, v, rk):
    return _seg_sum(r * k2 * rk, RWKV_HEAD) * v


def _rwkv_gate_in(xg, g1):
    return _sigmoid(_dot(xg, g1[...])).astype(BF16)


def _rwkv_out_cols(cols, y, bonus, sg, g2, wo, lng, lnb):
    gate = _dot(sg, g2[:, cols])
    inv_n = 1.0 / RWKV_HEAD
    mu_y = _seg_sum(y, RWKV_HEAD, split=True) * inv_n
    d = y - mu_y
    var = _seg_sum(d * d, RWKV_HEAD) * inv_n
    y = d * lax.rsqrt(var + RWKV_LNX_EPS) * lng[:, cols] + lnb[:, cols]
    return _bdot((y + bonus) * gate, wo[cols, :])


def _rwkv_weight_list(p):
    return [p["mu"], p["wr"], p["wk"], p["wv"], p["w0"], p["w1"], p["w2"], p["a0"], p["a1"], p["a2"],
            p["k_k"], p["k_a"], p["r_k"]]


def _rwkv_out_weight_list(p, ln_g, ln_b):
    return [p["g1"], p["g2"], p["wo"], p["lnx_g"], p["lnx_b"], ln_g, ln_b]


NEUMANN_LEVELS = int(math.log2(CHUNK)) - 1


def _wkv_chunks(problems):
    c = CHUNK
    hp = HEADS_PER_GROUP
    w = hp * RWKV_HEAD
    n = hp * c
    t_idx = lax.broadcasted_iota(jnp.int32, (c, w), 0)
    s_idx = lax.broadcasted_iota(jnp.int32, (c, w), 1) % c
    strict = t_idx > s_idx
    incl = t_idx >= s_idx
    eye = jnp.where(t_idx == s_idx, 1.0, 0.0)
    bd_mask = _block_diag_mask(w, w, RWKV_HEAD, RWKV_HEAD)
    idx = range(len(problems))
    At, Rt, Kt, Bt, V, S0, dec = zip(*problems)

    def bf(x):
        return x.astype(BF16)

    def stack(x, width):
        return bf(_head_stack(x, hp, width))

    X2 = [jnp.concatenate([At[i], Rt[i]], axis=0) for i in idx]
    KB = [jnp.concatenate([stack(Kt[i], RWKV_HEAD), stack(Bt[i], RWKV_HEAD)], axis=0) for i in idx]
    sc = [_dot_nt(X2[i], KB[i]) for i in idx]
    XS = [_dot_nt(X2[i], bf(S0[i])) for i in idx]
    Vbd = [stack(V[i], RWKV_HEAD) for i in idx]
    Aab = [jnp.where(strict, sc[i][:c, n:], 0.0) for i in idx]
    rhs = [XS[i][:c] + _dot(bf(jnp.where(strict, sc[i][:c, :n], 0.0)), Vbd[i]) for i in idx]

    Q = Aab
    P = [eye + Aab[i] for i in idx]
    Qbd = [stack(Q[i], c) for i in idx]
    for _ in range(NEUMANN_LEVELS):
        Q = [_dot(bf(Q[i]), Qbd[i]) for i in idx]
        Qbd = [stack(Q[i], c) for i in idx]
        P = [P[i] + _dot(bf(P[i]), Qbd[i]) for i in idx]
    U = [_dot(bf(P[i]), stack(rhs[i], RWKV_HEAD)) for i in idx]

    Y = [XS[i][c:] + _dot(bf(jnp.where(incl, sc[i][c:, :n], 0.0)), Vbd[i])
         + _dot(bf(jnp.where(incl, sc[i][c:, n:], 0.0)), stack(U[i], RWKV_HEAD)) for i in idx]

    upd = [_dot(bf(jnp.concatenate([V[i], U[i]], axis=0).T),
                bf(jnp.concatenate([Kt[i], Bt[i]], axis=0))) for i in idx]
    S1 = [(S0[i] + jnp.where(bd_mask, upd[i], 0.0)) * dec[i] for i in idx]
    return Y, S1


def _rwkv_prompt_body(x_ref, sh_ref, sc_ref, gt_ref,
                      mu, wr, wk, wv, w0, w1, w2, a0, a1, a2, kk_ref, ka_ref, rk_ref,
                      g1, g2, wo, lng, lnb, ln_g, ln_b,
                      o_ref, shift_ref, state_ref,
                      carry, s_at, s_rt, s_kt, s_bt, s_v, s_dec, s_bonus, s_sg):
    nb = x_ref.shape[0]
    rows = nb * CHUNK
    gw = HEADS_PER_GROUP * RWKV_HEAD
    step = pl.program_id(0)

    @pl.when(step == 0)
    def _():
        carry[...] = jnp.zeros(carry.shape, F32)
        state_ref[...] = jnp.zeros(state_ref.shape, F32)

    h3 = x_ref[...] * (1.0 + sc_ref[...]) + sh_ref[...]
    h = h3.reshape(rows, D_MODEL)
    first = lax.broadcasted_iota(jnp.int32, (rows, D_MODEL), 0) % CHUNK == 0
    carried = jnp.broadcast_to(carry[...], (nb, CHUNK, D_MODEL)).reshape(rows, D_MODEL)
    prev = jnp.where(first, carried, pltpu.roll(h, 1, axis=0))
    carry[...] = h3[:, CHUNK - 1:CHUNK, :]
    shift_ref[...] = h3[:, CHUNK - 1:CHUNK, :]

    mixes = _rwkv_mixes(h, prev, mu)
    tw, ta = _rwkv_lora_in(mixes, w1, a1)
    s_sg[...] = _rwkv_gate_in(mixes[5], g1)
    groups = _col_groups(gw)

    def project(gi):
        cols = groups[gi]
        r, lw, k2, v, av, bv = _rwkv_project_cols(
            cols, mixes, tw, ta, wr, wk, wv, w0, w2, a0, a2, kk_ref, ka_ref)
        s_bonus[:, cols] = _rwkv_bonus(r, k2, v, rk_ref[:, cols])
        g = _chunk_cumsum(lw, nb)
        e_neg = jnp.exp(-g)
        s_at[gi] = (av * jnp.exp(g - lw)).astype(BF16)
        s_rt[gi] = (r * jnp.exp(g)).astype(BF16)
        s_kt[gi] = (k2 * e_neg).astype(BF16)
        s_bt[gi] = (bv * e_neg).astype(BF16)
        s_v[gi] = v.astype(BF16)
        s_dec[gi] = jnp.exp(g.reshape(nb, CHUNK, gw)[:, CHUNK - 1:CHUNK, :])

    def scan(gi):
        sls = [slice(b * CHUNK, (b + 1) * CHUNK) for b in range(nb)]
        Y, S1 = _wkv_chunks([
            (s_at[gi, sl, :], s_rt[gi, sl, :], s_kt[gi, sl, :].astype(F32), s_bt[gi, sl, :].astype(F32),
             s_v[gi, sl, :].astype(F32), state_ref[b, gi], s_dec[gi, b])
            for b, sl in enumerate(sls)])
        for b in range(nb):
            state_ref[b, gi] = S1[b]
        return jnp.concatenate(Y, axis=0)

    project(0)
    out = None
    for gi in range(N_GROUPS):
        y = scan(gi)
        if gi + 1 < N_GROUPS:
            project(gi + 1)
        part = _rwkv_out_cols(groups[gi], y, s_bonus[:, groups[gi]], s_sg[...], g2, wo, lng, lnb)
        out = part if out is None else out + part
    gt = jnp.broadcast_to(gt_ref[...], (nb, CHUNK, D_MODEL)).reshape(rows, D_MODEL)
    x = x_ref[...].reshape(rows, D_MODEL)
    o_ref[...] = _layer_norm(DN_ALPHA * x + (1.0 + gt) * out,
                             ln_g[...], ln_b[...]).reshape(nb, CHUNK, D_MODEL)


def _rwkv_prompt(x, mods_p, layer, p, ln_g, ln_b):
    bsz, seq, _ = x.shape
    rows = bsz * CHUNK
    gw = HEADS_PER_GROUP * RWKV_HEAD
    weights = _rwkv_weight_list(p) + _rwkv_out_weight_list(p, ln_g, ln_b)
    mod_specs = [
        pl.BlockSpec((None, bsz, 1, D_MODEL), functools.partial(lambda i, col: (layer, 0, 0, col), col=3 + k))
        for k in range(3)
    ]
    slab16 = pltpu.VMEM((N_GROUPS, rows, gw), BF16)
    out, shift, state = pl.pallas_call(
        _rwkv_prompt_body,
        grid=(seq // CHUNK,),
        in_specs=[pl.BlockSpec((bsz, CHUNK, D_MODEL), lambda i: (0, i, 0))] + mod_specs
        + [_full_spec(w.shape, 1) for w in weights],
        out_specs=[
            pl.BlockSpec((bsz, CHUNK, D_MODEL), lambda i: (0, i, 0)),
            _full_spec((bsz, 1, D_MODEL), 1),
            _full_spec((bsz, N_GROUPS, gw, gw), 1),
        ],
        out_shape=[
            jax.ShapeDtypeStruct(x.shape, F32),
            jax.ShapeDtypeStruct((bsz, 1, D_MODEL), F32),
            jax.ShapeDtypeStruct((bsz, N_GROUPS, gw, gw), F32),
        ],
        scratch_shapes=[pltpu.VMEM((bsz, 1, D_MODEL), F32)] + [slab16] * 5
        + [pltpu.VMEM((N_GROUPS, bsz, 1, gw), F32),
           pltpu.VMEM((rows, D_MODEL), F32), pltpu.VMEM((rows, GATE_LORA_PAD), BF16)],
        compiler_params=_params(("arbitrary",)),
        name="rwkv_prompt",
    )(x, mods_p, mods_p, mods_p, *weights)
    s5d = state.reshape(bsz, N_GROUPS, HEADS_PER_GROUP, RWKV_HEAD, gw)
    wkv = jnp.stack([s5d[:, :, hh, :, hh * RWKV_HEAD:(hh + 1) * RWKV_HEAD]
                     for hh in range(HEADS_PER_GROUP)], axis=2)
    return out, shift.reshape(bsz, D_MODEL), wkv.reshape(bsz, RWKV_HEADS, RWKV_HEAD, RWKV_HEAD)


def _rwkv_sample_pre_body(x_ref, sh_ref, sc_ref, prev_ref,
                          mu, wr, wk, wv, w0, w1, w2, a0, a1, a2, kk_ref, ka_ref, rk_ref,
                          h_ref, rt_ref, wt_ref, kt_ref, vt_ref, at_ref, bt_ref, xg_ref, bonus_ref):
    h = x_ref[...] * (1.0 + sc_ref[...]) + sh_ref[...]
    h_ref[...] = h
    mixes = _rwkv_mixes(h, prev_ref[...], mu)
    xg_ref[...] = mixes[5]
    tw, ta = _rwkv_lora_in(mixes, w1, a1)
    for cols in _col_groups(MXU_DIM):
        r, lw, k2, v, av, bv = _rwkv_project_cols(
            cols, mixes, tw, ta, wr, wk, wv, w0, w2, a0, a2, kk_ref, ka_ref)
        bonus_ref[:, cols] = _rwkv_bonus(r, k2, v, rk_ref[:, cols])
        rt_ref[cols, :] = r.T
        wt_ref[cols, :] = jnp.exp(lw).T
        kt_ref[cols, :] = k2.T
        vt_ref[cols, :] = v.T
        at_ref[cols, :] = av.T
        bt_ref[cols, :] = bv.T


def _rwkv_sample_step_body(s_ref, r_ref, w_ref, k_ref, v_ref, a_ref, b_ref, so_ref, y_ref):
    a, w, b, k, r = a_ref[...], w_ref[...], b_ref[...], k_ref[...], r_ref[...]

    def row_step(i, carry_val):
        S = s_ref[i]
        sa = jnp.sum(S * a, axis=0, keepdims=True)
        S1 = S * w + sa * b + v_ref[pl.ds(i, 1), :] * k
        so_ref[i] = S1
        y_ref[pl.ds(i, 1), :] = jnp.sum(S1 * r, axis=0, keepdims=True)
        return carry_val

    lax.fori_loop(0, RWKV_HEAD, row_step, 0, unroll=2)


def _rwkv_sample_post_body(yt_ref, bonus_ref, xg_ref, x_ref, gt_ref,
                           g1, g2, wo, lng, lnb, ln_g, ln_b, o_ref):
    sg = _rwkv_gate_in(xg_ref[...], g1)
    out = None
    for cols in _col_groups(MXU_DIM):
        part = _rwkv_out_cols(cols, yt_ref[cols, :].T, bonus_ref[:, cols], sg, g2, wo, lng, lnb)
        out = part if out is None else out + part
    o_ref[...] = _layer_norm(DN_ALPHA * x_ref[...] + (1.0 + gt_ref[...]) * out, ln_g[...], ln_b[...])


def _to_columns(a):
    n, f = a.shape
    return a.reshape(n // SAMPLE_BLOCK, SAMPLE_BLOCK, f).transpose(0, 2, 1)


def _rwkv_sample(x, mods_s, layer, s_shift, s_wkv, p, ln_g, ln_b):
    n = x.shape[0]
    row = jax.ShapeDtypeStruct((n, D_MODEL), F32)
    row_spec = _full_spec((n, D_MODEL), 1)
    pre_w = _rwkv_weight_list(p)
    ms = _mod_specs_sample(layer, 3, n)
    col = jax.ShapeDtypeStruct((D_MODEL, n), F32)
    col_spec = _full_spec((D_MODEL, n), 1)
    h, *vecs, xg, bonus = pl.pallas_call(
        _rwkv_sample_pre_body,
        grid=(1,),
        in_specs=[row_spec, ms[0], ms[1], row_spec] + [_full_spec(a.shape, 1) for a in pre_w],
        out_specs=[row_spec] + [col_spec] * 6 + [row_spec] * 2,
        out_shape=[row] + [col] * 6 + [jax.ShapeDtypeStruct((n, D_MODEL), BF16), row],
        compiler_params=_params(("arbitrary",)),
        name="rwkv_sample_pre",
    )(x, mods_s, mods_s, s_shift, *pre_w)

    st_spec = pl.BlockSpec((None, RWKV_HEAD, RWKV_HEAD, n), lambda hd: (hd, 0, 0, 0))
    vec_spec = pl.BlockSpec((RWKV_HEAD, n), lambda hd: (hd, 0))
    st_new, yt = pl.pallas_call(
        _rwkv_sample_step_body,
        grid=(RWKV_HEADS,),
        in_specs=[st_spec] + [vec_spec] * len(vecs),
        out_specs=[st_spec, vec_spec],
        out_shape=[jax.ShapeDtypeStruct((RWKV_HEADS, RWKV_HEAD, RWKV_HEAD, n), F32), col],
        compiler_params=_params(("arbitrary",)),
        name="rwkv_sample_step",
    )(jnp.transpose(s_wkv, (1, 2, 3, 0)), *vecs)
    s_new = jnp.transpose(st_new, (3, 0, 1, 2))

    post_w = _rwkv_out_weight_list(p, ln_g, ln_b)
    out = pl.pallas_call(
        _rwkv_sample_post_body,
        grid=(1,),
        in_specs=[col_spec] + [row_spec] * 3 + [ms[2]] + [_full_spec(a.shape, 1) for a in post_w],
        out_specs=row_spec,
        out_shape=row,
        compiler_params=_params(("arbitrary",)),
        name="rwkv_sample_post",
    )(yt, bonus, xg, x, mods_s, *post_w)
    return out, h, s_new


S5_TILE_GROUPS = LANES // S5_GROUP
S5_TILES = S5_GROUPS // S5_TILE_GROUPS


def _s5_prep_body(are_ref, aim_ref, ls_ref, bre_ref, bim_ref, cre_ref, cim_ref,
                  abre_ref, abim_ref, bbre_ref, bbim_ref, cdre_ref, cdim_ref):
    a_re = are_ref[...]
    a_im = aim_ref[...]
    dt = jnp.exp(ls_ref[...])
    mag = jnp.exp(a_re * dt)
    ab_re = mag * jnp.cos(a_im * dt)
    ab_im = mag * jnp.sin(a_im * dt)
    den = a_re * a_re + a_im * a_im
    nr = ab_re - 1.0
    z_re = (nr * a_re + ab_im * a_im) / den
    z_im = (ab_im * a_re - nr * a_im) / den
    b_re = bre_ref[...]
    b_im = bim_ref[...]
    abre_ref[...] = ab_re
    abim_ref[...] = ab_im
    bb_re = (z_re * b_re - z_im * b_im).astype(BF16)
    bb_im = (z_re * b_im + z_im * b_re).astype(BF16)
    for ref in (bbre_ref, bbim_ref, cdre_ref, cdim_ref):
        ref[...] = jnp.zeros(ref.shape, BF16)
    c, p = S5_GROUP, S5_STATE
    for g in range(S5_GROUPS):
        k, j = divmod(g, S5_TILE_GROUPS)
        bbre_ref[k, j * c:(j + 1) * c, j * p:(j + 1) * p] = bb_re[g]
        bbim_ref[k, j * c:(j + 1) * c, j * p:(j + 1) * p] = bb_im[g]
        cdre_ref[k, j * p:(j + 1) * p, j * c:(j + 1) * c] = cre_ref[g].astype(BF16)
        cdim_ref[k, j * p:(j + 1) * p, j * c:(j + 1) * c] = cim_ref[g].astype(BF16)


def _s5_prepare(s5_a_re, s5_a_im, s5_log_step, s5_b_re, s5_b_im, s5_c_re, s5_c_im, s5_d):
    G, P, C = S5_GROUPS, S5_STATE, S5_GROUP
    small = jax.ShapeDtypeStruct((G, 1, P), F32)
    bb_shape = (S5_TILES, LANES, S5_TILE_GROUPS * P)
    cd_shape = (S5_TILES, S5_TILE_GROUPS * P, LANES)
    ls = jnp.broadcast_to(s5_log_step.reshape(G, 1, 1), (G, 1, P))
    ab_re, ab_im, bb_re, bb_im, cd_re, cd_im = pl.pallas_call(
        _s5_prep_body,
        grid=(1,),
        in_specs=[_full_spec((G, 1, P), 1)] * 3 + [_full_spec((G, C, P), 1)] * 2
        + [_full_spec((G, P, C), 1)] * 2,
        out_specs=[_full_spec((G, 1, P), 1)] * 2 + [_full_spec(bb_shape, 1)] * 2
        + [_full_spec(cd_shape, 1)] * 2,
        out_shape=[small, small] + [jax.ShapeDtypeStruct(bb_shape, BF16)] * 2
        + [jax.ShapeDtypeStruct(cd_shape, BF16)] * 2,
        name="s5_prepare",
    )(s5_a_re.reshape(G, 1, P), s5_a_im.reshape(G, 1, P), ls,
      s5_b_re.transpose(0, 2, 1), s5_b_im.transpose(0, 2, 1),
      s5_c_re.transpose(0, 2, 1), s5_c_im.transpose(0, 2, 1))
    return dict(
        ab_re=ab_re.reshape(1, G * P), ab_im=ab_im.reshape(1, G * P),
        bb_re=bb_re, bb_im=bb_im, c_re=cd_re, c_im=cd_im, d=s5_d.reshape(1, G * C),
    )


def _s5_input(u, bb_re_ref, bb_im_ref):
    n_tiles = bb_re_ref.shape[0]
    ub = u.astype(BF16)
    re = [_dot(ub[:, k * LANES:(k + 1) * LANES], bb_re_ref[k]) for k in range(n_tiles)]
    im = [_dot(ub[:, k * LANES:(k + 1) * LANES], bb_im_ref[k]) for k in range(n_tiles)]
    return jnp.concatenate(re, axis=1), jnp.concatenate(im, axis=1)


def _s5_output(h_re, h_im, c_re_ref, c_im_ref):
    n_tiles = c_re_ref.shape[0]
    w = h_re.shape[1] // n_tiles
    hr = h_re.astype(BF16)
    hi = h_im.astype(BF16)
    ys = [_dot(hr[:, k * w:(k + 1) * w], c_re_ref[k]) - _dot(hi[:, k * w:(k + 1) * w], c_im_ref[k])
          for k in range(n_tiles)]
    return jnp.concatenate(ys, axis=1)


C_Q, C_K, C_V, C_G, C_U, C_GK, C_END = 0, 256, 512, 1024, 1536, 2048, 2176


def _mix_project(h, w_in, w_gk, b_gk):
    p = _bdot(h, w_in[...])
    q = p[:, C_Q:C_K] * (GLA_DK ** -0.5)
    k = p[:, C_K:C_V]
    v = p[:, C_V:C_G]
    gg = p[:, C_G:C_U]
    u = p[:, C_U:C_GK]
    z = _bdot(p[:, C_GK:C_END], w_gk[...]) + b_gk[...]
    gk = -_softplus(-z) * (1.0 / GLA_GATE_NORM)
    return q, k, v, gg, u, gk


def _gla_finish(o, gg, norm_g):
    parts = []
    for hh in range(GLA_HEADS):
        oh = o[:, hh * GLA_DV:(hh + 1) * GLA_DV]
        parts.append(oh * lax.rsqrt(jnp.mean(oh * oh, axis=-1, keepdims=True) + RMS_EPS) * norm_g[...])
    return jnp.concatenate(parts, axis=1) * _silu(gg)


def _mix_output(o_gla, y_s5, x, gt, w_glu, b_glu, w_out, ln_g, ln_b):
    z = _gelu_tanh(y_s5)
    o_s5 = z * _sigmoid(_bdot(z, w_glu[...]) + b_glu[...])
    out = _bdot(o_gla, w_out[0:GLA_VAL, :]) + _bdot(o_s5, w_out[GLA_VAL:GLA_VAL + S5_WIDTH, :])
    return _layer_norm(DN_ALPHA * x + (1.0 + gt) * out, ln_g[...], ln_b[...])


def _gla_chunks(problems):
    c = CHUNK
    mid = c // 2 - 1
    idx = range(len(problems))
    q, k, v, bc, ST = zip(*problems)
    t_idx = lax.broadcasted_iota(jnp.int32, (c, GLA_KEY), 0)
    s_idx = lax.broadcasted_iota(jnp.int32, (c, GLA_KEY), 1) % c
    causal = t_idx >= s_idx
    bd_mask = _block_diag_mask(GLA_VAL, GLA_KEY, GLA_DV, GLA_DK)

    def bf(x):
        return x.astype(BF16)

    b_mid = [bc[i][mid:mid + 1, :] for i in idx]
    b_last = [bc[i][c - 1:c, :] for i in idx]
    q_in = [bf(q[i] * jnp.exp(bc[i] - b_mid[i])) for i in idx]
    k_in = [bf(_head_stack(k[i] * jnp.exp(b_mid[i] - bc[i]), GLA_HEADS, GLA_DK)) for i in idx]
    scores = [jnp.where(causal, _dot_nt(q_in[i], k_in[i]), 0.0) for i in idx]
    o_inter = [_dot_nt(bf(q[i] * jnp.exp(bc[i])), bf(ST[i])) for i in idx]
    upd = [_dot(bf(v[i].T), bf(k[i] * jnp.exp(b_last[i] - bc[i]))) for i in idx]
    o = [o_inter[i] + _dot(bf(scores[i]), bf(_head_stack(v[i], GLA_HEADS, GLA_DV))) for i in idx]
    ST1 = [ST[i] * jnp.exp(b_last[i]) + jnp.where(bd_mask, upd[i], 0.0) for i in idx]
    return o, ST1


def _mix_prompt_body(x_ref, sh_ref, sc_ref, gt_ref,
                     w_in, w_gk, b_gk, norm_g, ab_re_ref, ab_im_ref, bb_re, bb_im, c_re, c_im, d_ref,
                     w_glu, b_glu, w_out, ln_g, ln_b,
                     o_ref, gla_ref, hre_ref, him_ref,
                     s_q, s_k, s_v, s_bc, s_perm, s_re, s_im):
    nb = x_ref.shape[0]
    rows = nb * CHUNK
    step = pl.program_id(0)

    @pl.when(step == 0)
    def _():
        gla_ref[...] = jnp.zeros(gla_ref.shape, F32)
        hre_ref[...] = jnp.zeros(hre_ref.shape, F32)
        him_ref[...] = jnp.zeros(him_ref.shape, F32)

    x3 = x_ref[...]
    h3 = x3 * (1.0 + sc_ref[...]) + sh_ref[...]
    x = x3.reshape(rows, D_MODEL)
    h = h3.reshape(rows, D_MODEL)
    q, k, v, gg, u, gk = _mix_project(h, w_in, w_gk, b_gk)

    s_q[...] = q
    s_k[...] = k
    s_v[...] = v
    s_bc[...] = _chunk_cumsum(gk, nb)

    sls = [slice(b * CHUNK, (b + 1) * CHUNK) for b in range(nb)]
    o, ST1 = _gla_chunks([(s_q[sl, :], s_k[sl, :], s_v[sl, :], s_bc[sl, :], gla_ref[b])
                          for b, sl in enumerate(sls)])
    for b in range(nb):
        gla_ref[b] = ST1[b]
    o_gla = _gla_finish(jnp.concatenate(o, axis=0), gg, norm_g)

    n_lt = S5_WIDTH // LANES
    for b in range(nb):
        for lt in range(n_lt):
            s_perm[lt, b * PERM_STRIDE:b * PERM_STRIDE + CHUNK, :] = (
                u[b * CHUNK:(b + 1) * CHUNK, lt * LANES:(lt + 1) * LANES])
    u_tb = jnp.concatenate(
        [jnp.concatenate([s_perm[lt, pl.ds(t, nb, stride=PERM_STRIDE), :] for lt in range(n_lt)], axis=1)
         for t in range(CHUNK)], axis=0)
    bu_re, bu_im = _s5_input(u_tb, bb_re, bb_im)
    s_re[...] = bu_re
    s_im[...] = bu_im

    n_ch = 4
    cw = S5_CH // n_ch
    for ci in range(n_ch):
        lanes = slice(ci * cw, (ci + 1) * cw)
        a_re = jnp.broadcast_to(ab_re_ref[:, lanes], (nb, cw))
        a_im = jnp.broadcast_to(ab_im_ref[:, lanes], (nb, cw))

        h_re, h_im = hre_ref[:, lanes], him_ref[:, lanes]
        for t in range(CHUNK):
            sl = slice(t * nb, (t + 1) * nb)
            h_re, h_im = (a_re * h_re - a_im * h_im + s_re[sl, lanes],
                          a_re * h_im + a_im * h_re + s_im[sl, lanes])
            s_re[sl, lanes] = h_re
            s_im[sl, lanes] = h_im
        hre_ref[:, lanes] = h_re
        him_ref[:, lanes] = h_im

    y_tb = _s5_output(s_re[...], s_im[...], c_re, c_im)
    for t in range(CHUNK):
        for lt in range(n_lt):
            s_perm[lt, pl.ds(t, nb, stride=PERM_STRIDE), :] = (
                y_tb[t * nb:(t + 1) * nb, lt * LANES:(lt + 1) * LANES])
    y_s5 = jnp.concatenate(
        [jnp.concatenate([s_perm[lt, b * PERM_STRIDE:b * PERM_STRIDE + CHUNK, :] for lt in range(n_lt)],
                         axis=1) for b in range(nb)], axis=0)
    y_s5 = y_s5 + d_ref[...] * u

    gt = jnp.broadcast_to(gt_ref[...], (nb, CHUNK, D_MODEL)).reshape(rows, D_MODEL)
    out = _mix_output(o_gla, y_s5, x, gt, w_glu, b_glu, w_out, ln_g, ln_b)
    o_ref[...] = out.reshape(nb, CHUNK, D_MODEL)


def _mix_weight_list(m, s5):
    return [m["w_in"], m["w_gk"], m["b_gk"], m["norm_g"], s5["ab_re"], s5["ab_im"], s5["bb_re"], s5["bb_im"],
            s5["c_re"], s5["c_im"], s5["d"], m["w_glu"], m["b_glu"], m["w_out"]]


def _mix_prompt(x, mods_p, layer, m, s5, ln_g, ln_b):
    bsz, seq, _ = x.shape
    rows = bsz * CHUNK
    weights = _mix_weight_list(m, s5) + [ln_g, ln_b]
    mod_specs = [
        pl.BlockSpec((None, bsz, 1, D_MODEL), functools.partial(lambda i, col: (layer, 0, 0, col), col=3 + k))
        for k in range(3)
    ]
    out, gla_t, h_re, h_im = pl.pallas_call(
        _mix_prompt_body,
        grid=(seq // CHUNK,),
        in_specs=[pl.BlockSpec((bsz, CHUNK, D_MODEL), lambda i: (0, i, 0))] + mod_specs
        + [_full_spec(w.shape, 1) for w in weights],
        out_specs=[
            pl.BlockSpec((bsz, CHUNK, D_MODEL), lambda i: (0, i, 0)),
            _full_spec((bsz, GLA_VAL, GLA_KEY), 1),
            _full_spec((bsz, S5_CH), 1),
            _full_spec((bsz, S5_CH), 1),
        ],
        out_shape=[
            jax.ShapeDtypeStruct(x.shape, F32),
            jax.ShapeDtypeStruct((bsz, GLA_VAL, GLA_KEY), F32),
            jax.ShapeDtypeStruct((bsz, S5_CH), F32),
            jax.ShapeDtypeStruct((bsz, S5_CH), F32),
        ],
        scratch_shapes=[
            pltpu.VMEM((rows, GLA_KEY), F32), pltpu.VMEM((rows, GLA_KEY), F32),
            pltpu.VMEM((rows, GLA_VAL), F32), pltpu.VMEM((rows, GLA_KEY), F32),
            pltpu.VMEM((S5_WIDTH // LANES, bsz * PERM_STRIDE, LANES), F32),
            pltpu.VMEM((rows, S5_CH), F32), pltpu.VMEM((rows, S5_CH), F32),
        ],
        compiler_params=_params(("arbitrary",)),
        name="gla_s5_prompt",
    )(x, mods_p, mods_p, mods_p, *weights)
    s4 = gla_t.reshape(bsz, GLA_HEADS, GLA_DV, GLA_KEY)
    gla = jnp.stack([s4[:, hh, :, hh * GLA_DK:(hh + 1) * GLA_DK] for hh in range(GLA_HEADS)],
                    axis=1).transpose(0, 1, 3, 2)
    return (out, gla, h_re.reshape(bsz, S5_GROUPS, S5_STATE), h_im.reshape(bsz, S5_GROUPS, S5_STATE))


def _mix_sample_pre_body(x_ref, sh_ref, sc_ref, hre_ref, him_ref,
                         w_in, w_gk, b_gk, ab_re_ref, ab_im_ref, bb_re, bb_im, c_re, c_im, d_ref,
                         q_ref, k_ref, v_ref, gg_ref, dec_ref, y_ref, nre_ref, nim_ref):
    h = x_ref[...] * (1.0 + sc_ref[...]) + sh_ref[...]
    q, k, v, gg, u, gk = _mix_project(h, w_in, w_gk, b_gk)
    q_ref[...] = q
    k_ref[...] = k
    v_ref[...] = v
    gg_ref[...] = gg
    dec_ref[...] = jnp.exp(gk)
    bu_re, bu_im = _s5_input(u, bb_re, bb_im)
    a_re = ab_re_ref[...]
    a_im = ab_im_ref[...]
    h_re = hre_ref[...]
    h_im = him_ref[...]
    n_re = a_re * h_re - a_im * h_im + bu_re
    n_im = a_re * h_im + a_im * h_re + bu_im
    nre_ref[...] = n_re
    nim_ref[...] = n_im
    y_ref[...] = _s5_output(n_re, n_im, c_re, c_im) + d_ref[...] * u


def _gla_sample_step_body(s_ref, qc_ref, kc_ref, dc_ref, v_ref, so_ref, o_ref):
    for j in range(SAMPLE_BLOCK):
        for hh in range(GLA_HEADS):
            rows = slice(hh * GLA_DK, (hh + 1) * GLA_DK)
            lanes = slice(hh * GLA_DV, (hh + 1) * GLA_DV)
            S1 = dc_ref[rows, j:j + 1] * s_ref[j, hh] + kc_ref[rows, j:j + 1] * v_ref[j:j + 1, lanes]
            so_ref[j, hh] = S1
            o_ref[j:j + 1, lanes] = jnp.sum(qc_ref[rows, j:j + 1] * S1, axis=0, keepdims=True)


def _mix_sample_post_body(o_ref_in, gg_ref, y_ref, x_ref, gt_ref, norm_g, w_glu, b_glu, w_out, ln_g, ln_b,
                          o_ref):
    o_gla = _gla_finish(o_ref_in[...], gg_ref[...], norm_g)
    o_ref[...] = _mix_output(o_gla, y_ref[...], x_ref[...], gt_ref[...], w_glu, b_glu, w_out, ln_g, ln_b)


def _mix_sample(x, mods_s, layer, s_gla, s_re, s_im, m, s5, ln_g, ln_b):
    n = x.shape[0]
    ms = _mod_specs_sample(layer, 3, n)
    row_spec = _full_spec((n, D_MODEL), 1)

    def sds(w):
        return jax.ShapeDtypeStruct((n, w), F32)

    def fs(w):
        return _full_spec((n, w), 1)

    pre_w = [m["w_in"], m["w_gk"], m["b_gk"], s5["ab_re"], s5["ab_im"], s5["bb_re"], s5["bb_im"],
             s5["c_re"], s5["c_im"], s5["d"]]
    widths = [GLA_KEY, GLA_KEY, GLA_VAL, GLA_VAL, GLA_KEY, S5_WIDTH, S5_CH, S5_CH]
    q, k, v, gg, dec, y_s5, n_re, n_im = pl.pallas_call(
        _mix_sample_pre_body,
        grid=(1,),
        in_specs=[row_spec, ms[0], ms[1], fs(S5_CH), fs(S5_CH)] + [_full_spec(a.shape, 1) for a in pre_w],
        out_specs=[fs(w) for w in widths],
        out_shape=[sds(w) for w in widths],
        compiler_params=_params(("arbitrary",)),
        name="gla_s5_sample_pre",
    )(x, mods_s, mods_s, s_re.reshape(n, S5_CH), s_im.reshape(n, S5_CH), *pre_w)

    nblk = n // SAMPLE_BLOCK
    st_spec = pl.BlockSpec((SAMPLE_BLOCK, GLA_HEADS, GLA_DK, GLA_DV), lambda i: (i, 0, 0, 0))
    col_spec = pl.BlockSpec((None, GLA_KEY, SAMPLE_BLOCK), lambda i: (i, 0, 0))
    v_spec = pl.BlockSpec((SAMPLE_BLOCK, GLA_VAL), lambda i: (i, 0))
    s_new, o = pl.pallas_call(
        _gla_sample_step_body,
        grid=(nblk,),
        in_specs=[st_spec, col_spec, col_spec, col_spec, v_spec],
        out_specs=[st_spec, v_spec],
        out_shape=[jax.ShapeDtypeStruct(s_gla.shape, F32), sds(GLA_VAL)],
        compiler_params=_params(("arbitrary",)),
        name="gla_sample_step",
    )(s_gla, _to_columns(q), _to_columns(k), _to_columns(dec), v)

    post_w = [m["norm_g"], m["w_glu"], m["b_glu"], m["w_out"], ln_g, ln_b]
    out = pl.pallas_call(
        _mix_sample_post_body,
        grid=(1,),
        in_specs=[fs(GLA_VAL), fs(GLA_VAL), fs(S5_WIDTH), row_spec, ms[2]]
        + [_full_spec(a.shape, 1) for a in post_w],
        out_specs=row_spec,
        out_shape=sds(D_MODEL),
        compiler_params=_params(("arbitrary",)),
        name="gla_s5_sample_post",
    )(o, gg, y_s5, x, mods_s, *post_w)
    return (out, s_new, n_re.reshape(n, S5_GROUPS, S5_STATE), n_im.reshape(n, S5_GROUPS, S5_STATE))


def _pad_to(a, axis, size):
    pad = [(0, 0)] * a.ndim
    pad[axis] = (0, size - a.shape[axis])
    return jnp.pad(a, pad)


def _row(a):
    return a.reshape(1, -1).astype(F32)


def _prep_ffn(wg, wu, wd):
    return (jnp.transpose(wg, (0, 2, 1)).astype(BF16), jnp.transpose(wu, (0, 2, 1)).astype(BF16),
            wd.astype(BF16))


def _prep_mix(w_in, w_out, gla_w_gk, gla_b_gk, gla_norm_g, s5_w_glu, s5_b_glu):
    q, k, v, g, gk_low, u = jnp.split(
        w_in, (GLA_KEY, 2 * GLA_KEY, 2 * GLA_KEY + GLA_VAL, 2 * GLA_KEY + 2 * GLA_VAL,
               2 * GLA_KEY + 2 * GLA_VAL + GLA_GATE_RANK), axis=1)
    w_in_r = jnp.concatenate([q, k, v, g, u, _pad_to(gk_low, 1, LANES)], axis=1)
    return dict(
        w_in=w_in_r.astype(BF16), w_out=w_out.astype(BF16),
        w_gk=_pad_to(gla_w_gk, 0, LANES).astype(BF16), b_gk=_row(gla_b_gk), norm_g=_row(gla_norm_g),
        w_glu=s5_w_glu.astype(BF16), b_glu=_row(s5_b_glu),
    )


def _prep_rwkv(mu, w_r, w_k, w_v, w_o, w0, w1, w2, a0, a1, a2, g1, g2, k_k, k_a, r_k, lnx_g, lnx_b):
    return dict(
        mu=_pad_to(mu, 0, SUBLANES), wr=w_r.astype(BF16), wk=w_k.astype(BF16), wv=w_v.astype(BF16),
        wo=w_o.astype(BF16), w0=_row(w0),
        w1=_pad_to(w1, 1, LORA_PAD).astype(BF16), w2=_pad_to(w2, 0, LORA_PAD).astype(BF16),
        a0=_row(a0),
        a1=_pad_to(a1, 1, LORA_PAD).astype(BF16), a2=_pad_to(a2, 0, LORA_PAD).astype(BF16),
        g1=_pad_to(g1, 1, GATE_LORA_PAD).astype(BF16), g2=_pad_to(g2, 0, GATE_LORA_PAD).astype(BF16),
        k_k=_row(k_k), k_a=_row(k_a), r_k=_row(r_k), lnx_g=_row(lnx_g), lnx_b=_row(lnx_b),
    )


def kernel(x_prompt, x_sample, state_gla, state_s5_re, state_s5_im, state_rwkv_shift, state_rwkv_wkv,
           c_prompt, c_sample, ada_w, ada_b, ln_g, ln_b,
           ffn1_wg, ffn1_wu, ffn1_wd, ffn2_wg, ffn2_wu, ffn2_wd,
           w_in, w_out, gla_w_gk, gla_b_gk, gla_norm_g,
           s5_a_re, s5_a_im, s5_log_step, s5_b_re, s5_b_im, s5_c_re, s5_c_im, s5_d, s5_w_glu, s5_b_glu,
           rwkv_mu, rwkv_w_r, rwkv_w_k, rwkv_w_v, rwkv_w_o, rwkv_w0, rwkv_w1, rwkv_w2,
           rwkv_a0, rwkv_a1, rwkv_a2, rwkv_g1, rwkv_g2, rwkv_k_k, rwkv_k_a, rwkv_r_k,
           rwkv_lnx_g, rwkv_lnx_b):
    bp = x_prompt.shape[0]
    ns = x_sample.shape[0]

    mods_s, mods_p = _ada_mods(c_sample, c_prompt, ada_w, ada_b)
    mods_p = mods_p.reshape(DEPTH, bp, 1, N_MODS * D_MODEL)

    ffn1 = _prep_ffn(ffn1_wg, ffn1_wu, ffn1_wd)
    ffn2 = _prep_ffn(ffn2_wg, ffn2_wu, ffn2_wd)
    mix = _prep_mix(w_in, w_out, gla_w_gk, gla_b_gk, gla_norm_g, s5_w_glu, s5_b_glu)
    s5 = _s5_prepare(s5_a_re, s5_a_im, s5_log_step, s5_b_re, s5_b_im, s5_c_re, s5_c_im, s5_d)
    rwkv = _prep_rwkv(rwkv_mu, rwkv_w_r, rwkv_w_k, rwkv_w_v, rwkv_w_o, rwkv_w0, rwkv_w1, rwkv_w2,
                      rwkv_a0, rwkv_a1, rwkv_a2, rwkv_g1, rwkv_g2, rwkv_k_k, rwkv_k_a, rwkv_r_k,
                      rwkv_lnx_g, rwkv_lnx_b)

    def lnp(layer, idx):
        return _row(ln_g[layer, idx]), _row(ln_b[layer, idx])

    x = x_prompt
    xs = x_sample.reshape(ns, D_MODEL)
    x, xs = _ffn(x, xs, mods_p, mods_s, 0, 0, *ffn1, *lnp(0, 0))
    x, gla_p, s5_re_p, s5_im_p = _mix_prompt(x, mods_p, 0, mix, s5, *lnp(0, 1))
    xs, gla_s, s5_re_s, s5_im_s = _mix_sample(xs, mods_s, 0, state_gla, state_s5_re, state_s5_im,
                                              mix, s5, *lnp(0, 1))
    x, xs = _ffn(x, xs, mods_p, mods_s, 0, 6, *ffn2, *lnp(0, 2))
    x, xs = _ffn(x, xs, mods_p, mods_s, 1, 0, *ffn1, *lnp(1, 0))
    x, shift_p, wkv_p = _rwkv_prompt(x, mods_p, 1, rwkv, *lnp(1, 1))
    xs, shift_s, wkv_s = _rwkv_sample(xs, mods_s, 1, state_rwkv_shift, state_rwkv_wkv, rwkv, *lnp(1, 1))
    y_prompt, xs = _ffn(x, xs, mods_p, mods_s, 1, 6, *ffn2, *lnp(1, 2))
    y_sample = xs.reshape(ns, 1, D_MODEL)

    return (y_prompt, y_sample, gla_p, s5_re_p, s5_im_p, shift_p, wkv_p,
            gla_s, s5_re_s, s5_im_s, shift_s, wkv_s)
```

```python
import functools
import math

import jax
import jax.numpy as jnp
from jax import lax
from jax.experimental import pallas as pl
from jax.experimental.pallas import tpu as pltpu

F32 = jnp.float32
BF16 = jnp.bfloat16

D_MODEL = 1024
DEPTH = 2
DN_ALPHA = (2 * DEPTH) ** 0.25
LN_EPS = 1e-5
RMS_EPS = 1e-5
FFN_RES = 0.5
D_FF = 2752
N_MODS = 9

GLA_HEADS = 4
GLA_DK = 64
GLA_DV = 128
GLA_KEY = GLA_HEADS * GLA_DK
GLA_VAL = GLA_HEADS * GLA_DV
GLA_GATE_RANK = 16
GLA_GATE_NORM = 16.0

S5_GROUP = 16
S5_WIDTH = 512
S5_GROUPS = S5_WIDTH // S5_GROUP
S5_STATE = 64
S5_CH = S5_GROUPS * S5_STATE

RWKV_HEAD = 64
RWKV_HEADS = D_MODEL // RWKV_HEAD
RWKV_LNX_EPS = 64e-5
NORM_EPS = 1e-12

LANES = 128
SUBLANES = 8
MXU_DIM = 256
VMEM_LIMIT = 58 * 1024 * 1024

CHUNK = 64
HEADS_PER_GROUP = MXU_DIM // RWKV_HEAD
N_GROUPS = RWKV_HEADS // HEADS_PER_GROUP
FF_CHUNKS = ((0, 1024), (1024, 2048), (2048, D_FF))
LORA_PAD = 128
GATE_LORA_PAD = 256
PERM_STRIDE = CHUNK + SUBLANES
SAMPLE_BLOCK = 8


def _dot(a, b):
    return jnp.dot(a, b, preferred_element_type=F32)


def _dot_nt(a, b):
    return lax.dot_general(a, b, (((1,), (1,)), ((), ())), preferred_element_type=F32)


def _bdot(a, w_ref_or_val):
    return _dot(a.astype(BF16), w_ref_or_val)


def _sigmoid(x):
    return 1.0 / (1.0 + jnp.exp(-x))


def _silu(x):
    return x * _sigmoid(x)


def _softplus(x):
    return jnp.maximum(x, 0.0) + jnp.log(1.0 + jnp.exp(-jnp.abs(x)))


def _gelu_tanh(x):
    c = math.sqrt(2.0 / math.pi)
    return 0.5 * x * (1.0 + jnp.tanh(c * (x + 0.044715 * (x * x * x))))


def _layer_norm(x, g, b):
    mu = jnp.mean(x, axis=-1, keepdims=True)
    d = x - mu
    var = jnp.mean(d * d, axis=-1, keepdims=True)
    return d * lax.rsqrt(var + LN_EPS) * g + b


def _split3(x):
    hi = x.astype(BF16)
    r1 = x - hi.astype(F32)
    mid = r1.astype(BF16)
    lo = (r1 - mid.astype(F32)).astype(BF16)
    return hi, mid, lo


def _exact_dot_left01(m01, x):
    hi, mid, lo = _split3(x)
    return _dot(m01, hi) + _dot(m01, mid) + _dot(m01, lo)


def _dot_right01(x, m01):
    hi = x.astype(BF16)
    lo = (x - hi.astype(F32)).astype(BF16)
    return _dot(hi, m01) + _dot(lo, m01)


def _seg_ones(seg):
    r = lax.broadcasted_iota(jnp.int32, (MXU_DIM, MXU_DIM), 0) // seg
    c = lax.broadcasted_iota(jnp.int32, (MXU_DIM, MXU_DIM), 1) // seg
    return jnp.where(r == c, 1.0, 0.0).astype(BF16)


def _seg_sum(x, seg, split=False):
    ones = _seg_ones(seg)
    tiles = []
    for i in range(x.shape[1] // MXU_DIM):
        xt = x[:, i * MXU_DIM:(i + 1) * MXU_DIM]
        tiles.append(_dot_right01(xt, ones) if split else _dot(xt.astype(BF16), ones))
    return jnp.concatenate(tiles, axis=1)


def _tril_ones(n):
    r = lax.broadcasted_iota(jnp.int32, (n, n), 0)
    c = lax.broadcasted_iota(jnp.int32, (n, n), 1)
    return jnp.where(r >= c, 1.0, 0.0).astype(BF16)


def _chunk_cumsum(x, n_batch):
    tri = _tril_ones(CHUNK)
    parts = [_exact_dot_left01(tri, x[b * CHUNK:(b + 1) * CHUNK, :]) for b in range(n_batch)]
    return jnp.concatenate(parts, axis=0)


def _head_stack(z, n_heads, width):
    head = lax.broadcasted_iota(jnp.int32, z.shape, 1) // width
    return jnp.concatenate([jnp.where(head == h, z, 0.0) for h in range(n_heads)], axis=0)


def _block_diag_mask(rows, cols, rblk, cblk):
    r = lax.broadcasted_iota(jnp.int32, (rows, cols), 0) // rblk
    c = lax.broadcasted_iota(jnp.int32, (rows, cols), 1) // cblk
    return r == c


def _full_spec(shape, grid_rank):
    zeros = (0,) * len(shape)
    if grid_rank == 1:
        return pl.BlockSpec(shape, lambda i: zeros, pipeline_mode=pl.Buffered(1))
    return pl.BlockSpec(shape, lambda i, j: zeros, pipeline_mode=pl.Buffered(1))


def _params(semantics):
    return pltpu.CompilerParams(dimension_semantics=semantics, vmem_limit_bytes=VMEM_LIMIT)


ADA_TN = 2304


def _ada_body(cs_ref, cp_ref, w_ref, b_ref, os_ref, op_ref):
    w = w_ref[...].astype(BF16)
    os_ref[...] = _bdot(_silu(cs_ref[...]), w) + b_ref[...]
    op_ref[...] = _bdot(_silu(cp_ref[...]), w) + b_ref[...]


def _ada_mods(c_sample, c_prompt, ada_w, ada_b):
    ns, bp = c_sample.shape[0], c_prompt.shape[0]
    width = N_MODS * D_MODEL
    return pl.pallas_call(
        _ada_body,
        grid=(DEPTH, width // ADA_TN),
        in_specs=[
            pl.BlockSpec((ns, D_MODEL), lambda l, j: (0, 0)),
            pl.BlockSpec((bp, D_MODEL), lambda l, j: (0, 0)),
            pl.BlockSpec((None, D_MODEL, ADA_TN), lambda l, j: (l, 0, j)),
            pl.BlockSpec((None, 1, ADA_TN), lambda l, j: (l, 0, j)),
        ],
        out_specs=[pl.BlockSpec((None, ns, ADA_TN), lambda l, j: (l, 0, j)),
                   pl.BlockSpec((None, bp, ADA_TN), lambda l, j: (l, 0, j))],
        out_shape=[jax.ShapeDtypeStruct((DEPTH, ns, width), F32),
                   jax.ShapeDtypeStruct((DEPTH, bp, width), F32)],
        compiler_params=_params(("arbitrary", "arbitrary")),
        name="ada_mods",
    )(c_sample, c_prompt, ada_w, ada_b.reshape(DEPTH, 1, width))


FFN_TM = 1024


def _ffn_tile(x, sh, sc, gt, wg_ref, wu_ref, wd_ref, g_ref, b_ref):
    h = (x * (1.0 + sc) + sh).astype(BF16)
    acc = None
    for lo, hi in FF_CHUNKS:
        g = _dot_nt(h, wg_ref[lo:hi, :])
        u = _dot_nt(h, wu_ref[lo:hi, :])
        y = _bdot(_silu(g) * u, wd_ref[lo:hi, :])
        acc = y if acc is None else acc + y
    r = FFN_RES * (1.0 + gt) * acc
    return _layer_norm(DN_ALPHA * x + r, g_ref[...], b_ref[...])


def _ffn_body(x_ref, sh_ref, sc_ref, gt_ref, xs_ref, shs_ref, scs_ref, gts_ref,
              wg_ref, wu_ref, wd_ref, g_ref, b_ref, o_ref, os_ref):
    last = pl.num_programs(0) - 1

    @pl.when(pl.program_id(0) < last)
    def _():
        o_ref[...] = _ffn_tile(x_ref[...], sh_ref[...], sc_ref[...], gt_ref[...],
                               wg_ref, wu_ref, wd_ref, g_ref, b_ref)

    @pl.when(pl.program_id(0) == last)
    def _():
        os_ref[...] = _ffn_tile(xs_ref[...], shs_ref[...], scs_ref[...], gts_ref[...],
                                wg_ref, wu_ref, wd_ref, g_ref, b_ref)


def _mod_specs_sample(layer, first, rows):
    return [
        pl.BlockSpec((None, rows, D_MODEL), functools.partial(
            lambda i, col: (layer, 0, col), col=first + k))
        for k in range(3)
    ]


def _ffn(x, xs, mods_p, mods_s, layer, first, wg, wu, wd, ln_g, ln_b):
    bsz, seq, _ = x.shape
    n = xs.shape[0]
    per_b = seq // FFN_TM
    n_tiles = bsz * per_b

    def tile_idx(t):
        tc = jnp.minimum(t, n_tiles - 1)
        return tc // per_b, tc % per_b

    mod_specs = [
        pl.BlockSpec((None, None, 1, D_MODEL), functools.partial(
            lambda t, col: (layer, tile_idx(t)[0], 0, col), col=first + k))
        for k in range(3)
    ]
    w_spec = pl.BlockSpec((None, D_FF, D_MODEL), lambda t: (layer, 0, 0), pipeline_mode=pl.Buffered(1))
    tile = pl.BlockSpec((None, FFN_TM, D_MODEL), lambda t: (*tile_idx(t), 0))
    return pl.pallas_call(
        _ffn_body,
        grid=(n_tiles + 1,),
        in_specs=[tile] + mod_specs + [_full_spec((n, D_MODEL), 1)] + _mod_specs_sample(layer, first, n)
        + [w_spec] * 3 + [_full_spec((1, D_MODEL), 1)] * 2,
        out_specs=[tile, _full_spec((n, D_MODEL), 1)],
        out_shape=[jax.ShapeDtypeStruct(x.shape, F32), jax.ShapeDtypeStruct(xs.shape, F32)],
        compiler_params=_params(("arbitrary",)),
        name="ffn",
    )(x, mods_p, mods_p, mods_p, xs, mods_s, mods_s, mods_s, wg, wu, wd, ln_g, ln_b)


def _col_groups(width):
    return [slice(lo, lo + width) for lo in range(0, D_MODEL, width)]


def _rwkv_mixes(h, prev, mu_ref):
    xx = prev - h
    mu = mu_ref[...]
    return [(h + xx * mu[i:i + 1, :]).astype(BF16) for i in range(6)]


def _rwkv_lora_in(mixes, w1, a1):
    return jnp.tanh(_dot(mixes[1], w1[...])).astype(BF16), _dot(mixes[4], a1[...]).astype(BF16)


def _rwkv_project_cols(cols, mixes, tw, ta, wr, wk, wv, w0, w2, a0, a2, kk_ref, ka_ref):
    r = _dot(mixes[0], wr[:, cols])
    k = _dot(mixes[2], wk[:, cols])
    v = _dot(mixes[3], wv[:, cols])
    wl = w0[:, cols] + _dot(tw, w2[:, cols])
    log_decay = -math.exp(-0.5) * _sigmoid(wl)
    a_sig = _sigmoid(a0[:, cols] + _dot(ta, a2[:, cols]))
    kk = k * kk_ref[:, cols]
    norm = jnp.sqrt(_seg_sum(kk * kk, RWKV_HEAD))
    kk = kk / jnp.maximum(norm, NORM_EPS)
    k2 = k * (1.0 + (a_sig - 1.0) * ka_ref[:, cols])
    return r, log_decay, k2, v, -kk, kk * a_sig


def _rwkv_bonus(r, k2, v, rk):
    return _seg_sum(r * k2 * rk, RWKV_HEAD) * v


def _rwkv_gate_in(xg, g1):
    return _sigmoid(_dot(xg, g1[...])).astype(BF16)


def _rwkv_out_cols(cols, y, bonus, sg, g2, wo, lng, lnb):
    gate = _dot(sg, g2[:, cols])
    inv_n = 1.0 / RWKV_HEAD
    mu_y = _seg_sum(y, RWKV_HEAD, split=True) * inv_n
    d = y - mu_y
    var = _seg_sum(d * d, RWKV_HEAD) * inv_n
    y = d * lax.rsqrt(var + RWKV_LNX_EPS) * lng[:, cols] + lnb[:, cols]
    return _bdot((y + bonus) * gate, wo[cols, :])


def _rwkv_weight_list(p):
    return [p["mu"], p["wr"], p["wk"], p["wv"], p["w0"], p["w1"], p["w2"], p["a0"], p["a1"], p["a2"],
            p["k_k"], p["k_a"], p["r_k"]]


def _rwkv_out_weight_list(p, ln_g, ln_b):
    return [p["g1"], p["g2"], p["wo"], p["lnx_g"], p["lnx_b"], ln_g, ln_b]


NEUMANN_LEVELS = int(math.log2(CHUNK)) - 1


def _wkv_chunks(problems):
    c = CHUNK
    hp = HEADS_PER_GROUP
    w = hp * RWKV_HEAD
    n = hp * c
    t_idx = lax.broadcasted_iota(jnp.int32, (c, w), 0)
    s_idx = lax.broadcasted_iota(jnp.int32, (c, w), 1) % c
    strict = t_idx > s_idx
    incl = t_idx >= s_idx
    eye = jnp.where(t_idx == s_idx, 1.0, 0.0)
    bd_mask = _block_diag_mask(w, w, RWKV_HEAD, RWKV_HEAD)
    idx = range(len(problems))
    At, Rt, Kt, Bt, V, S0, dec = zip(*problems)

    def bf(x):
        return x.astype(BF16)

    def stack(x, width):
        return bf(_head_stack(x, hp, width))

    X2 = [jnp.concatenate([At[i], Rt[i]], axis=0) for i in idx]
    KB = [jnp.concatenate([stack(Kt[i], RWKV_HEAD), stack(Bt[i], RWKV_HEAD)], axis=0) for i in idx]
    sc = [_dot_nt(X2[i], KB[i]) for i in idx]
    XS = [_dot_nt(X2[i], bf(S0[i])) for i in idx]
    Vbd = [stack(V[i], RWKV_HEAD) for i in idx]
    Aab = [jnp.where(strict, sc[i][:c, n:], 0.0) for i in idx]
    rhs = [XS[i][:c] + _dot(bf(jnp.where(strict, sc[i][:c, :n], 0.0)), Vbd[i]) for i in idx]

    Q = Aab
    P = [eye + Aab[i] for i in idx]
    Qbd = [stack(Q[i], c) for i in idx]
    for _ in range(NEUMANN_LEVELS):
        Q = [_dot(bf(Q[i]), Qbd[i]) for i in idx]
        Qbd = [stack(Q[i], c) for i in idx]
        P = [P[i] + _dot(bf(P[i]), Qbd[i]) for i in idx]
    U = [_dot(bf(P[i]), stack(rhs[i], RWKV_HEAD)) for i in idx]

    Y = [XS[i][c:] + _dot(bf(jnp.where(incl, sc[i][c:, :n], 0.0)), Vbd[i])
         + _dot(bf(jnp.where(incl, sc[i][c:, n:], 0.0)), stack(U[i], RWKV_HEAD)) for i in idx]

    upd = [_dot(bf(jnp.concatenate([V[i], U[i]], axis=0).T),
                bf(jnp.concatenate([Kt[i], Bt[i]], axis=0))) for i in idx]
    S1 = [(S0[i] + jnp.where(bd_mask, upd[i], 0.0)) * dec[i] for i in idx]
    return Y, S1


def _rwkv_prompt_body(x_ref, sh_ref, sc_ref, gt_ref,
                      mu, wr, wk, wv, w0, w1, w2, a0, a1, a2, kk_ref, ka_ref, rk_ref,
                      g1, g2, wo, lng, lnb, ln_g, ln_b,
                      o_ref, shift_ref, state_ref,
                      carry, s_at, s_rt, s_kt, s_bt, s_v, s_dec, s_bonus, s_sg):
    nb = x_ref.shape[0]
    rows = nb * CHUNK
    gw = HEADS_PER_GROUP * RWKV_HEAD
    step = pl.program_id(0)

    @pl.when(step == 0)
    def _():
        carry[...] = jnp.zeros(carry.shape, F32)
        state_ref[...] = jnp.zeros(state_ref.shape, F32)

    h3 = x_ref[...] * (1.0 + sc_ref[...]) + sh_ref[...]
    h = h3.reshape(rows, D_MODEL)
    first = lax.broadcasted_iota(jnp.int32, (rows, D_MODEL), 0) % CHUNK == 0
    carried = jnp.broadcast_to(carry[...], (nb, CHUNK, D_MODEL)).reshape(rows, D_MODEL)
    prev = jnp.where(first, carried, pltpu.roll(h, 1, axis=0))
    carry[...] = h3[:, CHUNK - 1:CHUNK, :]
    shift_ref[...] = h3[:, CHUNK - 1:CHUNK, :]

    mixes = _rwkv_mixes(h, prev, mu)
    tw, ta = _rwkv_lora_in(mixes, w1, a1)
    s_sg[...] = _rwkv_gate_in(mixes[5], g1)
    groups = _col_groups(gw)

    def project(gi):
        cols = groups[gi]
        r, lw, k2, v, av, bv = _rwkv_project_cols(
            cols, mixes, tw, ta, wr, wk, wv, w0, w2, a0, a2, kk_ref, ka_ref)
        s_bonus[:, cols] = _rwkv_bonus(r, k2, v, rk_ref[:, cols])
        g = _chunk_cumsum(lw, nb)
        e_neg = jnp.exp(-g)
        s_at[gi] = (av * jnp.exp(g - lw)).astype(BF16)
        s_rt[gi] = (r * jnp.exp(g)).astype(BF16)
        s_kt[gi] = (k2 * e_neg).astype(BF16)
        s_bt[gi] = (bv * e_neg).astype(BF16)
        s_v[gi] = v.astype(BF16)
        s_dec[gi] = jnp.exp(g.reshape(nb, CHUNK, gw)[:, CHUNK - 1:CHUNK, :])

    def scan(gi):
        sls = [slice(b * CHUNK, (b + 1) * CHUNK) for b in range(nb)]
        Y, S1 = _wkv_chunks([
            (s_at[gi, sl, :], s_rt[gi, sl, :], s_kt[gi, sl, :].astype(F32), s_bt[gi, sl, :].astype(F32),
             s_v[gi, sl, :].astype(F32), state_ref[b, gi], s_dec[gi, b])
            for b, sl in enumerate(sls)])
        for b in range(nb):
            state_ref[b, gi] = S1[b]
        return jnp.concatenate(Y, axis=0)

    project(0)
    out = None
    for gi in range(N_GROUPS):
        y = scan(gi)
        if gi + 1 < N_GROUPS:
            project(gi + 1)
        part = _rwkv_out_cols(groups[gi], y, s_bonus[:, groups[gi]], s_sg[...], g2, wo, lng, lnb)
        out = part if out is None else out + part
    gt = jnp.broadcast_to(gt_ref[...], (nb, CHUNK, D_MODEL)).reshape(rows, D_MODEL)
    x = x_ref[...].reshape(rows, D_MODEL)
    o_ref[...] = _layer_norm(DN_ALPHA * x + (1.0 + gt) * out,
                             ln_g[...], ln_b[...]).reshape(nb, CHUNK, D_MODEL)


def _rwkv_prompt(x, mods_p, layer, p, ln_g, ln_b):
    bsz, seq, _ = x.shape
    rows = bsz * CHUNK
    gw = HEADS_PER_GROUP * RWKV_HEAD
    weights = _rwkv_weight_list(p) + _rwkv_out_weight_list(p, ln_g, ln_b)
    mod_specs = [
        pl.BlockSpec((None, bsz, 1, D_MODEL), functools.partial(lambda i, col: (layer, 0, 0, col), col=3 + k))
        for k in range(3)
    ]
    slab16 = pltpu.VMEM((N_GROUPS, rows, gw), BF16)
    out, shift, state = pl.pallas_call(
        _rwkv_prompt_body,
        grid=(seq // CHUNK,),
        in_specs=[pl.BlockSpec((bsz, CHUNK, D_MODEL), lambda i: (0, i, 0))] + mod_specs
        + [_full_spec(w.shape, 1) for w in weights],
        out_specs=[
            pl.BlockSpec((bsz, CHUNK, D_MODEL), lambda i: (0, i, 0)),
            _full_spec((bsz, 1, D_MODEL), 1),
            _full_spec((bsz, N_GROUPS, gw, gw), 1),
        ],
        out_shape=[
            jax.ShapeDtypeStruct(x.shape, F32),
            jax.ShapeDtypeStruct((bsz, 1, D_MODEL), F32),
            jax.ShapeDtypeStruct((bsz, N_GROUPS, gw, gw), F32),
        ],
        scratch_shapes=[pltpu.VMEM((bsz, 1, D_MODEL), F32)] + [slab16] * 5
        + [pltpu.VMEM((N_GROUPS, bsz, 1, gw), F32),
           pltpu.VMEM((rows, D_MODEL), F32), pltpu.VMEM((rows, GATE_LORA_PAD), BF16)],
        compiler_params=_params(("arbitrary",)),
        name="rwkv_prompt",
    )(x, mods_p, mods_p, mods_p, *weights)
    s5d = state.reshape(bsz, N_GROUPS, HEADS_PER_GROUP, RWKV_HEAD, gw)
    wkv = jnp.stack([s5d[:, :, hh, :, hh * RWKV_HEAD:(hh + 1) * RWKV_HEAD]
                     for hh in range(HEADS_PER_GROUP)], axis=2)
    return out, shift.reshape(bsz, D_MODEL), wkv.reshape(bsz, RWKV_HEADS, RWKV_HEAD, RWKV_HEAD)


def _rwkv_sample_pre_body(x_ref, sh_ref, sc_ref, prev_ref,
                          mu, wr, wk, wv, w0, w1, w2, a0, a1, a2, kk_ref, ka_ref, rk_ref,
                          h_ref, rt_ref, wt_ref, kt_ref, vt_ref, at_ref, bt_ref, xg_ref, bonus_ref):
    h = x_ref[...] * (1.0 + sc_ref[...]) + sh_ref[...]
    h_ref[...] = h
    mixes = _rwkv_mixes(h, prev_ref[...], mu)
    xg_ref[...] = mixes[5]
    tw, ta = _rwkv_lora_in(mixes, w1, a1)
    for cols in _col_groups(MXU_DIM):
        r, lw, k2, v, av, bv = _rwkv_project_cols(
            cols, mixes, tw, ta, wr, wk, wv, w0, w2, a0, a2, kk_ref, ka_ref)
        bonus_ref[:, cols] = _rwkv_bonus(r, k2, v, rk_ref[:, cols])
        rt_ref[cols, :] = r.T
        wt_ref[cols, :] = jnp.exp(lw).T
        kt_ref[cols, :] = k2.T
        vt_ref[cols, :] = v.T
        at_ref[cols, :] = av.T
        bt_ref[cols, :] = bv.T


def _rwkv_sample_step_body(s_ref, r_ref, w_ref, k_ref, v_ref, a_ref, b_ref, so_ref, y_ref):
    a, w, b, k, r = a_ref[...], w_ref[...], b_ref[...], k_ref[...], r_ref[...]

    def row_step(i, carry_val):
        S = s_ref[i]
        sa = jnp.sum(S * a, axis=0, keepdims=True)
        S1 = S * w + sa * b + v_ref[pl.ds(i, 1), :] * k
        so_ref[i] = S1
        y_ref[pl.ds(i, 1), :] = jnp.sum(S1 * r, axis=0, keepdims=True)
        return carry_val

    lax.fori_loop(0, RWKV_HEAD, row_step, 0, unroll=4)


def _rwkv_sample_post_body(yt_ref, bonus_ref, xg_ref, x_ref, gt_ref,
                           g1, g2, wo, lng, lnb, ln_g, ln_b, o_ref):
    sg = _rwkv_gate_in(xg_ref[...], g1)
    out = None
    for cols in _col_groups(MXU_DIM):
        part = _rwkv_out_cols(cols, yt_ref[cols, :].T, bonus_ref[:, cols], sg, g2, wo, lng, lnb)
        out = part if out is None else out + part
    o_ref[...] = _layer_norm(DN_ALPHA * x_ref[...] + (1.0 + gt_ref[...]) * out, ln_g[...], ln_b[...])


def _to_columns(a):
    n, f = a.shape
    return a.reshape(n // SAMPLE_BLOCK, SAMPLE_BLOCK, f).transpose(0, 2, 1)


def _rwkv_sample(x, mods_s, layer, s_shift, s_wkv, p, ln_g, ln_b):
    n = x.shape[0]
    row = jax.ShapeDtypeStruct((n, D_MODEL), F32)
    row_spec = _full_spec((n, D_MODEL), 1)
    pre_w = _rwkv_weight_list(p)
    ms = _mod_specs_sample(layer, 3, n)
    col = jax.ShapeDtypeStruct((D_MODEL, n), F32)
    col_spec = _full_spec((D_MODEL, n), 1)
    h, *vecs, xg, bonus = pl.pallas_call(
        _rwkv_sample_pre_body,
        grid=(1,),
        in_specs=[row_spec, ms[0], ms[1], row_spec] + [_full_spec(a.shape, 1) for a in pre_w],
        out_specs=[row_spec] + [col_spec] * 6 + [row_spec] * 2,
        out_shape=[row] + [col] * 6 + [jax.ShapeDtypeStruct((n, D_MODEL), BF16), row],
        compiler_params=_params(("arbitrary",)),
        name="rwkv_sample_pre",
    )(x, mods_s, mods_s, s_shift, *pre_w)

    st_spec = pl.BlockSpec((None, RWKV_HEAD, RWKV_HEAD, n), lambda hd: (hd, 0, 0, 0))
    vec_spec = pl.BlockSpec((RWKV_HEAD, n), lambda hd: (hd, 0))
    st_new, yt = pl.pallas_call(
        _rwkv_sample_step_body,
        grid=(RWKV_HEADS,),
        in_specs=[st_spec] + [vec_spec] * len(vecs),
        out_specs=[st_spec, vec_spec],
        out_shape=[jax.ShapeDtypeStruct((RWKV_HEADS, RWKV_HEAD, RWKV_HEAD, n), F32), col],
        compiler_params=_params(("arbitrary",)),
        name="rwkv_sample_step",
    )(jnp.transpose(s_wkv, (1, 2, 3, 0)), *vecs)
    s_new = jnp.transpose(st_new, (3, 0, 1, 2))

    post_w = _rwkv_out_weight_list(p, ln_g, ln_b)
    out = pl.pallas_call(
        _rwkv_sample_post_body,
        grid=(1,),
        in_specs=[col_spec] + [row_spec] * 3 + [ms[2]] + [_full_spec(a.shape, 1) for a in post_w],
        out_specs=row_spec,
        out_shape=row,
        compiler_params=_params(("arbitrary",)),
        name="rwkv_sample_post",
    )(yt, bonus, xg, x, mods_s, *post_w)
    return out, h, s_new


S5_TILE_GROUPS = LANES // S5_GROUP
S5_TILES = S5_GROUPS // S5_TILE_GROUPS


def _s5_prep_body(are_ref, aim_ref, ls_ref, bre_ref, bim_ref, cre_ref, cim_ref,
                  abre_ref, abim_ref, bbre_ref, bbim_ref, cdre_ref, cdim_ref):
    a_re = are_ref[...]
    a_im = aim_ref[...]
    dt = jnp.exp(ls_ref[...])
    mag = jnp.exp(a_re * dt)
    ab_re = mag * jnp.cos(a_im * dt)
    ab_im = mag * jnp.sin(a_im * dt)
    den = a_re * a_re + a_im * a_im
    nr = ab_re - 1.0
    z_re = (nr * a_re + ab_im * a_im) / den
    z_im = (ab_im * a_re - nr * a_im) / den
    b_re = bre_ref[...]
    b_im = bim_ref[...]
    abre_ref[...] = ab_re
    abim_ref[...] = ab_im
    bb_re = (z_re * b_re - z_im * b_im).astype(BF16)
    bb_im = (z_re * b_im + z_im * b_re).astype(BF16)
    for ref in (bbre_ref, bbim_ref, cdre_ref, cdim_ref):
        ref[...] = jnp.zeros(ref.shape, BF16)
    c, p = S5_GROUP, S5_STATE
    for g in range(S5_GROUPS):
        k, j = divmod(g, S5_TILE_GROUPS)
        bbre_ref[k, j * c:(j + 1) * c, j * p:(j + 1) * p] = bb_re[g]
        bbim_ref[k, j * c:(j + 1) * c, j * p:(j + 1) * p] = bb_im[g]
        cdre_ref[k, j * p:(j + 1) * p, j * c:(j + 1) * c] = cre_ref[g].astype(BF16)
        cdim_ref[k, j * p:(j + 1) * p, j * c:(j + 1) * c] = cim_ref[g].astype(BF16)


def _s5_prepare(s5_a_re, s5_a_im, s5_log_step, s5_b_re, s5_b_im, s5_c_re, s5_c_im, s5_d):
    G, P, C = S5_GROUPS, S5_STATE, S5_GROUP
    small = jax.ShapeDtypeStruct((G, 1, P), F32)
    bb_shape = (S5_TILES, LANES, S5_TILE_GROUPS * P)
    cd_shape = (S5_TILES, S5_TILE_GROUPS * P, LANES)
    ls = jnp.broadcast_to(s5_log_step.reshape(G, 1, 1), (G, 1, P))
    ab_re, ab_im, bb_re, bb_im, cd_re, cd_im = pl.pallas_call(
        _s5_prep_body,
        grid=(1,),
        in_specs=[_full_spec((G, 1, P), 1)] * 3 + [_full_spec((G, C, P), 1)] * 2
        + [_full_spec((G, P, C), 1)] * 2,
        out_specs=[_full_spec((G, 1, P), 1)] * 2 + [_full_spec(bb_shape, 1)] * 2
        + [_full_spec(cd_shape, 1)] * 2,
        out_shape=[small, small] + [jax.ShapeDtypeStruct(bb_shape, BF16)] * 2
        + [jax.ShapeDtypeStruct(cd_shape, BF16)] * 2,
        name="s5_prepare",
    )(s5_a_re.reshape(G, 1, P), s5_a_im.reshape(G, 1, P), ls,
      s5_b_re.transpose(0, 2, 1), s5_b_im.transpose(0, 2, 1),
      s5_c_re.transpose(0, 2, 1), s5_c_im.transpose(0, 2, 1))
    return dict(
        ab_re=ab_re.reshape(1, G * P), ab_im=ab_im.reshape(1, G * P),
        bb_re=bb_re, bb_im=bb_im, c_re=cd_re, c_im=cd_im, d=s5_d.reshape(1, G * C),
    )


def _s5_input(u, bb_re_ref, bb_im_ref):
    n_tiles = bb_re_ref.shape[0]
    ub = u.astype(BF16)
    re = [_dot(ub[:, k * LANES:(k + 1) * LANES], bb_re_ref[k]) for k in range(n_tiles)]
    im = [_dot(ub[:, k * LANES:(k + 1) * LANES], bb_im_ref[k]) for k in range(n_tiles)]
    return jnp.concatenate(re, axis=1), jnp.concatenate(im, axis=1)


def _s5_output(h_re, h_im, c_re_ref, c_im_ref):
    n_tiles = c_re_ref.shape[0]
    w = h_re.shape[1] // n_tiles
    hr = h_re.astype(BF16)
    hi = h_im.astype(BF16)
    ys = [_dot(hr[:, k * w:(k + 1) * w], c_re_ref[k]) - _dot(hi[:, k * w:(k + 1) * w], c_im_ref[k])
          for k in range(n_tiles)]
    return jnp.concatenate(ys, axis=1)


C_Q, C_K, C_V, C_G, C_U, C_GK, C_END = 0, 256, 512, 1024, 1536, 2048, 2176


def _mix_project(h, w_in, w_gk, b_gk):
    p = _bdot(h, w_in[...])
    q = p[:, C_Q:C_K] * (GLA_DK ** -0.5)
    k = p[:, C_K:C_V]
    v = p[:, C_V:C_G]
    gg = p[:, C_G:C_U]
    u = p[:, C_U:C_GK]
    z = _bdot(p[:, C_GK:C_END], w_gk[...]) + b_gk[...]
    gk = -_softplus(-z) * (1.0 / GLA_GATE_NORM)
    return q, k, v, gg, u, gk


def _gla_finish(o, gg, norm_g):
    parts = []
    for hh in range(GLA_HEADS):
        oh = o[:, hh * GLA_DV:(hh + 1) * GLA_DV]
        parts.append(oh * lax.rsqrt(jnp.mean(oh * oh, axis=-1, keepdims=True) + RMS_EPS) * norm_g[...])
    return jnp.concatenate(parts, axis=1) * _silu(gg)


def _mix_output(o_gla, y_s5, x, gt, w_glu, b_glu, w_out, ln_g, ln_b):
    z = _gelu_tanh(y_s5)
    o_s5 = z * _sigmoid(_bdot(z, w_glu[...]) + b_glu[...])
    out = _bdot(o_gla, w_out[0:GLA_VAL, :]) + _bdot(o_s5, w_out[GLA_VAL:GLA_VAL + S5_WIDTH, :])
    return _layer_norm(DN_ALPHA * x + (1.0 + gt) * out, ln_g[...], ln_b[...])


def _gla_chunks(problems):
    c = CHUNK
    mid = c // 2 - 1
    idx = range(len(problems))
    q, k, v, bc, ST = zip(*problems)
    t_idx = lax.broadcasted_iota(jnp.int32, (c, GLA_KEY), 0)
    s_idx = lax.broadcasted_iota(jnp.int32, (c, GLA_KEY), 1) % c
    causal = t_idx >= s_idx
    bd_mask = _block_diag_mask(GLA_VAL, GLA_KEY, GLA_DV, GLA_DK)

    def bf(x):
        return x.astype(BF16)

    b_mid = [bc[i][mid:mid + 1, :] for i in idx]
    b_last = [bc[i][c - 1:c, :] for i in idx]
    q_in = [bf(q[i] * jnp.exp(bc[i] - b_mid[i])) for i in idx]
    k_in = [bf(_head_stack(k[i] * jnp.exp(b_mid[i] - bc[i]), GLA_HEADS, GLA_DK)) for i in idx]
    scores = [jnp.where(causal, _dot_nt(q_in[i], k_in[i]), 0.0) for i in idx]
    o_inter = [_dot_nt(bf(q[i] * jnp.exp(bc[i])), bf(ST[i])) for i in idx]
    upd = [_dot(bf(v[i].T), bf(k[i] * jnp.exp(b_last[i] - bc[i]))) for i in idx]
    o = [o_inter[i] + _dot(bf(scores[i]), bf(_head_stack(v[i], GLA_HEADS, GLA_DV))) for i in idx]
    ST1 = [ST[i] * jnp.exp(b_last[i]) + jnp.where(bd_mask, upd[i], 0.0) for i in idx]
    return o, ST1


def _mix_prompt_body(x_ref, sh_ref, sc_ref, gt_ref,
                     w_in, w_gk, b_gk, norm_g, ab_re_ref, ab_im_ref, bb_re, bb_im, c_re, c_im, d_ref,
                     w_glu, b_glu, w_out, ln_g, ln_b,
                     o_ref, gla_ref, hre_ref, him_ref,
                     s_q, s_k, s_v, s_bc, s_perm, s_re, s_im):
    nb = x_ref.shape[0]
    rows = nb * CHUNK
    step = pl.program_id(0)

    @pl.when(step == 0)
    def _():
        gla_ref[...] = jnp.zeros(gla_ref.shape, F32)
        hre_ref[...] = jnp.zeros(hre_ref.shape, F32)
        him_ref[...] = jnp.zeros(him_ref.shape, F32)

    x3 = x_ref[...]
    h3 = x3 * (1.0 + sc_ref[...]) + sh_ref[...]
    x = x3.reshape(rows, D_MODEL)
    h = h3.reshape(rows, D_MODEL)
    q, k, v, gg, u, gk = _mix_project(h, w_in, w_gk, b_gk)

    s_q[...] = q
    s_k[...] = k
    s_v[...] = v
    s_bc[...] = _chunk_cumsum(gk, nb)

    sls = [slice(b * CHUNK, (b + 1) * CHUNK) for b in range(nb)]
    o, ST1 = _gla_chunks([(s_q[sl, :], s_k[sl, :], s_v[sl, :], s_bc[sl, :], gla_ref[b])
                          for b, sl in enumerate(sls)])
    for b in range(nb):
        gla_ref[b] = ST1[b]
    o_gla = _gla_finish(jnp.concatenate(o, axis=0), gg, norm_g)

    n_lt = S5_WIDTH // LANES
    for b in range(nb):
        for lt in range(n_lt):
            s_perm[lt, b * PERM_STRIDE:b * PERM_STRIDE + CHUNK, :] = (
                u[b * CHUNK:(b + 1) * CHUNK, lt * LANES:(lt + 1) * LANES])
    u_tb = jnp.concatenate(
        [jnp.concatenate([s_perm[lt, pl.ds(t, nb, stride=PERM_STRIDE), :] for lt in range(n_lt)], axis=1)
         for t in range(CHUNK)], axis=0)
    bu_re, bu_im = _s5_input(u_tb, bb_re, bb_im)
    s_re[...] = bu_re
    s_im[...] = bu_im

    n_ch = 4
    cw = S5_CH // n_ch
    for ci in range(n_ch):
        lanes = slice(ci * cw, (ci + 1) * cw)
        a_re = jnp.broadcast_to(ab_re_ref[:, lanes], (nb, cw))
        a_im = jnp.broadcast_to(ab_im_ref[:, lanes], (nb, cw))

        h_re, h_im = hre_ref[:, lanes], him_ref[:, lanes]
        for t in range(CHUNK):
            sl = slice(t * nb, (t + 1) * nb)
            h_re, h_im = (a_re * h_re - a_im * h_im + s_re[sl, lanes],
                          a_re * h_im + a_im * h_re + s_im[sl, lanes])
            s_re[sl, lanes] = h_re
            s_im[sl, lanes] = h_im
        hre_ref[:, lanes] = h_re
        him_ref[:, lanes] = h_im

    y_tb = _s5_output(s_re[...], s_im[...], c_re, c_im)
    for t in range(CHUNK):
        for lt in range(n_lt):
            s_perm[lt, pl.ds(t, nb, stride=PERM_STRIDE), :] = (
                y_tb[t * nb:(t + 1) * nb, lt * LANES:(lt + 1) * LANES])
    y_s5 = jnp.concatenate(
        [jnp.concatenate([s_perm[lt, b * PERM_STRIDE:b * PERM_STRIDE + CHUNK, :] for lt in range(n_lt)],
                         axis=1) for b in range(nb)], axis=0)
    y_s5 = y_s5 + d_ref[...] * u

    gt = jnp.broadcast_to(gt_ref[...], (nb, CHUNK, D_MODEL)).reshape(rows, D_MODEL)
    out = _mix_output(o_gla, y_s5, x, gt, w_glu, b_glu, w_out, ln_g, ln_b)
    o_ref[...] = out.reshape(nb, CHUNK, D_MODEL)


def _mix_weight_list(m, s5):
    return [m["w_in"], m["w_gk"], m["b_gk"], m["norm_g"], s5["ab_re"], s5["ab_im"], s5["bb_re"], s5["bb_im"],
            s5["c_re"], s5["c_im"], s5["d"], m["w_glu"], m["b_glu"], m["w_out"]]


def _mix_prompt(x, mods_p, layer, m, s5, ln_g, ln_b):
    bsz, seq, _ = x.shape
    rows = bsz * CHUNK
    weights = _mix_weight_list(m, s5) + [ln_g, ln_b]
    mod_specs = [
        pl.BlockSpec((None, bsz, 1, D_MODEL), functools.partial(lambda i, col: (layer, 0, 0, col), col=3 + k))
        for k in range(3)
    ]
    out, gla_t, h_re, h_im = pl.pallas_call(
        _mix_prompt_body,
        grid=(seq // CHUNK,),
        in_specs=[pl.BlockSpec((bsz, CHUNK, D_MODEL), lambda i: (0, i, 0))] + mod_specs
        + [_full_spec(w.shape, 1) for w in weights],
        out_specs=[
            pl.BlockSpec((bsz, CHUNK, D_MODEL), lambda i: (0, i, 0)),
            _full_spec((bsz, GLA_VAL, GLA_KEY), 1),
            _full_spec((bsz, S5_CH), 1),
            _full_spec((bsz, S5_CH), 1),
        ],
        out_shape=[
            jax.ShapeDtypeStruct(x.shape, F32),
            jax.ShapeDtypeStruct((bsz, GLA_VAL, GLA_KEY), F32),
            jax.ShapeDtypeStruct((bsz, S5_CH), F32),
            jax.ShapeDtypeStruct((bsz, S5_CH), F32),
        ],
        scratch_shapes=[
            pltpu.VMEM((rows, GLA_KEY), F32), pltpu.VMEM((rows, GLA_KEY), F32),
            pltpu.VMEM((rows, GLA_VAL), F32), pltpu.VMEM((rows, GLA_KEY), F32),
            pltpu.VMEM((S5_WIDTH // LANES, bsz * PERM_STRIDE, LANES), F32),
            pltpu.VMEM((rows, S5_CH), F32), pltpu.VMEM((rows, S5_CH), F32),
        ],
        compiler_params=_params(("arbitrary",)),
        name="gla_s5_prompt",
    )(x, mods_p, mods_p, mods_p, *weights)
    s4 = gla_t.reshape(bsz, GLA_HEADS, GLA_DV, GLA_KEY)
    gla = jnp.stack([s4[:, hh, :, hh * GLA_DK:(hh + 1) * GLA_DK] for hh in range(GLA_HEADS)],
                    axis=1).transpose(0, 1, 3, 2)
    return (out, gla, h_re.reshape(bsz, S5_GROUPS, S5_STATE), h_im.reshape(bsz, S5_GROUPS, S5_STATE))


def _mix_sample_pre_body(x_ref, sh_ref, sc_ref, hre_ref, him_ref,
                         w_in, w_gk, b_gk, ab_re_ref, ab_im_ref, bb_re, bb_im, c_re, c_im, d_ref,
                         q_ref, k_ref, v_ref, gg_ref, dec_ref, y_ref, nre_ref, nim_ref):
    h = x_ref[...] * (1.0 + sc_ref[...]) + sh_ref[...]
    q, k, v, gg, u, gk = _mix_project(h, w_in, w_gk, b_gk)
    q_ref[...] = q
    k_ref[...] = k
    v_ref[...] = v
    gg_ref[...] = gg
    dec_ref[...] = jnp.exp(gk)
    bu_re, bu_im = _s5_input(u, bb_re, bb_im)
    a_re = ab_re_ref[...]
    a_im = ab_im_ref[...]
    h_re = hre_ref[...]
    h_im = him_ref[...]
    n_re = a_re * h_re - a_im * h_im + bu_re
    n_im = a_re * h_im + a_im * h_re + bu_im
    nre_ref[...] = n_re
    nim_ref[...] = n_im
    y_ref[...] = _s5_output(n_re, n_im, c_re, c_im) + d_ref[...] * u


def _gla_sample_step_body(s_ref, qc_ref, kc_ref, dc_ref, v_ref, so_ref, o_ref):
    for j in range(SAMPLE_BLOCK):
        for hh in range(GLA_HEADS):
            rows = slice(hh * GLA_DK, (hh + 1) * GLA_DK)
            lanes = slice(hh * GLA_DV, (hh + 1) * GLA_DV)
            S1 = dc_ref[rows, j:j + 1] * s_ref[j, hh] + kc_ref[rows, j:j + 1] * v_ref[j:j + 1, lanes]
            so_ref[j, hh] = S1
            o_ref[j:j + 1, lanes] = jnp.sum(qc_ref[rows, j:j + 1] * S1, axis=0, keepdims=True)


def _mix_sample_post_body(o_ref_in, gg_ref, y_ref, x_ref, gt_ref, norm_g, w_glu, b_glu, w_out, ln_g, ln_b,
                          o_ref):
    o_gla = _gla_finish(o_ref_in[...], gg_ref[...], norm_g)
    o_ref[...] = _mix_output(o_gla, y_ref[...], x_ref[...], gt_ref[...], w_glu, b_glu, w_out, ln_g, ln_b)


def _mix_sample(x, mods_s, layer, s_gla, s_re, s_im, m, s5, ln_g, ln_b):
    n = x.shape[0]
    ms = _mod_specs_sample(layer, 3, n)
    row_spec = _full_spec((n, D_MODEL), 1)

    def sds(w):
        return jax.ShapeDtypeStruct((n, w), F32)

    def fs(w):
        return _full_spec((n, w), 1)

    pre_w = [m["w_in"], m["w_gk"], m["b_gk"], s5["ab_re"], s5["ab_im"], s5["bb_re"], s5["bb_im"],
             s5["c_re"], s5["c_im"], s5["d"]]
    widths = [GLA_KEY, GLA_KEY, GLA_VAL, GLA_VAL, GLA_KEY, S5_WIDTH, S5_CH, S5_CH]
    q, k, v, gg, dec, y_s5, n_re, n_im = pl.pallas_call(
        _mix_sample_pre_body,
        grid=(1,),
        in_specs=[row_spec, ms[0], ms[1], fs(S5_CH), fs(S5_CH)] + [_full_spec(a.shape, 1) for a in pre_w],
        out_specs=[fs(w) for w in widths],
        out_shape=[sds(w) for w in widths],
        compiler_params=_params(("arbitrary",)),
        name="gla_s5_sample_pre",
    )(x, mods_s, mods_s, s_re.reshape(n, S5_CH), s_im.reshape(n, S5_CH), *pre_w)

    nblk = n // SAMPLE_BLOCK
    st_spec = pl.BlockSpec((SAMPLE_BLOCK, GLA_HEADS, GLA_DK, GLA_DV), lambda i: (i, 0, 0, 0))
    col_spec = pl.BlockSpec((None, GLA_KEY, SAMPLE_BLOCK), lambda i: (i, 0, 0))
    v_spec = pl.BlockSpec((SAMPLE_BLOCK, GLA_VAL), lambda i: (i, 0))
    s_new, o = pl.pallas_call(
        _gla_sample_step_body,
        grid=(nblk,),
        in_specs=[st_spec, col_spec, col_spec, col_spec, v_spec],
        out_specs=[st_spec, v_spec],
        out_shape=[jax.ShapeDtypeStruct(s_gla.shape, F32), sds(GLA_VAL)],
        compiler_params=_params(("arbitrary",)),
        name="gla_sample_step",
    )(s_gla, _to_columns(q), _to_columns(k), _to_columns(dec), v)

    post_w = [m["norm_g"], m["w_glu"], m["b_glu"], m["w_out"], ln_g, ln_b]
    out = pl.pallas_call(
        _mix_sample_post_body,
        grid=(1,),
        in_specs=[fs(GLA_VAL), fs(GLA_VAL), fs(S5_WIDTH), row_spec, ms[2]]
        + [_full_spec(a.shape, 1) for a in post_w],
        out_specs=row_spec,
        out_shape=sds(D_MODEL),
        compiler_params=_params(("arbitrary",)),
        name="gla_s5_sample_post",
    )(o, gg, y_s5, x, mods_s, *post_w)
    return (out, s_new, n_re.reshape(n, S5_GROUPS, S5_STATE), n_im.reshape(n, S5_GROUPS, S5_STATE))


def _pad_to(a, axis, size):
    pad = [(0, 0)] * a.ndim
    pad[axis] = (0, size - a.shape[axis])
    return jnp.pad(a, pad)


def _row(a):
    return a.reshape(1, -1).astype(F32)


def _prep_ffn(wg, wu, wd):
    return (jnp.transpose(wg, (0, 2, 1)).astype(BF16), jnp.transpose(wu, (0, 2, 1)).astype(BF16),
            wd.astype(BF16))


def _prep_mix(w_in, w_out, gla_w_gk, gla_b_gk, gla_norm_g, s5_w_glu, s5_b_glu):
    q, k, v, g, gk_low, u = jnp.split(
        w_in, (GLA_KEY, 2 * GLA_KEY, 2 * GLA_KEY + GLA_VAL, 2 * GLA_KEY + 2 * GLA_VAL,
               2 * GLA_KEY + 2 * GLA_VAL + GLA_GATE_RANK), axis=1)
    w_in_r = jnp.concatenate([q, k, v, g, u, _pad_to(gk_low, 1, LANES)], axis=1)
    return dict(
        w_in=w_in_r.astype(BF16), w_out=w_out.astype(BF16),
        w_gk=_pad_to(gla_w_gk, 0, LANES).astype(BF16), b_gk=_row(gla_b_gk), norm_g=_row(gla_norm_g),
        w_glu=s5_w_glu.astype(BF16), b_glu=_row(s5_b_glu),
    )


def _prep_rwkv(mu, w_r, w_k, w_v, w_o, w0, w1, w2, a0, a1, a2, g1, g2, k_k, k_a, r_k, lnx_g, lnx_b):
    return dict(
        mu=_pad_to(mu, 0, SUBLANES), wr=w_r.astype(BF16), wk=w_k.astype(BF16), wv=w_v.astype(BF16),
        wo=w_o.astype(BF16), w0=_row(w0),
        w1=_pad_to(w1, 1, LORA_PAD).astype(BF16), w2=_pad_to(w2, 0, LORA_PAD).astype(BF16),
        a0=_row(a0),
        a1=_pad_to(a1, 1, LORA_PAD).astype(BF16), a2=_pad_to(a2, 0, LORA_PAD).astype(BF16),
        g1=_pad_to(g1, 1, GATE_LORA_PAD).astype(BF16), g2=_pad_to(g2, 0, GATE_LORA_PAD).astype(BF16),
        k_k=_row(k_k), k_a=_row(k_a), r_k=_row(r_k), lnx_g=_row(lnx_g), lnx_b=_row(lnx_b),
    )


def kernel(x_prompt, x_sample, state_gla, state_s5_re, state_s5_im, state_rwkv_shift, state_rwkv_wkv,
           c_prompt, c_sample, ada_w, ada_b, ln_g, ln_b,
           ffn1_wg, ffn1_wu, ffn1_wd, ffn2_wg, ffn2_wu, ffn2_wd,
           w_in, w_out, gla_w_gk, gla_b_gk, gla_norm_g,
           s5_a_re, s5_a_im, s5_log_step, s5_b_re, s5_b_im, s5_c_re, s5_c_im, s5_d, s5_w_glu, s5_b_glu,
           rwkv_mu, rwkv_w_r, rwkv_w_k, rwkv_w_v, rwkv_w_o, rwkv_w0, rwkv_w1, rwkv_w2,
           rwkv_a0, rwkv_a1, rwkv_a2, rwkv_g1, rwkv_g2, rwkv_k_k, rwkv_k_a, rwkv_r_k,
           rwkv_lnx_g, rwkv_lnx_b):
    bp = x_prompt.shape[0]
    ns = x_sample.shape[0]

    mods_s, mods_p = _ada_mods(c_sample, c_prompt, ada_w, ada_b)
    mods_p = mods_p.reshape(DEPTH, bp, 1, N_MODS * D_MODEL)

    ffn1 = _prep_ffn(ffn1_wg, ffn1_wu, ffn1_wd)
    ffn2 = _prep_ffn(ffn2_wg, ffn2_wu, ffn2_wd)
    mix = _prep_mix(w_in, w_out, gla_w_gk, gla_b_gk, gla_norm_g, s5_w_glu, s5_b_glu)
    s5 = _s5_prepare(s5_a_re, s5_a_im, s5_log_step, s5_b_re, s5_b_im, s5_c_re, s5_c_im, s5_d)
    rwkv = _prep_rwkv(rwkv_mu, rwkv_w_r, rwkv_w_k, rwkv_w_v, rwkv_w_o, rwkv_w0, rwkv_w1, rwkv_w2,
                      rwkv_a0, rwkv_a1, rwkv_a2, rwkv_g1, rwkv_g2, rwkv_k_k, rwkv_k_a, rwkv_r_k,
                      rwkv_lnx_g, rwkv_lnx_b)

    def lnp(layer, idx):
        return _row(ln_g[layer, idx]), _row(ln_b[layer, idx])

    x = x_prompt
    xs = x_sample.reshape(ns, D_MODEL)
    x, xs = _ffn(x, xs, mods_p, mods_s, 0, 0, *ffn1, *lnp(0, 0))
    x, gla_p, s5_re_p, s5_im_p = _mix_prompt(x, mods_p, 0, mix, s5, *lnp(0, 1))
    xs, gla_s, s5_re_s, s5_im_s = _mix_sample(xs, mods_s, 0, state_gla, state_s5_re, state_s5_im,
                                              mix, s5, *lnp(0, 1))
    x, xs = _ffn(x, xs, mods_p, mods_s, 0, 6, *ffn2, *lnp(0, 2))
    x, xs = _ffn(x, xs, mods_p, mods_s, 1, 0, *ffn1, *lnp(1, 0))
    x, shift_p, wkv_p = _rwkv_prompt(x, mods_p, 1, rwkv, *lnp(1, 1))
    xs, shift_s, wkv_s = _rwkv_sample(xs, mods_s, 1, state_rwkv_shift, state_rwkv_wkv, rwkv, *lnp(1, 1))
    y_prompt, xs = _ffn(x, xs, mods_p, mods_s, 1, 6, *ffn2, *lnp(1, 2))
    y_sample = xs.reshape(ns, 1, D_MODEL)

    return (y_prompt, y_sample, gla_p, s5_re_p, s5_im_p, shift_p, wkv_p,
            gla_s, s5_re_s, s5_im_s, shift_s, wkv_s)
```

```python
import functools
import math

import jax
import jax.numpy as jnp
from jax import lax
from jax.experimental import pallas as pl
from jax.experimental.pallas import tpu as pltpu

F32 = jnp.float32
BF16 = jnp.bfloat16

D_MODEL = 1024
DEPTH = 2
DN_ALPHA = (2 * DEPTH) ** 0.25
LN_EPS = 1e-5
RMS_EPS = 1e-5
FFN_RES = 0.5
D_FF = 2752
N_MODS = 9

GLA_HEADS = 4
GLA_DK = 64
GLA_DV = 128
GLA_KEY = GLA_HEADS * GLA_DK
GLA_VAL = GLA_HEADS * GLA_DV
GLA_GATE_RANK = 16
GLA_GATE_NORM = 16.0

S5_GROUP = 16
S5_WIDTH = 512
S5_GROUPS = S5_WIDTH // S5_GROUP
S5_STATE = 64
S5_CH = S5_GROUPS * S5_STATE

RWKV_HEAD = 64
RWKV_HEADS = D_MODEL // RWKV_HEAD
RWKV_LNX_EPS = 64e-5
NORM_EPS = 1e-12

LANES = 128
SUBLANES = 8
MXU_DIM = 256
VMEM_LIMIT = 58 * 1024 * 1024

CHUNK = 64
HEADS_PER_GROUP = MXU_DIM // RWKV_HEAD
N_GROUPS = RWKV_HEADS // HEADS_PER_GROUP
FF_CHUNKS = ((0, 1536), (1536, D_FF))
LORA_PAD = 128
GATE_LORA_PAD = 256
PERM_STRIDE = CHUNK + SUBLANES
SAMPLE_BLOCK = 8


def _dot(a, b):
    return jnp.dot(a, b, preferred_element_type=F32)


def _dot_nt(a, b):
    return lax.dot_general(a, b, (((1,), (1,)), ((), ())), preferred_element_type=F32)


def _bdot(a, w_ref_or_val):
    return _dot(a.astype(BF16), w_ref_or_val)


def _sigmoid(x):
    return 1.0 / (1.0 + jnp.exp(-x))


def _silu(x):
    return x * _sigmoid(x)


def _softplus(x):
    return jnp.maximum(x, 0.0) + jnp.log(1.0 + jnp.exp(-jnp.abs(x)))


def _gelu_tanh(x):
    c = math.sqrt(2.0 / math.pi)
    return 0.5 * x * (1.0 + jnp.tanh(c * (x + 0.044715 * (x * x * x))))


def _layer_norm(x, g, b):
    mu = jnp.mean(x, axis=-1, keepdims=True)
    d = x - mu
    var = jnp.mean(d * d, axis=-1, keepdims=True)
    return d * lax.rsqrt(var + LN_EPS) * g + b


def _split3(x):
    hi = x.astype(BF16)
    r1 = x - hi.astype(F32)
    mid = r1.astype(BF16)
    lo = (r1 - mid.astype(F32)).astype(BF16)
    return hi, mid, lo


def _exact_dot_left01(m01, x):
    hi, mid, lo = _split3(x)
    return _dot(m01, hi) + _dot(m01, mid) + _dot(m01, lo)


def _dot_right01(x, m01):
    hi = x.astype(BF16)
    lo = (x - hi.astype(F32)).astype(BF16)
    return _dot(hi, m01) + _dot(lo, m01)


def _seg_ones(seg):
    r = lax.broadcasted_iota(jnp.int32, (MXU_DIM, MXU_DIM), 0) // seg
    c = lax.broadcasted_iota(jnp.int32, (MXU_DIM, MXU_DIM), 1) // seg
    return jnp.where(r == c, 1.0, 0.0).astype(BF16)


def _seg_sum(x, seg, split=False):
    ones = _seg_ones(seg)
    tiles = []
    for i in range(x.shape[1] // MXU_DIM):
        xt = x[:, i * MXU_DIM:(i + 1) * MXU_DIM]
        tiles.append(_dot_right01(xt, ones) if split else _dot(xt.astype(BF16), ones))
    return jnp.concatenate(tiles, axis=1)


def _tril_ones(n):
    r = lax.broadcasted_iota(jnp.int32, (n, n), 0)
    c = lax.broadcasted_iota(jnp.int32, (n, n), 1)
    return jnp.where(r >= c, 1.0, 0.0).astype(BF16)


def _chunk_cumsum(x, n_batch):
    tri = _tril_ones(CHUNK)
    parts = [_exact_dot_left01(tri, x[b * CHUNK:(b + 1) * CHUNK, :]) for b in range(n_batch)]
    return jnp.concatenate(parts, axis=0)


def _head_stack(z, n_heads, width):
    head = lax.broadcasted_iota(jnp.int32, z.shape, 1) // width
    return jnp.concatenate([jnp.where(head == h, z, 0.0) for h in range(n_heads)], axis=0)


def _block_diag_mask(rows, cols, rblk, cblk):
    r = lax.broadcasted_iota(jnp.int32, (rows, cols), 0) // rblk
    c = lax.broadcasted_iota(jnp.int32, (rows, cols), 1) // cblk
    return r == c


def _full_spec(shape, grid_rank):
    zeros = (0,) * len(shape)
    if grid_rank == 1:
        return pl.BlockSpec(shape, lambda i: zeros, pipeline_mode=pl.Buffered(1))
    return pl.BlockSpec(shape, lambda i, j: zeros, pipeline_mode=pl.Buffered(1))


def _params(semantics):
    return pltpu.CompilerParams(dimension_semantics=semantics, vmem_limit_bytes=VMEM_LIMIT)


ADA_TN = 2304


def _ada_body(cs_ref, cp_ref, w_ref, b_ref, os_ref, op_ref):
    w = w_ref[...].astype(BF16)
    os_ref[...] = _bdot(_silu(cs_ref[...]), w) + b_ref[...]
    op_ref[...] = _bdot(_silu(cp_ref[...]), w) + b_ref[...]


def _ada_mods(c_sample, c_prompt, ada_w, ada_b):
    ns, bp = c_sample.shape[0], c_prompt.shape[0]
    width = N_MODS * D_MODEL
    return pl.pallas_call(
        _ada_body,
        grid=(DEPTH, width // ADA_TN),
        in_specs=[
            pl.BlockSpec((ns, D_MODEL), lambda l, j: (0, 0)),
            pl.BlockSpec((bp, D_MODEL), lambda l, j: (0, 0)),
            pl.BlockSpec((None, D_MODEL, ADA_TN), lambda l, j: (l, 0, j)),
            pl.BlockSpec((None, 1, ADA_TN), lambda l, j: (l, 0, j)),
        ],
        out_specs=[pl.BlockSpec((None, ns, ADA_TN), lambda l, j: (l, 0, j)),
                   pl.BlockSpec((None, bp, ADA_TN), lambda l, j: (l, 0, j))],
        out_shape=[jax.ShapeDtypeStruct((DEPTH, ns, width), F32),
                   jax.ShapeDtypeStruct((DEPTH, bp, width), F32)],
        compiler_params=_params(("arbitrary", "arbitrary")),
        name="ada_mods",
    )(c_sample, c_prompt, ada_w, ada_b.reshape(DEPTH, 1, width))


FFN_TM = 1024


def _ffn_tile(x, sh, sc, gt, wg_ref, wu_ref, wd_ref, g_ref, b_ref):
    h = (x * (1.0 + sc) + sh).astype(BF16)
    acc = None
    for lo, hi in FF_CHUNKS:
        g = _dot_nt(h, wg_ref[lo:hi, :])
        u = _dot_nt(h, wu_ref[lo:hi, :])
        y = _bdot(_silu(g) * u, wd_ref[lo:hi, :])
        acc = y if acc is None else acc + y
    r = FFN_RES * (1.0 + gt) * acc
    return _layer_norm(DN_ALPHA * x + r, g_ref[...], b_ref[...])


def _ffn_body(x_ref, sh_ref, sc_ref, gt_ref, xs_ref, shs_ref, scs_ref, gts_ref,
              wg_ref, wu_ref, wd_ref, g_ref, b_ref, o_ref, os_ref):
    last = pl.num_programs(0) - 1

    @pl.when(pl.program_id(0) < last)
    def _():
        o_ref[...] = _ffn_tile(x_ref[...], sh_ref[...], sc_ref[...], gt_ref[...],
                               wg_ref, wu_ref, wd_ref, g_ref, b_ref)

    @pl.when(pl.program_id(0) == last)
    def _():
        os_ref[...] = _ffn_tile(xs_ref[...], shs_ref[...], scs_ref[...], gts_ref[...],
                                wg_ref, wu_ref, wd_ref, g_ref, b_ref)


def _mod_specs_sample(layer, first, rows):
    return [
        pl.BlockSpec((None, rows, D_MODEL), functools.partial(
            lambda i, col: (layer, 0, col), col=first + k))
        for k in range(3)
    ]


def _ffn(x, xs, mods_p, mods_s, layer, first, wg, wu, wd, ln_g, ln_b):
    bsz, seq, _ = x.shape
    n = xs.shape[0]
    per_b = seq // FFN_TM
    n_tiles = bsz * per_b

    def tile_idx(t):
        tc = jnp.minimum(t, n_tiles - 1)
        return tc // per_b, tc % per_b

    mod_specs = [
        pl.BlockSpec((None, None, 1, D_MODEL), functools.partial(
            lambda t, col: (layer, tile_idx(t)[0], 0, col), col=first + k))
        for k in range(3)
    ]
    w_spec = pl.BlockSpec((None, D_FF, D_MODEL), lambda t: (layer, 0, 0), pipeline_mode=pl.Buffered(1))
    tile = pl.BlockSpec((None, FFN_TM, D_MODEL), lambda t: (*tile_idx(t), 0))
    return pl.pallas_call(
        _ffn_body,
        grid=(n_tiles + 1,),
        in_specs=[tile] + mod_specs + [_full_spec((n, D_MODEL), 1)] + _mod_specs_sample(layer, first, n)
        + [w_spec] * 3 + [_full_spec((1, D_MODEL), 1)] * 2,
        out_specs=[tile, _full_spec((n, D_MODEL), 1)],
        out_shape=[jax.ShapeDtypeStruct(x.shape, F32), jax.ShapeDtypeStruct(xs.shape, F32)],
        compiler_params=_params(("arbitrary",)),
        name="ffn",
    )(x, mods_p, mods_p, mods_p, xs, mods_s, mods_s, mods_s, wg, wu, wd, ln_g, ln_b)


def _col_groups(width):
    return [slice(lo, lo + width) for lo in range(0, D_MODEL, width)]


def _rwkv_mixes(h, prev, mu_ref):
    xx = prev - h
    mu = mu_ref[...]
    return [(h + xx * mu[i:i + 1, :]).astype(BF16) for i in range(6)]


def _rwkv_lora_in(mixes, w1, a1):
    return jnp.tanh(_dot(mixes[1], w1[...])).astype(BF16), _dot(mixes[4], a1[...]).astype(BF16)


def _rwkv_project_cols(cols, mixes, tw, ta, wr, wk, wv, w0, w2, a0, a2, kk_ref, ka_ref):
    r = _dot(mixes[0], wr[:, cols])
    k = _dot(mixes[2], wk[:, cols])
    v = _dot(mixes[3], wv[:, cols])
    wl = w0[:, cols] + _dot(tw, w2[:, cols])
    log_decay = -math.exp(-0.5) * _sigmoid(wl)
    a_sig = _sigmoid(a0[:, cols] + _dot(ta, a2[:, cols]))
    kk = k * kk_ref[:, cols]
    norm = jnp.sqrt(_seg_sum(kk * kk, RWKV_HEAD))
    kk = kk / jnp.maximum(norm, NORM_EPS)
    k2 = k * (1.0 + (a_sig - 1.0) * ka_ref[:, cols])
    return r, log_decay, k2, v, -kk, kk * a_sig


def _rwkv_bonus(r, k2, v, rk):
    return _seg_sum(r * k2 * rk, RWKV_HEAD) * v


def _rwkv_gate_in(xg, g1):
    return _sigmoid(_dot(xg, g1[...])).astype(BF16)


def _rwkv_out_cols(cols, y, bonus, sg, g2, wo, lng, lnb):
    gate = _dot(sg, g2[:, cols])
    inv_n = 1.0 / RWKV_HEAD
    mu_y = _seg_sum(y, RWKV_HEAD, split=True) * inv_n
    d = y - mu_y
    var = _seg_sum(d * d, RWKV_HEAD) * inv_n
    y = d * lax.rsqrt(var + RWKV_LNX_EPS) * lng[:, cols] + lnb[:, cols]
    return _bdot((y + bonus) * gate, wo[cols, :])


def _rwkv_weight_list(p):
    return [p["mu"], p["wr"], p["wk"], p["wv"], p["w0"], p["w1"], p["w2"], p["a0"], p["a1"], p["a2"],
            p["k_k"], p["k_a"], p["r_k"]]


def _rwkv_out_weight_list(p, ln_g, ln_b):
    return [p["g1"], p["g2"], p["wo"], p["lnx_g"], p["lnx_b"], ln_g, ln_b]


NEUMANN_LEVELS = int(math.log2(CHUNK)) - 1


def _wkv_chunks(problems):
    c = CHUNK
    hp = HEADS_PER_GROUP
    w = hp * RWKV_HEAD
    n = hp * c
    t_idx = lax.broadcasted_iota(jnp.int32, (c, w), 0)
    s_idx = lax.broadcasted_iota(jnp.int32, (c, w), 1) % c
    strict = t_idx > s_idx
    incl = t_idx >= s_idx
    eye = jnp.where(t_idx == s_idx, 1.0, 0.0)
    bd_mask = _block_diag_mask(w, w, RWKV_HEAD, RWKV_HEAD)
    idx = range(len(problems))
    At, Rt, Kt, Bt, V, S0, dec = zip(*problems)

    def bf(x):
        return x.astype(BF16)

    def stack(x, width):
        return bf(_head_stack(x, hp, width))

    X2 = [jnp.concatenate([At[i], Rt[i]], axis=0) for i in idx]
    KB = [jnp.concatenate([stack(Kt[i], RWKV_HEAD), stack(Bt[i], RWKV_HEAD)], axis=0) for i in idx]
    sc = [_dot_nt(X2[i], KB[i]) for i in idx]
    XS = [_dot_nt(X2[i], bf(S0[i])) for i in idx]
    Vbd = [stack(V[i], RWKV_HEAD) for i in idx]
    Aab = [jnp.where(strict, sc[i][:c, n:], 0.0) for i in idx]
    rhs = [XS[i][:c] + _dot(bf(jnp.where(strict, sc[i][:c, :n], 0.0)), Vbd[i]) for i in idx]

    Q = Aab
    P = [eye + Aab[i] for i in idx]
    Qbd = [stack(Q[i], c) for i in idx]
    for _ in range(NEUMANN_LEVELS):
        Q = [_dot(bf(Q[i]), Qbd[i]) for i in idx]
        Qbd = [stack(Q[i], c) for i in idx]
        P = [P[i] + _dot(bf(P[i]), Qbd[i]) for i in idx]
    U = [_dot(bf(P[i]), stack(rhs[i], RWKV_HEAD)) for i in idx]

    Y = [XS[i][c:] + _dot(bf(jnp.where(incl, sc[i][c:, :n], 0.0)), Vbd[i])
         + _dot(bf(jnp.where(incl, sc[i][c:, n:], 0.0)), stack(U[i], RWKV_HEAD)) for i in idx]

    upd = [_dot(bf(jnp.concatenate([V[i], U[i]], axis=0).T),
                bf(jnp.concatenate([Kt[i], Bt[i]], axis=0))) for i in idx]
    S1 = [(S0[i] + jnp.where(bd_mask, upd[i], 0.0)) * dec[i] for i in idx]
    return Y, S1


def _rwkv_prompt_body(x_ref, sh_ref, sc_ref, gt_ref,
                      mu, wr, wk, wv, w0, w1, w2, a0, a1, a2, kk_ref, ka_ref, rk_ref,
                      g1, g2, wo, lng, lnb, ln_g, ln_b,
                      o_ref, shift_ref, state_ref,
                      carry, s_at, s_rt, s_kt, s_bt, s_v, s_dec, s_bonus, s_sg):
    nb = x_ref.shape[0]
    rows = nb * CHUNK
    gw = HEADS_PER_GROUP * RWKV_HEAD
    step = pl.program_id(0)

    @pl.when(step == 0)
    def _():
        carry[...] = jnp.zeros(carry.shape, F32)
        state_ref[...] = jnp.zeros(state_ref.shape, F32)

    h3 = x_ref[...] * (1.0 + sc_ref[...]) + sh_ref[...]
    h = h3.reshape(rows, D_MODEL)
    first = lax.broadcasted_iota(jnp.int32, (rows, D_MODEL), 0) % CHUNK == 0
    carried = jnp.broadcast_to(carry[...], (nb, CHUNK, D_MODEL)).reshape(rows, D_MODEL)
    prev = jnp.where(first, carried, pltpu.roll(h, 1, axis=0))
    carry[...] = h3[:, CHUNK - 1:CHUNK, :]
    shift_ref[...] = h3[:, CHUNK - 1:CHUNK, :]

    mixes = _rwkv_mixes(h, prev, mu)
    tw, ta = _rwkv_lora_in(mixes, w1, a1)
    s_sg[...] = _rwkv_gate_in(mixes[5], g1)
    groups = _col_groups(gw)

    def project(gi):
        cols = groups[gi]
        r, lw, k2, v, av, bv = _rwkv_project_cols(
            cols, mixes, tw, ta, wr, wk, wv, w0, w2, a0, a2, kk_ref, ka_ref)
        s_bonus[:, cols] = _rwkv_bonus(r, k2, v, rk_ref[:, cols])
        g = _chunk_cumsum(lw, nb)
        e_neg = jnp.exp(-g)
        s_at[gi] = (av * jnp.exp(g - lw)).astype(BF16)
        s_rt[gi] = (r * jnp.exp(g)).astype(BF16)
        s_kt[gi] = (k2 * e_neg).astype(BF16)
        s_bt[gi] = (bv * e_neg).astype(BF16)
        s_v[gi] = v.astype(BF16)
        s_dec[gi] = jnp.exp(g.reshape(nb, CHUNK, gw)[:, CHUNK - 1:CHUNK, :])

    def scan(gi):
        sls = [slice(b * CHUNK, (b + 1) * CHUNK) for b in range(nb)]
        Y, S1 = _wkv_chunks([
            (s_at[gi, sl, :], s_rt[gi, sl, :], s_kt[gi, sl, :].astype(F32), s_bt[gi, sl, :].astype(F32),
             s_v[gi, sl, :].astype(F32), state_ref[b, gi], s_dec[gi, b])
            for b, sl in enumerate(sls)])
        for b in range(nb):
            state_ref[b, gi] = S1[b]
        return jnp.concatenate(Y, axis=0)

    project(0)
    out = None
    for gi in range(N_GROUPS):
        y = scan(gi)
        if gi + 1 < N_GROUPS:
            project(gi + 1)
        part = _rwkv_out_cols(groups[gi], y, s_bonus[:, groups[gi]], s_sg[...], g2, wo, lng, lnb)
        out = part if out is None else out + part
    gt = jnp.broadcast_to(gt_ref[...], (nb, CHUNK, D_MODEL)).reshape(rows, D_MODEL)
    x = x_ref[...].reshape(rows, D_MODEL)
    o_ref[...] = _layer_norm(DN_ALPHA * x + (1.0 + gt) * out,
                             ln_g[...], ln_b[...]).reshape(nb, CHUNK, D_MODEL)


def _rwkv_prompt(x, mods_p, layer, p, ln_g, ln_b):
    bsz, seq, _ = x.shape
    rows = bsz * CHUNK
    gw = HEADS_PER_GROUP * RWKV_HEAD
    weights = _rwkv_weight_list(p) + _rwkv_out_weight_list(p, ln_g, ln_b)
    mod_specs = [
        pl.BlockSpec((None, bsz, 1, D_MODEL), functools.partial(lambda i, col: (layer, 0, 0, col), col=3 + k))
        for k in range(3)
    ]
    slab16 = pltpu.VMEM((N_GROUPS, rows, gw), BF16)
    out, shift, state = pl.pallas_call(
        _rwkv_prompt_body,
        grid=(seq // CHUNK,),
        in_specs=[pl.BlockSpec((bsz, CHUNK, D_MODEL), lambda i: (0, i, 0))] + mod_specs
        + [_full_spec(w.shape, 1) for w in weights],
        out_specs=[
            pl.BlockSpec((bsz, CHUNK, D_MODEL), lambda i: (0, i, 0)),
            _full_spec((bsz, 1, D_MODEL), 1),
            _full_spec((bsz, N_GROUPS, gw, gw), 1),
        ],
        out_shape=[
            jax.ShapeDtypeStruct(x.shape, F32),
            jax.ShapeDtypeStruct((bsz, 1, D_MODEL), F32),
            jax.ShapeDtypeStruct((bsz, N_GROUPS, gw, gw), F32),
        ],
        scratch_shapes=[pltpu.VMEM((bsz, 1, D_MODEL), F32)] + [slab16] * 5
        + [pltpu.VMEM((N_GROUPS, bsz, 1, gw), F32),
           pltpu.VMEM((rows, D_MODEL), F32), pltpu.VMEM((rows, GATE_LORA_PAD), BF16)],
        compiler_params=_params(("arbitrary",)),
        name="rwkv_prompt",
    )(x, mods_p, mods_p, mods_p, *weights)
    s5d = state.reshape(bsz, N_GROUPS, HEADS_PER_GROUP, RWKV_HEAD, gw)
    wkv = jnp.stack([s5d[:, :, hh, :, hh * RWKV_HEAD:(hh + 1) * RWKV_HEAD]
                     for hh in range(HEADS_PER_GROUP)], axis=2)
    return out, shift.reshape(bsz, D_MODEL), wkv.reshape(bsz, RWKV_HEADS, RWKV_HEAD, RWKV_HEAD)


def _rwkv_sample_pre_body(x_ref, sh_ref, sc_ref, prev_ref,
                          mu, wr, wk, wv, w0, w1, w2, a0, a1, a2, kk_ref, ka_ref, rk_ref,
                          h_ref, rt_ref, wt_ref, kt_ref, vt_ref, at_ref, bt_ref, xg_ref, bonus_ref):
    h = x_ref[...] * (1.0 + sc_ref[...]) + sh_ref[...]
    h_ref[...] = h
    mixes = _rwkv_mixes(h, prev_ref[...], mu)
    xg_ref[...] = mixes[5]
    tw, ta = _rwkv_lora_in(mixes, w1, a1)
    for cols in _col_groups(MXU_DIM):
        r, lw, k2, v, av, bv = _rwkv_project_cols(
            cols, mixes, tw, ta, wr, wk, wv, w0, w2, a0, a2, kk_ref, ka_ref)
        bonus_ref[:, cols] = _rwkv_bonus(r, k2, v, rk_ref[:, cols])
        rt_ref[cols, :] = r.T
        wt_ref[cols, :] = jnp.exp(lw).T
        kt_ref[cols, :] = k2.T
        vt_ref[cols, :] = v.T
        at_ref[cols, :] = av.T
        bt_ref[cols, :] = bv.T


def _rwkv_sample_step_body(s_ref, r_ref, w_ref, k_ref, v_ref, a_ref, b_ref, so_ref, y_ref):
    a, w, b, k, r = a_ref[...], w_ref[...], b_ref[...], k_ref[...], r_ref[...]

    def row_step(i, carry_val):
        S = s_ref[i]
        sa = jnp.sum(S * a, axis=0, keepdims=True)
        S1 = S * w + sa * b + v_ref[pl.ds(i, 1), :] * k
        so_ref[i] = S1
        y_ref[pl.ds(i, 1), :] = jnp.sum(S1 * r, axis=0, keepdims=True)
        return carry_val

    lax.fori_loop(0, RWKV_HEAD, row_step, 0, unroll=4)


def _rwkv_sample_post_body(yt_ref, bonus_ref, xg_ref, x_ref, gt_ref,
                           g1, g2, wo, lng, lnb, ln_g, ln_b, o_ref):
    sg = _rwkv_gate_in(xg_ref[...], g1)
    out = None
    for cols in _col_groups(MXU_DIM):
        part = _rwkv_out_cols(cols, yt_ref[cols, :].T, bonus_ref[:, cols], sg, g2, wo, lng, lnb)
        out = part if out is None else out + part
    o_ref[...] = _layer_norm(DN_ALPHA * x_ref[...] + (1.0 + gt_ref[...]) * out, ln_g[...], ln_b[...])


def _to_columns(a):
    n, f = a.shape
    return a.reshape(n // SAMPLE_BLOCK, SAMPLE_BLOCK, f).transpose(0, 2, 1)


def _rwkv_sample(x, mods_s, layer, s_shift, s_wkv, p, ln_g, ln_b):
    n = x.shape[0]
    row = jax.ShapeDtypeStruct((n, D_MODEL), F32)
    row_spec = _full_spec((n, D_MODEL), 1)
    pre_w = _rwkv_weight_list(p)
    ms = _mod_specs_sample(layer, 3, n)
    col = jax.ShapeDtypeStruct((D_MODEL, n), F32)
    col_spec = _full_spec((D_MODEL, n), 1)
    h, *vecs, xg, bonus = pl.pallas_call(
        _rwkv_sample_pre_body,
        grid=(1,),
        in_specs=[row_spec, ms[0], ms[1], row_spec] + [_full_spec(a.shape, 1) for a in pre_w],
        out_specs=[row_spec] + [col_spec] * 6 + [row_spec] * 2,
        out_shape=[row] + [col] * 6 + [jax.ShapeDtypeStruct((n, D_MODEL), BF16), row],
        compiler_params=_params(("arbitrary",)),
        name="rwkv_sample_pre",
    )(x, mods_s, mods_s, s_shift, *pre_w)

    st_spec = pl.BlockSpec((None, RWKV_HEAD, RWKV_HEAD, n), lambda hd: (hd, 0, 0, 0))
    vec_spec = pl.BlockSpec((RWKV_HEAD, n), lambda hd: (hd, 0))
    st_new, yt = pl.pallas_call(
        _rwkv_sample_step_body,
        grid=(RWKV_HEADS,),
        in_specs=[st_spec] + [vec_spec] * len(vecs),
        out_specs=[st_spec, vec_spec],
        out_shape=[jax.ShapeDtypeStruct((RWKV_HEADS, RWKV_HEAD, RWKV_HEAD, n), F32), col],
        compiler_params=_params(("arbitrary",)),
        name="rwkv_sample_step",
    )(jnp.transpose(s_wkv, (1, 2, 3, 0)), *vecs)
    s_new = jnp.transpose(st_new, (3, 0, 1, 2))

    post_w = _rwkv_out_weight_list(p, ln_g, ln_b)
    out = pl.pallas_call(
        _rwkv_sample_post_body,
        grid=(1,),
        in_specs=[col_spec] + [row_spec] * 3 + [ms[2]] + [_full_spec(a.shape, 1) for a in post_w],
        out_specs=row_spec,
        out_shape=row,
        compiler_params=_params(("arbitrary",)),
        name="rwkv_sample_post",
    )(yt, bonus, xg, x, mods_s, *post_w)
    return out, h, s_new


S5_TILE_GROUPS = LANES // S5_GROUP
S5_TILES = S5_GROUPS // S5_TILE_GROUPS


def _s5_prep_body(are_ref, aim_ref, ls_ref, bre_ref, bim_ref, cre_ref, cim_ref,
                  abre_ref, abim_ref, bbre_ref, bbim_ref, cdre_ref, cdim_ref):
    a_re = are_ref[...]
    a_im = aim_ref[...]
    dt = jnp.exp(ls_ref[...])
    mag = jnp.exp(a_re * dt)
    ab_re = mag * jnp.cos(a_im * dt)
    ab_im = mag * jnp.sin(a_im * dt)
    den = a_re * a_re + a_im * a_im
    nr = ab_re - 1.0
    z_re = (nr * a_re + ab_im * a_im) / den
    z_im = (ab_im * a_re - nr * a_im) / den
    b_re = bre_ref[...]
    b_im = bim_ref[...]
    abre_ref[...] = ab_re
    abim_ref[...] = ab_im
    bb_re = (z_re * b_re - z_im * b_im).astype(BF16)
    bb_im = (z_re * b_im + z_im * b_re).astype(BF16)
    for ref in (bbre_ref, bbim_ref, cdre_ref, cdim_ref):
        ref[...] = jnp.zeros(ref.shape, BF16)
    c, p = S5_GROUP, S5_STATE
    for g in range(S5_GROUPS):
        k, j = divmod(g, S5_TILE_GROUPS)
        bbre_ref[k, j * c:(j + 1) * c, j * p:(j + 1) * p] = bb_re[g]
        bbim_ref[k, j * c:(j + 1) * c, j * p:(j + 1) * p] = bb_im[g]
        cdre_ref[k, j * p:(j + 1) * p, j * c:(j + 1) * c] = cre_ref[g].astype(BF16)
        cdim_ref[k, j * p:(j + 1) * p, j * c:(j + 1) * c] = cim_ref[g].astype(BF16)


def _s5_prepare(s5_a_re, s5_a_im, s5_log_step, s5_b_re, s5_b_im, s5_c_re, s5_c_im, s5_d):
    G, P, C = S5_GROUPS, S5_STATE, S5_GROUP
    small = jax.ShapeDtypeStruct((G, 1, P), F32)
    bb_shape = (S5_TILES, LANES, S5_TILE_GROUPS * P)
    cd_shape = (S5_TILES, S5_TILE_GROUPS * P, LANES)
    ls = jnp.broadcast_to(s5_log_step.reshape(G, 1, 1), (G, 1, P))
    ab_re, ab_im, bb_re, bb_im, cd_re, cd_im = pl.pallas_call(
        _s5_prep_body,
        grid=(1,),
        in_specs=[_full_spec((G, 1, P), 1)] * 3 + [_full_spec((G, C, P), 1)] * 2
        + [_full_spec((G, P, C), 1)] * 2,
        out_specs=[_full_spec((G, 1, P), 1)] * 2 + [_full_spec(bb_shape, 1)] * 2
        + [_full_spec(cd_shape, 1)] * 2,
        out_shape=[small, small] + [jax.ShapeDtypeStruct(bb_shape, BF16)] * 2
        + [jax.ShapeDtypeStruct(cd_shape, BF16)] * 2,
        name="s5_prepare",
    )(s5_a_re.reshape(G, 1, P), s5_a_im.reshape(G, 1, P), ls,
      s5_b_re.transpose(0, 2, 1), s5_b_im.transpose(0, 2, 1),
      s5_c_re.transpose(0, 2, 1), s5_c_im.transpose(0, 2, 1))
    return dict(
        ab_re=ab_re.reshape(1, G * P), ab_im=ab_im.reshape(1, G * P),
        bb_re=bb_re, bb_im=bb_im, c_re=cd_re, c_im=cd_im, d=s5_d.reshape(1, G * C),
    )


def _s5_input(u, bb_re_ref, bb_im_ref):
    n_tiles = bb_re_ref.shape[0]
    ub = u.astype(BF16)
    re = [_dot(ub[:, k * LANES:(k + 1) * LANES], bb_re_ref[k]) for k in range(n_tiles)]
    im = [_dot(ub[:, k * LANES:(k + 1) * LANES], bb_im_ref[k]) for k in range(n_tiles)]
    return jnp.concatenate(re, axis=1), jnp.concatenate(im, axis=1)


def _s5_output(h_re, h_im, c_re_ref, c_im_ref):
    n_tiles = c_re_ref.shape[0]
    w = h_re.shape[1] // n_tiles
    hr = h_re.astype(BF16)
    hi = h_im.astype(BF16)
    ys = [_dot(hr[:, k * w:(k + 1) * w], c_re_ref[k]) - _dot(hi[:, k * w:(k + 1) * w], c_im_ref[k])
          for k in range(n_tiles)]
    return jnp.concatenate(ys, axis=1)


C_Q, C_K, C_V, C_G, C_U, C_GK, C_END = 0, 256, 512, 1024, 1536, 2048, 2176


def _mix_project(h, w_in, w_gk, b_gk):
    p = _bdot(h, w_in[...])
    q = p[:, C_Q:C_K] * (GLA_DK ** -0.5)
    k = p[:, C_K:C_V]
    v = p[:, C_V:C_G]
    gg = p[:, C_G:C_U]
    u = p[:, C_U:C_GK]
    z = _bdot(p[:, C_GK:C_END], w_gk[...]) + b_gk[...]
    gk = -_softplus(-z) * (1.0 / GLA_GATE_NORM)
    return q, k, v, gg, u, gk


def _gla_finish(o, gg, norm_g):
    parts = []
    for hh in range(GLA_HEADS):
        oh = o[:, hh * GLA_DV:(hh + 1) * GLA_DV]
        parts.append(oh * lax.rsqrt(jnp.mean(oh * oh, axis=-1, keepdims=True) + RMS_EPS) * norm_g[...])
    return jnp.concatenate(parts, axis=1) * _silu(gg)


def _mix_output(o_gla, y_s5, x, gt, w_glu, b_glu, w_out, ln_g, ln_b):
    z = _gelu_tanh(y_s5)
    o_s5 = z * _sigmoid(_bdot(z, w_glu[...]) + b_glu[...])
    out = _bdot(o_gla, w_out[0:GLA_VAL, :]) + _bdot(o_s5, w_out[GLA_VAL:GLA_VAL + S5_WIDTH, :])
    return _layer_norm(DN_ALPHA * x + (1.0 + gt) * out, ln_g[...], ln_b[...])


def _gla_chunks(problems):
    c = CHUNK
    mid = c // 2 - 1
    idx = range(len(problems))
    q, k, v, bc, ST = zip(*problems)
    t_idx = lax.broadcasted_iota(jnp.int32, (c, GLA_KEY), 0)
    s_idx = lax.broadcasted_iota(jnp.int32, (c, GLA_KEY), 1) % c
    causal = t_idx >= s_idx
    bd_mask = _block_diag_mask(GLA_VAL, GLA_KEY, GLA_DV, GLA_DK)

    def bf(x):
        return x.astype(BF16)

    b_mid = [bc[i][mid:mid + 1, :] for i in idx]
    b_last = [bc[i][c - 1:c, :] for i in idx]
    q_in = [bf(q[i] * jnp.exp(bc[i] - b_mid[i])) for i in idx]
    k_in = [bf(_head_stack(k[i] * jnp.exp(b_mid[i] - bc[i]), GLA_HEADS, GLA_DK)) for i in idx]
    scores = [jnp.where(causal, _dot_nt(q_in[i], k_in[i]), 0.0) for i in idx]
    o_inter = [_dot_nt(bf(q[i] * jnp.exp(bc[i])), bf(ST[i])) for i in idx]
    upd = [_dot(bf(v[i].T), bf(k[i] * jnp.exp(b_last[i] - bc[i]))) for i in idx]
    o = [o_inter[i] + _dot(bf(scores[i]), bf(_head_stack(v[i], GLA_HEADS, GLA_DV))) for i in idx]
    ST1 = [ST[i] * jnp.exp(b_last[i]) + jnp.where(bd_mask, upd[i], 0.0) for i in idx]
    return o, ST1


def _mix_prompt_body(x_ref, sh_ref, sc_ref, gt_ref,
                     w_in, w_gk, b_gk, norm_g, ab_re_ref, ab_im_ref, bb_re, bb_im, c_re, c_im, d_ref,
                     w_glu, b_glu, w_out, ln_g, ln_b,
                     o_ref, gla_ref, hre_ref, him_ref,
                     s_q, s_k, s_v, s_bc, s_perm, s_re, s_im):
    nb = x_ref.shape[0]
    rows = nb * CHUNK
    step = pl.program_id(0)

    @pl.when(step == 0)
    def _():
        gla_ref[...] = jnp.zeros(gla_ref.shape, F32)
        hre_ref[...] = jnp.zeros(hre_ref.shape, F32)
        him_ref[...] = jnp.zeros(him_ref.shape, F32)

    x3 = x_ref[...]
    h3 = x3 * (1.0 + sc_ref[...]) + sh_ref[...]
    x = x3.reshape(rows, D_MODEL)
    h = h3.reshape(rows, D_MODEL)
    q, k, v, gg, u, gk = _mix_project(h, w_in, w_gk, b_gk)

    s_q[...] = q
    s_k[...] = k
    s_v[...] = v
    s_bc[...] = _chunk_cumsum(gk, nb)

    sls = [slice(b * CHUNK, (b + 1) * CHUNK) for b in range(nb)]
    o, ST1 = _gla_chunks([(s_q[sl, :], s_k[sl, :], s_v[sl, :], s_bc[sl, :], gla_ref[b])
                          for b, sl in enumerate(sls)])
    for b in range(nb):
        gla_ref[b] = ST1[b]
    o_gla = _gla_finish(jnp.concatenate(o, axis=0), gg, norm_g)

    n_lt = S5_WIDTH // LANES
    for b in range(nb):
        for lt in range(n_lt):
            s_perm[lt, b * PERM_STRIDE:b * PERM_STRIDE + CHUNK, :] = (
                u[b * CHUNK:(b + 1) * CHUNK, lt * LANES:(lt + 1) * LANES])
    u_tb = jnp.concatenate(
        [jnp.concatenate([s_perm[lt, pl.ds(t, nb, stride=PERM_STRIDE), :] for lt in range(n_lt)], axis=1)
         for t in range(CHUNK)], axis=0)
    bu_re, bu_im = _s5_input(u_tb, bb_re, bb_im)
    s_re[...] = bu_re
    s_im[...] = bu_im

    n_ch = 4
    cw = S5_CH // n_ch
    for ci in range(n_ch):
        lanes = slice(ci * cw, (ci + 1) * cw)
        a_re = jnp.broadcast_to(ab_re_ref[:, lanes], (nb, cw))
        a_im = jnp.broadcast_to(ab_im_ref[:, lanes], (nb, cw))

        h_re, h_im = hre_ref[:, lanes], him_ref[:, lanes]
        for t in range(CHUNK):
            sl = slice(t * nb, (t + 1) * nb)
            h_re, h_im = (a_re * h_re - a_im * h_im + s_re[sl, lanes],
                          a_re * h_im + a_im * h_re + s_im[sl, lanes])
            s_re[sl, lanes] = h_re
            s_im[sl, lanes] = h_im
        hre_ref[:, lanes] = h_re
        him_ref[:, lanes] = h_im

    y_tb = _s5_output(s_re[...], s_im[...], c_re, c_im)
    for t in range(CHUNK):
        for lt in range(n_lt):
            s_perm[lt, pl.ds(t, nb, stride=PERM_STRIDE), :] = (
                y_tb[t * nb:(t + 1) * nb, lt * LANES:(lt + 1) * LANES])
    y_s5 = jnp.concatenate(
        [jnp.concatenate([s_perm[lt, b * PERM_STRIDE:b * PERM_STRIDE + CHUNK, :] for lt in range(n_lt)],
                         axis=1) for b in range(nb)], axis=0)
    y_s5 = y_s5 + d_ref[...] * u

    gt = jnp.broadcast_to(gt_ref[...], (nb, CHUNK, D_MODEL)).reshape(rows, D_MODEL)
    out = _mix_output(o_gla, y_s5, x, gt, w_glu, b_glu, w_out, ln_g, ln_b)
    o_ref[...] = out.reshape(nb, CHUNK, D_MODEL)


def _mix_weight_list(m, s5):
    return [m["w_in"], m["w_gk"], m["b_gk"], m["norm_g"], s5["ab_re"], s5["ab_im"], s5["bb_re"], s5["bb_im"],
            s5["c_re"], s5["c_im"], s5["d"], m["w_glu"], m["b_glu"], m["w_out"]]


def _mix_prompt(x, mods_p, layer, m, s5, ln_g, ln_b):
    bsz, seq, _ = x.shape
    rows = bsz * CHUNK
    weights = _mix_weight_list(m, s5) + [ln_g, ln_b]
    mod_specs = [
        pl.BlockSpec((None, bsz, 1, D_MODEL), functools.partial(lambda i, col: (layer, 0, 0, col), col=3 + k))
        for k in range(3)
    ]
    out, gla_t, h_re, h_im = pl.pallas_call(
        _mix_prompt_body,
        grid=(seq // CHUNK,),
        in_specs=[pl.BlockSpec((bsz, CHUNK, D_MODEL), lambda i: (0, i, 0))] + mod_specs
        + [_full_spec(w.shape, 1) for w in weights],
        out_specs=[
            pl.BlockSpec((bsz, CHUNK, D_MODEL), lambda i: (0, i, 0)),
            _full_spec((bsz, GLA_VAL, GLA_KEY), 1),
            _full_spec((bsz, S5_CH), 1),
            _full_spec((bsz, S5_CH), 1),
        ],
        out_shape=[
            jax.ShapeDtypeStruct(x.shape, F32),
            jax.ShapeDtypeStruct((bsz, GLA_VAL, GLA_KEY), F32),
            jax.ShapeDtypeStruct((bsz, S5_CH), F32),
            jax.ShapeDtypeStruct((bsz, S5_CH), F32),
        ],
        scratch_shapes=[
            pltpu.VMEM((rows, GLA_KEY), F32), pltpu.VMEM((rows, GLA_KEY), F32),
            pltpu.VMEM((rows, GLA_VAL), F32), pltpu.VMEM((rows, GLA_KEY), F32),
            pltpu.VMEM((S5_WIDTH // LANES, bsz * PERM_STRIDE, LANES), F32),
            pltpu.VMEM((rows, S5_CH), F32), pltpu.VMEM((rows, S5_CH), F32),
        ],
        compiler_params=_params(("arbitrary",)),
        name="gla_s5_prompt",
    )(x, mods_p, mods_p, mods_p, *weights)
    s4 = gla_t.reshape(bsz, GLA_HEADS, GLA_DV, GLA_KEY)
    gla = jnp.stack([s4[:, hh, :, hh * GLA_DK:(hh + 1) * GLA_DK] for hh in range(GLA_HEADS)],
                    axis=1).transpose(0, 1, 3, 2)
    return (out, gla, h_re.reshape(bsz, S5_GROUPS, S5_STATE), h_im.reshape(bsz, S5_GROUPS, S5_STATE))


def _mix_sample_pre_body(x_ref, sh_ref, sc_ref, hre_ref, him_ref,
                         w_in, w_gk, b_gk, ab_re_ref, ab_im_ref, bb_re, bb_im, c_re, c_im, d_ref,
                         q_ref, k_ref, v_ref, gg_ref, dec_ref, y_ref, nre_ref, nim_ref):
    h = x_ref[...] * (1.0 + sc_ref[...]) + sh_ref[...]
    q, k, v, gg, u, gk = _mix_project(h, w_in, w_gk, b_gk)
    q_ref[...] = q
    k_ref[...] = k
    v_ref[...] = v
    gg_ref[...] = gg
    dec_ref[...] = jnp.exp(gk)
    bu_re, bu_im = _s5_input(u, bb_re, bb_im)
    a_re = ab_re_ref[...]
    a_im = ab_im_ref[...]
    h_re = hre_ref[...]
    h_im = him_ref[...]
    n_re = a_re * h_re - a_im * h_im + bu_re
    n_im = a_re * h_im + a_im * h_re + bu_im
    nre_ref[...] = n_re
    nim_ref[...] = n_im
    y_ref[...] = _s5_output(n_re, n_im, c_re, c_im) + d_ref[...] * u


def _gla_sample_step_body(s_ref, qc_ref, kc_ref, dc_ref, v_ref, so_ref, o_ref):
    for j in range(SAMPLE_BLOCK):
        for hh in range(GLA_HEADS):
            rows = slice(hh * GLA_DK, (hh + 1) * GLA_DK)
            lanes = slice(hh * GLA_DV, (hh + 1) * GLA_DV)
            S1 = dc_ref[rows, j:j + 1] * s_ref[j, hh] + kc_ref[rows, j:j + 1] * v_ref[j:j + 1, lanes]
            so_ref[j, hh] = S1
            o_ref[j:j + 1, lanes] = jnp.sum(qc_ref[rows, j:j + 1] * S1, axis=0, keepdims=True)


def _mix_sample_post_body(o_ref_in, gg_ref, y_ref, x_ref, gt_ref, norm_g, w_glu, b_glu, w_out, ln_g, ln_b,
                          o_ref):
    o_gla = _gla_finish(o_ref_in[...], gg_ref[...], norm_g)
    o_ref[...] = _mix_output(o_gla, y_ref[...], x_ref[...], gt_ref[...], w_glu, b_glu, w_out, ln_g, ln_b)


def _mix_sample(x, mods_s, layer, s_gla, s_re, s_im, m, s5, ln_g, ln_b):
    n = x.shape[0]
    ms = _mod_specs_sample(layer, 3, n)
    row_spec = _full_spec((n, D_MODEL), 1)

    def sds(w):
        return jax.ShapeDtypeStruct((n, w), F32)

    def fs(w):
        return _full_spec((n, w), 1)

    pre_w = [m["w_in"], m["w_gk"], m["b_gk"], s5["ab_re"], s5["ab_im"], s5["bb_re"], s5["bb_im"],
             s5["c_re"], s5["c_im"], s5["d"]]
    widths = [GLA_KEY, GLA_KEY, GLA_VAL, GLA_VAL, GLA_KEY, S5_WIDTH, S5_CH, S5_CH]
    q, k, v, gg, dec, y_s5, n_re, n_im = pl.pallas_call(
        _mix_sample_pre_body,
        grid=(1,),
        in_specs=[row_spec, ms[0], ms[1], fs(S5_CH), fs(S5_CH)] + [_full_spec(a.shape, 1) for a in pre_w],
        out_specs=[fs(w) for w in widths],
        out_shape=[sds(w) for w in widths],
        compiler_params=_params(("arbitrary",)),
        name="gla_s5_sample_pre",
    )(x, mods_s, mods_s, s_re.reshape(n, S5_CH), s_im.reshape(n, S5_CH), *pre_w)

    nblk = n // SAMPLE_BLOCK
    st_spec = pl.BlockSpec((SAMPLE_BLOCK, GLA_HEADS, GLA_DK, GLA_DV), lambda i: (i, 0, 0, 0))
    col_spec = pl.BlockSpec((None, GLA_KEY, SAMPLE_BLOCK), lambda i: (i, 0, 0))
    v_spec = pl.BlockSpec((SAMPLE_BLOCK, GLA_VAL), lambda i: (i, 0))
    s_new, o = pl.pallas_call(
        _gla_sample_step_body,
        grid=(nblk,),
        in_specs=[st_spec, col_spec, col_spec, col_spec, v_spec],
        out_specs=[st_spec, v_spec],
        out_shape=[jax.ShapeDtypeStruct(s_gla.shape, F32), sds(GLA_VAL)],
        compiler_params=_params(("arbitrary",)),
        name="gla_sample_step",
    )(s_gla, _to_columns(q), _to_columns(k), _to_columns(dec), v)

    post_w = [m["norm_g"], m["w_glu"], m["b_glu"], m["w_out"], ln_g, ln_b]
    out = pl.pallas_call(
        _mix_sample_post_body,
        grid=(1,),
        in_specs=[fs(GLA_VAL), fs(GLA_VAL), fs(S5_WIDTH), row_spec, ms[2]]
        + [_full_spec(a.shape, 1) for a in post_w],
        out_specs=row_spec,
        out_shape=sds(D_MODEL),
        compiler_params=_params(("arbitrary",)),
        name="gla_s5_sample_post",
    )(o, gg, y_s5, x, mods_s, *post_w)
    return (out, s_new, n_re.reshape(n, S5_GROUPS, S5_STATE), n_im.reshape(n, S5_GROUPS, S5_STATE))


def _pad_to(a, axis, size):
    pad = [(0, 0)] * a.ndim
    pad[axis] = (0, size - a.shape[axis])
    return jnp.pad(a, pad)


def _row(a):
    return a.reshape(1, -1).astype(F32)


def _prep_ffn(wg, wu, wd):
    return (jnp.transpose(wg, (0, 2, 1)).astype(BF16), jnp.transpose(wu, (0, 2, 1)).astype(BF16),
            wd.astype(BF16))


def _prep_mix(w_in, w_out, gla_w_gk, gla_b_gk, gla_norm_g, s5_w_glu, s5_b_glu):
    q, k, v, g, gk_low, u = jnp.split(
        w_in, (GLA_KEY, 2 * GLA_KEY, 2 * GLA_KEY + GLA_VAL, 2 * GLA_KEY + 2 * GLA_VAL,
               2 * GLA_KEY + 2 * GLA_VAL + GLA_GATE_RANK), axis=1)
    w_in_r = jnp.concatenate([q, k, v, g, u, _pad_to(gk_low, 1, LANES)], axis=1)
    return dict(
        w_in=w_in_r.astype(BF16), w_out=w_out.astype(BF16),
        w_gk=_pad_to(gla_w_gk, 0, LANES).astype(BF16), b_gk=_row(gla_b_gk), norm_g=_row(gla_norm_g),
        w_glu=s5_w_glu.astype(BF16), b_glu=_row(s5_b_glu),
    )


def _prep_rwkv(mu, w_r, w_k, w_v, w_o, w0, w1, w2, a0, a1, a2, g1, g2, k_k, k_a, r_k, lnx_g, lnx_b):
    return dict(
        mu=_pad_to(mu, 0, SUBLANES), wr=w_r.astype(BF16), wk=w_k.astype(BF16), wv=w_v.astype(BF16),
        wo=w_o.astype(BF16), w0=_row(w0),
        w1=_pad_to(w1, 1, LORA_PAD).astype(BF16), w2=_pad_to(w2, 0, LORA_PAD).astype(BF16),
        a0=_row(a0),
        a1=_pad_to(a1, 1, LORA_PAD).astype(BF16), a2=_pad_to(a2, 0, LORA_PAD).astype(BF16),
        g1=_pad_to(g1, 1, GATE_LORA_PAD).astype(BF16), g2=_pad_to(g2, 0, GATE_LORA_PAD).astype(BF16),
        k_k=_row(k_k), k_a=_row(k_a), r_k=_row(r_k), lnx_g=_row(lnx_g), lnx_b=_row(lnx_b),
    )


def kernel(x_prompt, x_sample, state_gla, state_s5_re, state_s5_im, state_rwkv_shift, state_rwkv_wkv,
           c_prompt, c_sample, ada_w, ada_b, ln_g, ln_b,
           ffn1_wg, ffn1_wu, ffn1_wd, ffn2_wg, ffn2_wu, ffn2_wd,
           w_in, w_out, gla_w_gk, gla_b_gk, gla_norm_g,
           s5_a_re, s5_a_im, s5_log_step, s5_b_re, s5_b_im, s5_c_re, s5_c_im, s5_d, s5_w_glu, s5_b_glu,
           rwkv_mu, rwkv_w_r, rwkv_w_k, rwkv_w_v, rwkv_w_o, rwkv_w0, rwkv_w1, rwkv_w2,
           rwkv_a0, rwkv_a1, rwkv_a2, rwkv_g1, rwkv_g2, rwkv_k_k, rwkv_k_a, rwkv_r_k,
           rwkv_lnx_g, rwkv_lnx_b):
    bp = x_prompt.shape[0]
    ns = x_sample.shape[0]

    mods_s, mods_p = _ada_mods(c_sample, c_prompt, ada_w, ada_b)
    mods_p = mods_p.reshape(DEPTH, bp, 1, N_MODS * D_MODEL)

    ffn1 = _prep_ffn(ffn1_wg, ffn1_wu, ffn1_wd)
    ffn2 = _prep_ffn(ffn2_wg, ffn2_wu, ffn2_wd)
    mix = _prep_mix(w_in, w_out, gla_w_gk, gla_b_gk, gla_norm_g, s5_w_glu, s5_b_glu)
    s5 = _s5_prepare(s5_a_re, s5_a_im, s5_log_step, s5_b_re, s5_b_im, s5_c_re, s5_c_im, s5_d)
    rwkv = _prep_rwkv(rwkv_mu, rwkv_w_r, rwkv_w_k, rwkv_w_v, rwkv_w_o, rwkv_w0, rwkv_w1, rwkv_w2,
                      rwkv_a0, rwkv_a1, rwkv_a2, rwkv_g1, rwkv_g2, rwkv_k_k, rwkv_k_a, rwkv_r_k,
                      rwkv_lnx_g, rwkv_lnx_b)

    def lnp(layer, idx):
        return _row(ln_g[layer, idx]), _row(ln_b[layer, idx])

    x = x_prompt
    xs = x_sample.reshape(ns, D_MODEL)
    x, xs = _ffn(x, xs, mods_p, mods_s, 0, 0, *ffn1, *lnp(0, 0))
    x, gla_p, s5_re_p, s5_im_p = _mix_prompt(x, mods_p, 0, mix, s5, *lnp(0, 1))
    xs, gla_s, s5_re_s, s5_im_s = _mix_sample(xs, mods_s, 0, state_gla, state_s5_re, state_s5_im,
                                              mix, s5, *lnp(0, 1))
    x, xs = _ffn(x, xs, mods_p, mods_s, 0, 6, *ffn2, *lnp(0, 2))
    x, xs = _ffn(x, xs, mods_p, mods_s, 1, 0, *ffn1, *lnp(1, 0))
    x, shift_p, wkv_p = _rwkv_prompt(x, mods_p, 1, rwkv, *lnp(1, 1))
    xs, shift_s, wkv_s = _rwkv_sample(xs, mods_s, 1, state_rwkv_shift, state_rwkv_wkv, rwkv, *lnp(1, 1))
    y_prompt, xs = _ffn(x, xs, mods_p, mods_s, 1, 6, *ffn2, *lnp(1, 2))
    y_sample = xs.reshape(ns, 1, D_MODEL)

    return (y_prompt, y_sample, gla_p, s5_re_p, s5_im_p, shift_p, wkv_p,
            gla_s, s5_re_s, s5_im_s, shift_s, wkv_s)
```

```python
import functools
import math

import jax
import jax.numpy as jnp
from jax import lax
from jax.experimental import pallas as pl
from jax.experimental.pallas import tpu as pltpu

F32 = jnp.float32
BF16 = jnp.bfloat16

D_MODEL = 1024
DEPTH = 2
DN_ALPHA = (2 * DEPTH) ** 0.25
LN_EPS = 1e-5
RMS_EPS = 1e-5
FFN_RES = 0.5
D_FF = 2752
N_MODS = 9

GLA_HEADS = 4
GLA_DK = 64
GLA_DV = 128
GLA_KEY = GLA_HEADS * GLA_DK
GLA_VAL = GLA_HEADS * GLA_DV
GLA_GATE_RANK = 16
GLA_GATE_NORM = 16.0

S5_GROUP = 16
S5_WIDTH = 512
S5_GROUPS = S5_WIDTH // S5_GROUP
S5_STATE = 64
S5_CH = S5_GROUPS * S5_STATE

RWKV_HEAD = 64
RWKV_HEADS = D_MODEL // RWKV_HEAD
RWKV_LNX_EPS = 64e-5
NORM_EPS = 1e-12

LANES = 128
SUBLANES = 8
MXU_DIM = 256
VMEM_LIMIT = 58 * 1024 * 1024

CHUNK = 64
HEADS_PER_GROUP = MXU_DIM // RWKV_HEAD
N_GROUPS = RWKV_HEADS // HEADS_PER_GROUP
FF_CHUNKS = ((0, 1536), (1536, D_FF))
LORA_PAD = 128
GATE_LORA_PAD = 256
PERM_STRIDE = CHUNK + SUBLANES
SAMPLE_BLOCK = 8


def _dot(a, b):
    return jnp.dot(a, b, preferred_element_type=F32)


def _dot_nt(a, b):
    return lax.dot_general(a, b, (((1,), (1,)), ((), ())), preferred_element_type=F32)


def _bdot(a, w_ref_or_val):
    return _dot(a.astype(BF16), w_ref_or_val)


def _sigmoid(x):
    return 1.0 / (1.0 + jnp.exp(-x))


def _silu(x):
    return x * _sigmoid(x)


def _softplus(x):
    return jnp.maximum(x, 0.0) + jnp.log(1.0 + jnp.exp(-jnp.abs(x)))


def _gelu_tanh(x):
    c = math.sqrt(2.0 / math.pi)
    return 0.5 * x * (1.0 + jnp.tanh(c * (x + 0.044715 * (x * x * x))))


def _layer_norm(x, g, b):
    mu = jnp.mean(x, axis=-1, keepdims=True)
    d = x - mu
    var = jnp.mean(d * d, axis=-1, keepdims=True)
    return d * lax.rsqrt(var + LN_EPS) * g + b


def _split3(x):
    hi = x.astype(BF16)
    r1 = x - hi.astype(F32)
    mid = r1.astype(BF16)
    lo = (r1 - mid.astype(F32)).astype(BF16)
    return hi, mid, lo


def _exact_dot_left01(m01, x):
    hi, mid, lo = _split3(x)
    return _dot(m01, hi) + _dot(m01, mid) + _dot(m01, lo)


def _dot_right01(x, m01):
    hi = x.astype(BF16)
    lo = (x - hi.astype(F32)).astype(BF16)
    return _dot(hi, m01) + _dot(lo, m01)


def _seg_ones(seg):
    r = lax.broadcasted_iota(jnp.int32, (MXU_DIM, MXU_DIM), 0) // seg
    c = lax.broadcasted_iota(jnp.int32, (MXU_DIM, MXU_DIM), 1) // seg
    return jnp.where(r == c, 1.0, 0.0).astype(BF16)


def _seg_sum(x, seg, split=False):
    ones = _seg_ones(seg)
    tiles = []
    for i in range(x.shape[1] // MXU_DIM):
        xt = x[:, i * MXU_DIM:(i + 1) * MXU_DIM]
        tiles.append(_dot_right01(xt, ones) if split else _dot(xt.astype(BF16), ones))
    return jnp.concatenate(tiles, axis=1)


def _tril_ones(n):
    r = lax.broadcasted_iota(jnp.int32, (n, n), 0)
    c = lax.broadcasted_iota(jnp.int32, (n, n), 1)
    return jnp.where(r >= c, 1.0, 0.0).astype(BF16)


def _chunk_cumsum(x, n_batch):
    tri = _tril_ones(CHUNK)
    parts = [_exact_dot_left01(tri, x[b * CHUNK:(b + 1) * CHUNK, :]) for b in range(n_batch)]
    return jnp.concatenate(parts, axis=0)


def _head_stack(z, n_heads, width):
    head = lax.broadcasted_iota(jnp.int32, z.shape, 1) // width
    return jnp.concatenate([jnp.where(head == h, z, 0.0) for h in range(n_heads)], axis=0)


def _block_diag_mask(rows, cols, rblk, cblk):
    r = lax.broadcasted_iota(jnp.int32, (rows, cols), 0) // rblk
    c = lax.broadcasted_iota(jnp.int32, (rows, cols), 1) // cblk
    return r == c


def _full_spec(shape, grid_rank):
    zeros = (0,) * len(shape)
    if grid_rank == 1:
        return pl.BlockSpec(shape, lambda i: zeros, pipeline_mode=pl.Buffered(1))
    return pl.BlockSpec(shape, lambda i, j: zeros, pipeline_mode=pl.Buffered(1))


def _params(semantics):
    return pltpu.CompilerParams(dimension_semantics=semantics, vmem_limit_bytes=VMEM_LIMIT)


ADA_TN = 2304


def _ada_body(cs_ref, cp_ref, w_ref, b_ref, os_ref, op_ref):
    w = w_ref[...].astype(BF16)
    os_ref[...] = _bdot(_silu(cs_ref[...]), w) + b_ref[...]
    op_ref[...] = _bdot(_silu(cp_ref[...]), w) + b_ref[...]


def _ada_mods(c_sample, c_prompt, ada_w, ada_b):
    ns, bp = c_sample.shape[0], c_prompt.shape[0]
    width = N_MODS * D_MODEL
    return pl.pallas_call(
        _ada_body,
        grid=(DEPTH, width // ADA_TN),
        in_specs=[
            pl.BlockSpec((ns, D_MODEL), lambda l, j: (0, 0)),
            pl.BlockSpec((bp, D_MODEL), lambda l, j: (0, 0)),
            pl.BlockSpec((None, D_MODEL, ADA_TN), lambda l, j: (l, 0, j)),
            pl.BlockSpec((None, 1, ADA_TN), lambda l, j: (l, 0, j)),
        ],
        out_specs=[pl.BlockSpec((None, ns, ADA_TN), lambda l, j: (l, 0, j)),
                   pl.BlockSpec((None, bp, ADA_TN), lambda l, j: (l, 0, j))],
        out_shape=[jax.ShapeDtypeStruct((DEPTH, ns, width), F32),
                   jax.ShapeDtypeStruct((DEPTH, bp, width), F32)],
        compiler_params=_params(("arbitrary", "arbitrary")),
        name="ada_mods",
    )(c_sample, c_prompt, ada_w, ada_b.reshape(DEPTH, 1, width))


FFN_TM = 1024


def _ffn_tile(x, sh, sc, gt, wg_ref, wu_ref, wd_ref, g_ref, b_ref):
    h = (x * (1.0 + sc) + sh).astype(BF16)
    acc = None
    for lo, hi in FF_CHUNKS:
        g = _dot_nt(h, wg_ref[lo:hi, :])
        u = _dot_nt(h, wu_ref[lo:hi, :])
        y = _bdot(_silu(g) * u, wd_ref[lo:hi, :])
        acc = y if acc is None else acc + y
    r = FFN_RES * (1.0 + gt) * acc
    return _layer_norm(DN_ALPHA * x + r, g_ref[...], b_ref[...])


def _ffn_body(x_ref, sh_ref, sc_ref, gt_ref, xs_ref, shs_ref, scs_ref, gts_ref,
              wg_ref, wu_ref, wd_ref, g_ref, b_ref, o_ref, os_ref):
    last = pl.num_programs(0) - 1

    @pl.when(pl.program_id(0) < last)
    def _():
        o_ref[...] = _ffn_tile(x_ref[...], sh_ref[...], sc_ref[...], gt_ref[...],
                               wg_ref, wu_ref, wd_ref, g_ref, b_ref)

    @pl.when(pl.program_id(0) == last)
    def _():
        os_ref[...] = _ffn_tile(xs_ref[...], shs_ref[...], scs_ref[...], gts_ref[...],
                                wg_ref, wu_ref, wd_ref, g_ref, b_ref)


def _mod_specs_sample(layer, first, rows):
    return [
        pl.BlockSpec((None, rows, D_MODEL), functools.partial(
            lambda i, col: (layer, 0, col), col=first + k))
        for k in range(3)
    ]


def _ffn(x, xs, mods_p, mods_s, layer, first, wg, wu, wd, ln_g, ln_b):
    bsz, seq, _ = x.shape
    n = xs.shape[0]
    per_b = seq // FFN_TM
    n_tiles = bsz * per_b

    def tile_idx(t):
        tc = jnp.minimum(t, n_tiles - 1)
        return tc // per_b, tc % per_b

    mod_specs = [
        pl.BlockSpec((None, None, 1, D_MODEL), functools.partial(
            lambda t, col: (layer, tile_idx(t)[0], 0, col), col=first + k))
        for k in range(3)
    ]
    w_spec = pl.BlockSpec((None, D_FF, D_MODEL), lambda t: (layer, 0, 0), pipeline_mode=pl.Buffered(1))
    tile = pl.BlockSpec((None, FFN_TM, D_MODEL), lambda t: (*tile_idx(t), 0))
    return pl.pallas_call(
        _ffn_body,
        grid=(n_tiles + 1,),
        in_specs=[tile] + mod_specs + [_full_spec((n, D_MODEL), 1)] + _mod_specs_sample(layer, first, n)
        + [w_spec] * 3 + [_full_spec((1, D_MODEL), 1)] * 2,
        out_specs=[tile, _full_spec((n, D_MODEL), 1)],
        out_shape=[jax.ShapeDtypeStruct(x.shape, F32), jax.ShapeDtypeStruct(xs.shape, F32)],
        compiler_params=_params(("arbitrary",)),
        name="ffn",
    )(x, mods_p, mods_p, mods_p, xs, mods_s, mods_s, mods_s, wg, wu, wd, ln_g, ln_b)


def _col_groups(width):
    return [slice(lo, lo + width) for lo in range(0, D_MODEL, width)]


def _rwkv_mixes(h, prev, mu_ref):
    xx = prev - h
    mu = mu_ref[...]
    return [(h + xx * mu[i:i + 1, :]).astype(BF16) for i in range(6)]


def _rwkv_lora_in(mixes, w1, a1):
    return jnp.tanh(_dot(mixes[1], w1[...])).astype(BF16), _dot(mixes[4], a1[...]).astype(BF16)


def _rwkv_project_cols(cols, mixes, tw, ta, wr, wk, wv, w0, w2, a0, a2, kk_ref, ka_ref):
    r = _dot(mixes[0], wr[:, cols])
    k = _dot(mixes[2], wk[:, cols])
    v = _dot(mixes[3], wv[:, cols])
    wl = w0[:, cols] + _dot(tw, w2[:, cols])
    log_decay = -math.exp(-0.5) * _sigmoid(wl)
    a_sig = _sigmoid(a0[:, cols] + _dot(ta, a2[:, cols]))
    kk = k * kk_ref[:, cols]
    norm = jnp.sqrt(_seg_sum(kk * kk, RWKV_HEAD))
    kk = kk / jnp.maximum(norm, NORM_EPS)
    k2 = k * (1.0 + (a_sig - 1.0) * ka_ref[:, cols])
    return r, log_decay, k2, v, -kk, kk * a_sig


def _rwkv_bonus(r, k2, v, rk):
    return _seg_sum(r * k2 * rk, RWKV_HEAD) * v


def _rwkv_gate_in(xg, g1):
    return _sigmoid(_dot(xg, g1[...])).astype(BF16)


def _rwkv_out_cols(cols, y, bonus, sg, g2, wo, lng, lnb):
    gate = _dot(sg, g2[:, cols])
    inv_n = 1.0 / RWKV_HEAD
    mu_y = _seg_sum(y, RWKV_HEAD, split=True) * inv_n
    d = y - mu_y
    var = _seg_sum(d * d, RWKV_HEAD) * inv_n
    y = d * lax.rsqrt(var + RWKV_LNX_EPS) * lng[:, cols] + lnb[:, cols]
    return _bdot((y + bonus) * gate, wo[cols, :])


def _rwkv_weight_list(p):
    return [p["mu"], p["wr"], p["wk"], p["wv"], p["w0"], p["w1"], p["w2"], p["a0"], p["a1"], p["a2"],
            p["k_k"], p["k_a"], p["r_k"]]


def _rwkv_out_weight_list(p, ln_g, ln_b):
    return [p["g1"], p["g2"], p["wo"], p["lnx_g"], p["lnx_b"], ln_g, ln_b]


NEUMANN_LEVELS = int(math.log2(CHUNK)) - 1
GROUPS_PER_STAGE = 2


def _wkv_chunks(problems):
    c = CHUNK
    hp = HEADS_PER_GROUP
    w = hp * RWKV_HEAD
    n = hp * c
    t_idx = lax.broadcasted_iota(jnp.int32, (c, w), 0)
    s_idx = lax.broadcasted_iota(jnp.int32, (c, w), 1) % c
    strict = t_idx > s_idx
    incl = t_idx >= s_idx
    eye = jnp.where(t_idx == s_idx, 1.0, 0.0)
    bd_mask = _block_diag_mask(w, w, RWKV_HEAD, RWKV_HEAD)
    idx = range(len(problems))
    At, Rt, Kt, Bt, V, S0, dec = zip(*problems)

    def bf(x):
        return x.astype(BF16)

    def stack(x, width):
        return bf(_head_stack(x, hp, width))

    X2 = [jnp.concatenate([At[i], Rt[i]], axis=0) for i in idx]
    KB = [jnp.concatenate([stack(Kt[i], RWKV_HEAD), stack(Bt[i], RWKV_HEAD)], axis=0) for i in idx]
    sc = [_dot_nt(X2[i], KB[i]) for i in idx]
    XS = [_dot_nt(X2[i], bf(S0[i])) for i in idx]
    Vbd = [stack(V[i], RWKV_HEAD) for i in idx]
    Aab = [jnp.where(strict, sc[i][:c, n:], 0.0) for i in idx]
    rhs = [XS[i][:c] + _dot(bf(jnp.where(strict, sc[i][:c, :n], 0.0)), Vbd[i]) for i in idx]

    Q = Aab
    P = [eye + Aab[i] for i in idx]
    Qbd = [stack(Q[i], c) for i in idx]
    for _ in range(NEUMANN_LEVELS):
        Q = [_dot(bf(Q[i]), Qbd[i]) for i in idx]
        Qbd = [stack(Q[i], c) for i in idx]
        P = [P[i] + _dot(bf(P[i]), Qbd[i]) for i in idx]
    U = [_dot(bf(P[i]), stack(rhs[i], RWKV_HEAD)) for i in idx]

    Y = [XS[i][c:] + _dot(bf(jnp.where(incl, sc[i][c:, :n], 0.0)), Vbd[i])
         + _dot(bf(jnp.where(incl, sc[i][c:, n:], 0.0)), stack(U[i], RWKV_HEAD)) for i in idx]

    upd = [_dot(bf(jnp.concatenate([V[i], U[i]], axis=0).T),
                bf(jnp.concatenate([Kt[i], Bt[i]], axis=0))) for i in idx]
    S1 = [(S0[i] + jnp.where(bd_mask, upd[i], 0.0)) * dec[i] for i in idx]
    return Y, S1


def _rwkv_prompt_body(x_ref, sh_ref, sc_ref, gt_ref,
                      mu, wr, wk, wv, w0, w1, w2, a0, a1, a2, kk_ref, ka_ref, rk_ref,
                      g1, g2, wo, lng, lnb, ln_g, ln_b,
                      o_ref, shift_ref, state_ref,
                      carry, s_at, s_rt, s_kt, s_bt, s_v, s_dec, s_bonus, s_sg):
    nb = x_ref.shape[0]
    rows = nb * CHUNK
    gw = HEADS_PER_GROUP * RWKV_HEAD
    step = pl.program_id(0)

    @pl.when(step == 0)
    def _():
        carry[...] = jnp.zeros(carry.shape, F32)
        state_ref[...] = jnp.zeros(state_ref.shape, F32)

    h3 = x_ref[...] * (1.0 + sc_ref[...]) + sh_ref[...]
    h = h3.reshape(rows, D_MODEL)
    first = lax.broadcasted_iota(jnp.int32, (rows, D_MODEL), 0) % CHUNK == 0
    carried = jnp.broadcast_to(carry[...], (nb, CHUNK, D_MODEL)).reshape(rows, D_MODEL)
    prev = jnp.where(first, carried, pltpu.roll(h, 1, axis=0))
    carry[...] = h3[:, CHUNK - 1:CHUNK, :]
    shift_ref[...] = h3[:, CHUNK - 1:CHUNK, :]

    mixes = _rwkv_mixes(h, prev, mu)
    tw, ta = _rwkv_lora_in(mixes, w1, a1)
    s_sg[...] = _rwkv_gate_in(mixes[5], g1)
    groups = _col_groups(gw)

    def project(gi):
        cols = groups[gi]
        r, lw, k2, v, av, bv = _rwkv_project_cols(
            cols, mixes, tw, ta, wr, wk, wv, w0, w2, a0, a2, kk_ref, ka_ref)
        s_bonus[:, cols] = _rwkv_bonus(r, k2, v, rk_ref[:, cols])
        g = _chunk_cumsum(lw, nb)
        e_neg = jnp.exp(-g)
        s_at[gi] = (av * jnp.exp(g - lw)).astype(BF16)
        s_rt[gi] = (r * jnp.exp(g)).astype(BF16)
        s_kt[gi] = (k2 * e_neg).astype(BF16)
        s_bt[gi] = (bv * e_neg).astype(BF16)
        s_v[gi] = v.astype(BF16)
        s_dec[gi] = jnp.exp(g.reshape(nb, CHUNK, gw)[:, CHUNK - 1:CHUNK, :])

    def scan(gis):
        keys = [(gi, b) for gi in gis for b in range(nb)]
        sls = [slice(b * CHUNK, (b + 1) * CHUNK) for b in range(nb)]
        Y, S1 = _wkv_chunks([
            (s_at[gi, sls[b], :], s_rt[gi, sls[b], :], s_kt[gi, sls[b], :].astype(F32),
             s_bt[gi, sls[b], :].astype(F32), s_v[gi, sls[b], :].astype(F32),
             state_ref[b, gi], s_dec[gi, b]) for gi, b in keys])
        for (gi, b), s_val in zip(keys, S1):
            state_ref[b, gi] = s_val
        return [jnp.concatenate(Y[n * nb:(n + 1) * nb], axis=0) for n in range(len(gis))]

    pairs = [tuple(range(g0, g0 + GROUPS_PER_STAGE)) for g0 in range(0, N_GROUPS, GROUPS_PER_STAGE)]
    for gi in pairs[0]:
        project(gi)
    out = None
    for n, gis in enumerate(pairs):
        ys = scan(gis)
        if n + 1 < len(pairs):
            for gi in pairs[n + 1]:
                project(gi)
        for gi, y in zip(gis, ys):
            part = _rwkv_out_cols(groups[gi], y, s_bonus[:, groups[gi]], s_sg[...], g2, wo, lng, lnb)
            out = part if out is None else out + part
    gt = jnp.broadcast_to(gt_ref[...], (nb, CHUNK, D_MODEL)).reshape(rows, D_MODEL)
    x = x_ref[...].reshape(rows, D_MODEL)
    o_ref[...] = _layer_norm(DN_ALPHA * x + (1.0 + gt) * out,
                             ln_g[...], ln_b[...]).reshape(nb, CHUNK, D_MODEL)


def _rwkv_prompt(x, mods_p, layer, p, ln_g, ln_b):
    bsz, seq, _ = x.shape
    rows = bsz * CHUNK
    gw = HEADS_PER_GROUP * RWKV_HEAD
    weights = _rwkv_weight_list(p) + _rwkv_out_weight_list(p, ln_g, ln_b)
    mod_specs = [
        pl.BlockSpec((None, bsz, 1, D_MODEL), functools.partial(lambda i, col: (layer, 0, 0, col), col=3 + k))
        for k in range(3)
    ]
    slab16 = pltpu.VMEM((N_GROUPS, rows, gw), BF16)
    out, shift, state = pl.pallas_call(
        _rwkv_prompt_body,
        grid=(seq // CHUNK,),
        in_specs=[pl.BlockSpec((bsz, CHUNK, D_MODEL), lambda i: (0, i, 0))] + mod_specs
        + [_full_spec(w.shape, 1) for w in weights],
        out_specs=[
            pl.BlockSpec((bsz, CHUNK, D_MODEL), lambda i: (0, i, 0)),
            _full_spec((bsz, 1, D_MODEL), 1),
            _full_spec((bsz, N_GROUPS, gw, gw), 1),
        ],
        out_shape=[
            jax.ShapeDtypeStruct(x.shape, F32),
            jax.ShapeDtypeStruct((bsz, 1, D_MODEL), F32),
            jax.ShapeDtypeStruct((bsz, N_GROUPS, gw, gw), F32),
        ],
        scratch_shapes=[pltpu.VMEM((bsz, 1, D_MODEL), F32)] + [slab16] * 5
        + [pltpu.VMEM((N_GROUPS, bsz, 1, gw), F32),
           pltpu.VMEM((rows, D_MODEL), F32), pltpu.VMEM((rows, GATE_LORA_PAD), BF16)],
        compiler_params=_params(("arbitrary",)),
        name="rwkv_prompt",
    )(x, mods_p, mods_p, mods_p, *weights)
    s5d = state.reshape(bsz, N_GROUPS, HEADS_PER_GROUP, RWKV_HEAD, gw)
    wkv = jnp.stack([s5d[:, :, hh, :, hh * RWKV_HEAD:(hh + 1) * RWKV_HEAD]
                     for hh in range(HEADS_PER_GROUP)], axis=2)
    return out, shift.reshape(bsz, D_MODEL), wkv.reshape(bsz, RWKV_HEADS, RWKV_HEAD, RWKV_HEAD)


def _rwkv_sample_pre_body(x_ref, sh_ref, sc_ref, prev_ref,
                          mu, wr, wk, wv, w0, w1, w2, a0, a1, a2, kk_ref, ka_ref, rk_ref,
                          h_ref, rt_ref, wt_ref, kt_ref, vt_ref, at_ref, bt_ref, xg_ref, bonus_ref):
    h = x_ref[...] * (1.0 + sc_ref[...]) + sh_ref[...]
    h_ref[...] = h
    mixes = _rwkv_mixes(h, prev_ref[...], mu)
    xg_ref[...] = mixes[5]
    tw, ta = _rwkv_lora_in(mixes, w1, a1)
    for cols in _col_groups(MXU_DIM):
        r, lw, k2, v, av, bv = _rwkv_project_cols(
            cols, mixes, tw, ta, wr, wk, wv, w0, w2, a0, a2, kk_ref, ka_ref)
        bonus_ref[:, cols] = _rwkv_bonus(r, k2, v, rk_ref[:, cols])
        rt_ref[cols, :] = r.T
        wt_ref[cols, :] = jnp.exp(lw).T
        kt_ref[cols, :] = k2.T
        vt_ref[cols, :] = v.T
        at_ref[cols, :] = av.T
        bt_ref[cols, :] = bv.T


def _rwkv_sample_step_body(s_ref, r_ref, w_ref, k_ref, v_ref, a_ref, b_ref, so_ref, y_ref):
    a, w, b, k, r = a_ref[...], w_ref[...], b_ref[...], k_ref[...], r_ref[...]

    def row_step(i, carry_val):
        S = s_ref[i]
        sa = jnp.sum(S * a, axis=0, keepdims=True)
        S1 = S * w + sa * b + v_ref[pl.ds(i, 1), :] * k
        so_ref[i] = S1
        y_ref[pl.ds(i, 1), :] = jnp.sum(S1 * r, axis=0, keepdims=True)
        return carry_val

    lax.fori_loop(0, RWKV_HEAD, row_step, 0, unroll=4)


def _rwkv_sample_post_body(yt_ref, bonus_ref, xg_ref, x_ref, gt_ref,
                           g1, g2, wo, lng, lnb, ln_g, ln_b, o_ref):
    sg = _rwkv_gate_in(xg_ref[...], g1)
    out = None
    for cols in _col_groups(MXU_DIM):
        part = _rwkv_out_cols(cols, yt_ref[cols, :].T, bonus_ref[:, cols], sg, g2, wo, lng, lnb)
        out = part if out is None else out + part
    o_ref[...] = _layer_norm(DN_ALPHA * x_ref[...] + (1.0 + gt_ref[...]) * out, ln_g[...], ln_b[...])


def _to_columns(a):
    n, f = a.shape
    return a.reshape(n // SAMPLE_BLOCK, SAMPLE_BLOCK, f).transpose(0, 2, 1)


def _rwkv_sample(x, mods_s, layer, s_shift, s_wkv, p, ln_g, ln_b):
    n = x.shape[0]
    row = jax.ShapeDtypeStruct((n, D_MODEL), F32)
    row_spec = _full_spec((n, D_MODEL), 1)
    pre_w = _rwkv_weight_list(p)
    ms = _mod_specs_sample(layer, 3, n)
    col = jax.ShapeDtypeStruct((D_MODEL, n), F32)
    col_spec = _full_spec((D_MODEL, n), 1)
    h, *vecs, xg, bonus = pl.pallas_call(
        _rwkv_sample_pre_body,
        grid=(1,),
        in_specs=[row_spec, ms[0], ms[1], row_spec] + [_full_spec(a.shape, 1) for a in pre_w],
        out_specs=[row_spec] + [col_spec] * 6 + [row_spec] * 2,
        out_shape=[row] + [col] * 6 + [jax.ShapeDtypeStruct((n, D_MODEL), BF16), row],
        compiler_params=_params(("arbitrary",)),
        name="rwkv_sample_pre",
    )(x, mods_s, mods_s, s_shift, *pre_w)

    st_spec = pl.BlockSpec((None, RWKV_HEAD, RWKV_HEAD, n), lambda hd: (hd, 0, 0, 0))
    vec_spec = pl.BlockSpec((RWKV_HEAD, n), lambda hd: (hd, 0))
    st_new, yt = pl.pallas_call(
        _rwkv_sample_step_body,
        grid=(RWKV_HEADS,),
        in_specs=[st_spec] + [vec_spec] * len(vecs),
        out_specs=[st_spec, vec_spec],
        out_shape=[jax.ShapeDtypeStruct((RWKV_HEADS, RWKV_HEAD, RWKV_HEAD, n), F32), col],
        compiler_params=_params(("arbitrary",)),
        name="rwkv_sample_step",
    )(jnp.transpose(s_wkv, (1, 2, 3, 0)), *vecs)
    s_new = jnp.transpose(st_new, (3, 0, 1, 2))

    post_w = _rwkv_out_weight_list(p, ln_g, ln_b)
    out = pl.pallas_call(
        _rwkv_sample_post_body,
        grid=(1,),
        in_specs=[col_spec] + [row_spec] * 3 + [ms[2]] + [_full_spec(a.shape, 1) for a in post_w],
        out_specs=row_spec,
        out_shape=row,
        compiler_params=_params(("arbitrary",)),
        name="rwkv_sample_post",
    )(yt, bonus, xg, x, mods_s, *post_w)
    return out, h, s_new


S5_TILE_GROUPS = LANES // S5_GROUP
S5_TILES = S5_GROUPS // S5_TILE_GROUPS


def _s5_prep_body(are_ref, aim_ref, ls_ref, bre_ref, bim_ref, cre_ref, cim_ref,
                  abre_ref, abim_ref, bbre_ref, bbim_ref, cdre_ref, cdim_ref):
    a_re = are_ref[...]
    a_im = aim_ref[...]
    dt = jnp.exp(ls_ref[...])
    mag = jnp.exp(a_re * dt)
    ab_re = mag * jnp.cos(a_im * dt)
    ab_im = mag * jnp.sin(a_im * dt)
    den = a_re * a_re + a_im * a_im
    nr = ab_re - 1.0
    z_re = (nr * a_re + ab_im * a_im) / den
    z_im = (ab_im * a_re - nr * a_im) / den
    b_re = bre_ref[...]
    b_im = bim_ref[...]
    abre_ref[...] = ab_re
    abim_ref[...] = ab_im
    bb_re = (z_re * b_re - z_im * b_im).astype(BF16)
    bb_im = (z_re * b_im + z_im * b_re).astype(BF16)
    for ref in (bbre_ref, bbim_ref, cdre_ref, cdim_ref):
        ref[...] = jnp.zeros(ref.shape, BF16)
    c, p = S5_GROUP, S5_STATE
    for g in range(S5_GROUPS):
        k, j = divmod(g, S5_TILE_GROUPS)
        bbre_ref[k, j * c:(j + 1) * c, j * p:(j + 1) * p] = bb_re[g]
        bbim_ref[k, j * c:(j + 1) * c, j * p:(j + 1) * p] = bb_im[g]
        cdre_ref[k, j * p:(j + 1) * p, j * c:(j + 1) * c] = cre_ref[g].astype(BF16)
        cdim_ref[k, j * p:(j + 1) * p, j * c:(j + 1) * c] = cim_ref[g].astype(BF16)


def _s5_prepare(s5_a_re, s5_a_im, s5_log_step, s5_b_re, s5_b_im, s5_c_re, s5_c_im, s5_d):
    G, P, C = S5_GROUPS, S5_STATE, S5_GROUP
    small = jax.ShapeDtypeStruct((G, 1, P), F32)
    bb_shape = (S5_TILES, LANES, S5_TILE_GROUPS * P)
    cd_shape = (S5_TILES, S5_TILE_GROUPS * P, LANES)
    ls = jnp.broadcast_to(s5_log_step.reshape(G, 1, 1), (G, 1, P))
    ab_re, ab_im, bb_re, bb_im, cd_re, cd_im = pl.pallas_call(
        _s5_prep_body,
        grid=(1,),
        in_specs=[_full_spec((G, 1, P), 1)] * 3 + [_full_spec((G, C, P), 1)] * 2
        + [_full_spec((G, P, C), 1)] * 2,
        out_specs=[_full_spec((G, 1, P), 1)] * 2 + [_full_spec(bb_shape, 1)] * 2
        + [_full_spec(cd_shape, 1)] * 2,
        out_shape=[small, small] + [jax.ShapeDtypeStruct(bb_shape, BF16)] * 2
        + [jax.ShapeDtypeStruct(cd_shape, BF16)] * 2,
        name="s5_prepare",
    )(s5_a_re.reshape(G, 1, P), s5_a_im.reshape(G, 1, P), ls,
      s5_b_re.transpose(0, 2, 1), s5_b_im.transpose(0, 2, 1),
      s5_c_re.transpose(0, 2, 1), s5_c_im.transpose(0, 2, 1))
    return dict(
        ab_re=ab_re.reshape(1, G * P), ab_im=ab_im.reshape(1, G * P),
        bb_re=bb_re, bb_im=bb_im, c_re=cd_re, c_im=cd_im, d=s5_d.reshape(1, G * C),
    )


def _s5_input(u, bb_re_ref, bb_im_ref):
    n_tiles = bb_re_ref.shape[0]
    ub = u.astype(BF16)
    re = [_dot(ub[:, k * LANES:(k + 1) * LANES], bb_re_ref[k]) for k in range(n_tiles)]
    im = [_dot(ub[:, k * LANES:(k + 1) * LANES], bb_im_ref[k]) for k in range(n_tiles)]
    return jnp.concatenate(re, axis=1), jnp.concatenate(im, axis=1)


def _s5_output(h_re, h_im, c_re_ref, c_im_ref):
    n_tiles = c_re_ref.shape[0]
    w = h_re.shape[1] // n_tiles
    hr = h_re.astype(BF16)
    hi = h_im.astype(BF16)
    ys = [_dot(hr[:, k * w:(k + 1) * w], c_re_ref[k]) - _dot(hi[:, k * w:(k + 1) * w], c_im_ref[k])
          for k in range(n_tiles)]
    return jnp.concatenate(ys, axis=1)


C_Q, C_K, C_V, C_G, C_U, C_GK, C_END = 0, 256, 512, 1024, 1536, 2048, 2176


def _mix_project(h, w_in, w_gk, b_gk):
    p = _bdot(h, w_in[...])
    q = p[:, C_Q:C_K] * (GLA_DK ** -0.5)
    k = p[:, C_K:C_V]
    v = p[:, C_V:C_G]
    gg = p[:, C_G:C_U]
    u = p[:, C_U:C_GK]
    z = _bdot(p[:, C_GK:C_END], w_gk[...]) + b_gk[...]
    gk = -_softplus(-z) * (1.0 / GLA_GATE_NORM)
    return q, k, v, gg, u, gk


def _gla_finish(o, gg, norm_g):
    parts = []
    for hh in range(GLA_HEADS):
        oh = o[:, hh * GLA_DV:(hh + 1) * GLA_DV]
        parts.append(oh * lax.rsqrt(jnp.mean(oh * oh, axis=-1, keepdims=True) + RMS_EPS) * norm_g[...])
    return jnp.concatenate(parts, axis=1) * _silu(gg)


def _mix_output(o_gla, y_s5, x, gt, w_glu, b_glu, w_out, ln_g, ln_b):
    z = _gelu_tanh(y_s5)
    o_s5 = z * _sigmoid(_bdot(z, w_glu[...]) + b_glu[...])
    out = _bdot(o_gla, w_out[0:GLA_VAL, :]) + _bdot(o_s5, w_out[GLA_VAL:GLA_VAL + S5_WIDTH, :])
    return _layer_norm(DN_ALPHA * x + (1.0 + gt) * out, ln_g[...], ln_b[...])


def _gla_chunks(problems):
    c = CHUNK
    mid = c // 2 - 1
    idx = range(len(problems))
    q, k, v, bc, ST = zip(*problems)
    t_idx = lax.broadcasted_iota(jnp.int32, (c, GLA_KEY), 0)
    s_idx = lax.broadcasted_iota(jnp.int32, (c, GLA_KEY), 1) % c
    causal = t_idx >= s_idx
    bd_mask = _block_diag_mask(GLA_VAL, GLA_KEY, GLA_DV, GLA_DK)

    def bf(x):
        return x.astype(BF16)

    b_mid = [bc[i][mid:mid + 1, :] for i in idx]
    b_last = [bc[i][c - 1:c, :] for i in idx]
    q_in = [bf(q[i] * jnp.exp(bc[i] - b_mid[i])) for i in idx]
    k_in = [bf(_head_stack(k[i] * jnp.exp(b_mid[i] - bc[i]), GLA_HEADS, GLA_DK)) for i in idx]
    scores = [jnp.where(causal, _dot_nt(q_in[i], k_in[i]), 0.0) for i in idx]
    o_inter = [_dot_nt(bf(q[i] * jnp.exp(bc[i])), bf(ST[i])) for i in idx]
    upd = [_dot(bf(v[i].T), bf(k[i] * jnp.exp(b_last[i] - bc[i]))) for i in idx]
    o = [o_inter[i] + _dot(bf(scores[i]), bf(_head_stack(v[i], GLA_HEADS, GLA_DV))) for i in idx]
    ST1 = [ST[i] * jnp.exp(b_last[i]) + jnp.where(bd_mask, upd[i], 0.0) for i in idx]
    return o, ST1


def _mix_prompt_body(x_ref, sh_ref, sc_ref, gt_ref,
                     w_in, w_gk, b_gk, norm_g, ab_re_ref, ab_im_ref, bb_re, bb_im, c_re, c_im, d_ref,
                     w_glu, b_glu, w_out, ln_g, ln_b,
                     o_ref, gla_ref, hre_ref, him_ref,
                     s_q, s_k, s_v, s_bc, s_perm, s_re, s_im):
    nb = x_ref.shape[0]
    rows = nb * CHUNK
    step = pl.program_id(0)

    @pl.when(step == 0)
    def _():
        gla_ref[...] = jnp.zeros(gla_ref.shape, F32)
        hre_ref[...] = jnp.zeros(hre_ref.shape, F32)
        him_ref[...] = jnp.zeros(him_ref.shape, F32)

    x3 = x_ref[...]
    h3 = x3 * (1.0 + sc_ref[...]) + sh_ref[...]
    x = x3.reshape(rows, D_MODEL)
    h = h3.reshape(rows, D_MODEL)
    q, k, v, gg, u, gk = _mix_project(h, w_in, w_gk, b_gk)

    s_q[...] = q
    s_k[...] = k
    s_v[...] = v
    s_bc[...] = _chunk_cumsum(gk, nb)

    sls = [slice(b * CHUNK, (b + 1) * CHUNK) for b in range(nb)]
    o, ST1 = _gla_chunks([(s_q[sl, :], s_k[sl, :], s_v[sl, :], s_bc[sl, :], gla_ref[b])
                          for b, sl in enumerate(sls)])
    for b in range(nb):
        gla_ref[b] = ST1[b]
    o_gla = _gla_finish(jnp.concatenate(o, axis=0), gg, norm_g)

    n_lt = S5_WIDTH // LANES
    for b in range(nb):
        for lt in range(n_lt):
            s_perm[lt, b * PERM_STRIDE:b * PERM_STRIDE + CHUNK, :] = (
                u[b * CHUNK:(b + 1) * CHUNK, lt * LANES:(lt + 1) * LANES])
    u_tb = jnp.concatenate(
        [jnp.concatenate([s_perm[lt, pl.ds(t, nb, stride=PERM_STRIDE), :] for lt in range(n_lt)], axis=1)
         for t in range(CHUNK)], axis=0)
    bu_re, bu_im = _s5_input(u_tb, bb_re, bb_im)
    s_re[...] = bu_re
    s_im[...] = bu_im

    n_ch = 4
    cw = S5_CH // n_ch
    for ci in range(n_ch):
        lanes = slice(ci * cw, (ci + 1) * cw)
        a_re = jnp.broadcast_to(ab_re_ref[:, lanes], (nb, cw))
        a_im = jnp.broadcast_to(ab_im_ref[:, lanes], (nb, cw))

        h_re, h_im = hre_ref[:, lanes], him_ref[:, lanes]
        for t in range(CHUNK):
            sl = slice(t * nb, (t + 1) * nb)
            h_re, h_im = (a_re * h_re - a_im * h_im + s_re[sl, lanes],
                          a_re * h_im + a_im * h_re + s_im[sl, lanes])
            s_re[sl, lanes] = h_re
            s_im[sl, lanes] = h_im
        hre_ref[:, lanes] = h_re
        him_ref[:, lanes] = h_im

    y_tb = _s5_output(s_re[...], s_im[...], c_re, c_im)
    for t in range(CHUNK):
        for lt in range(n_lt):
            s_perm[lt, pl.ds(t, nb, stride=PERM_STRIDE), :] = (
                y_tb[t * nb:(t + 1) * nb, lt * LANES:(lt + 1) * LANES])
    y_s5 = jnp.concatenate(
        [jnp.concatenate([s_perm[lt, b * PERM_STRIDE:b * PERM_STRIDE + CHUNK, :] for lt in range(n_lt)],
                         axis=1) for b in range(nb)], axis=0)
    y_s5 = y_s5 + d_ref[...] * u

    gt = jnp.broadcast_to(gt_ref[...], (nb, CHUNK, D_MODEL)).reshape(rows, D_MODEL)
    out = _mix_output(o_gla, y_s5, x, gt, w_glu, b_glu, w_out, ln_g, ln_b)
    o_ref[...] = out.reshape(nb, CHUNK, D_MODEL)


def _mix_weight_list(m, s5):
    return [m["w_in"], m["w_gk"], m["b_gk"], m["norm_g"], s5["ab_re"], s5["ab_im"], s5["bb_re"], s5["bb_im"],
            s5["c_re"], s5["c_im"], s5["d"], m["w_glu"], m["b_glu"], m["w_out"]]


def _mix_prompt(x, mods_p, layer, m, s5, ln_g, ln_b):
    bsz, seq, _ = x.shape
    rows = bsz * CHUNK
    weights = _mix_weight_list(m, s5) + [ln_g, ln_b]
    mod_specs = [
        pl.BlockSpec((None, bsz, 1, D_MODEL), functools.partial(lambda i, col: (layer, 0, 0, col), col=3 + k))
        for k in range(3)
    ]
    out, gla_t, h_re, h_im = pl.pallas_call(
        _mix_prompt_body,
        grid=(seq // CHUNK,),
        in_specs=[pl.BlockSpec((bsz, CHUNK, D_MODEL), lambda i: (0, i, 0))] + mod_specs
        + [_full_spec(w.shape, 1) for w in weights],
        out_specs=[
            pl.BlockSpec((bsz, CHUNK, D_MODEL), lambda i: (0, i, 0)),
            _full_spec((bsz, GLA_VAL, GLA_KEY), 1),
            _full_spec((bsz, S5_CH), 1),
            _full_spec((bsz, S5_CH), 1),
        ],
        out_shape=[
            jax.ShapeDtypeStruct(x.shape, F32),
            jax.ShapeDtypeStruct((bsz, GLA_VAL, GLA_KEY), F32),
            jax.ShapeDtypeStruct((bsz, S5_CH), F32),
            jax.ShapeDtypeStruct((bsz, S5_CH), F32),
        ],
        scratch_shapes=[
            pltpu.VMEM((rows, GLA_KEY), F32), pltpu.VMEM((rows, GLA_KEY), F32),
            pltpu.VMEM((rows, GLA_VAL), F32), pltpu.VMEM((rows, GLA_KEY), F32),
            pltpu.VMEM((S5_WIDTH // LANES, bsz * PERM_STRIDE, LANES), F32),
            pltpu.VMEM((rows, S5_CH), F32), pltpu.VMEM((rows, S5_CH), F32),
        ],
        compiler_params=_params(("arbitrary",)),
        name="gla_s5_prompt",
    )(x, mods_p, mods_p, mods_p, *weights)
    s4 = gla_t.reshape(bsz, GLA_HEADS, GLA_DV, GLA_KEY)
    gla = jnp.stack([s4[:, hh, :, hh * GLA_DK:(hh + 1) * GLA_DK] for hh in range(GLA_HEADS)],
                    axis=1).transpose(0, 1, 3, 2)
    return (out, gla, h_re.reshape(bsz, S5_GROUPS, S5_STATE), h_im.reshape(bsz, S5_GROUPS, S5_STATE))


def _mix_sample_pre_body(x_ref, sh_ref, sc_ref, hre_ref, him_ref,
                         w_in, w_gk, b_gk, ab_re_ref, ab_im_ref, bb_re, bb_im, c_re, c_im, d_ref,
                         q_ref, k_ref, v_ref, gg_ref, dec_ref, y_ref, nre_ref, nim_ref):
    h = x_ref[...] * (1.0 + sc_ref[...]) + sh_ref[...]
    q, k, v, gg, u, gk = _mix_project(h, w_in, w_gk, b_gk)
    q_ref[...] = q
    k_ref[...] = k
    v_ref[...] = v
    gg_ref[...] = gg
    dec_ref[...] = jnp.exp(gk)
    bu_re, bu_im = _s5_input(u, bb_re, bb_im)
    a_re = ab_re_ref[...]
    a_im = ab_im_ref[...]
    h_re = hre_ref[...]
    h_im = him_ref[...]
    n_re = a_re * h_re - a_im * h_im + bu_re
    n_im = a_re * h_im + a_im * h_re + bu_im
    nre_ref[...] = n_re
    nim_ref[...] = n_im
    y_ref[...] = _s5_output(n_re, n_im, c_re, c_im) + d_ref[...] * u


def _gla_sample_step_body(s_ref, qc_ref, kc_ref, dc_ref, v_ref, so_ref, o_ref):
    for j in range(SAMPLE_BLOCK):
        for hh in range(GLA_HEADS):
            rows = slice(hh * GLA_DK, (hh + 1) * GLA_DK)
            lanes = slice(hh * GLA_DV, (hh + 1) * GLA_DV)
            S1 = dc_ref[rows, j:j + 1] * s_ref[j, hh] + kc_ref[rows, j:j + 1] * v_ref[j:j + 1, lanes]
            so_ref[j, hh] = S1
            o_ref[j:j + 1, lanes] = jnp.sum(qc_ref[rows, j:j + 1] * S1, axis=0, keepdims=True)


def _mix_sample_post_body(o_ref_in, gg_ref, y_ref, x_ref, gt_ref, norm_g, w_glu, b_glu, w_out, ln_g, ln_b,
                          o_ref):
    o_gla = _gla_finish(o_ref_in[...], gg_ref[...], norm_g)
    o_ref[...] = _mix_output(o_gla, y_ref[...], x_ref[...], gt_ref[...], w_glu, b_glu, w_out, ln_g, ln_b)


def _mix_sample(x, mods_s, layer, s_gla, s_re, s_im, m, s5, ln_g, ln_b):
    n = x.shape[0]
    ms = _mod_specs_sample(layer, 3, n)
    row_spec = _full_spec((n, D_MODEL), 1)

    def sds(w):
        return jax.ShapeDtypeStruct((n, w), F32)

    def fs(w):
        return _full_spec((n, w), 1)

    pre_w = [m["w_in"], m["w_gk"], m["b_gk"], s5["ab_re"], s5["ab_im"], s5["bb_re"], s5["bb_im"],
             s5["c_re"], s5["c_im"], s5["d"]]
    widths = [GLA_KEY, GLA_KEY, GLA_VAL, GLA_VAL, GLA_KEY, S5_WIDTH, S5_CH, S5_CH]
    q, k, v, gg, dec, y_s5, n_re, n_im = pl.pallas_call(
        _mix_sample_pre_body,
        grid=(1,),
        in_specs=[row_spec, ms[0], ms[1], fs(S5_CH), fs(S5_CH)] + [_full_spec(a.shape, 1) for a in pre_w],
        out_specs=[fs(w) for w in widths],
        out_shape=[sds(w) for w in widths],
        compiler_params=_params(("arbitrary",)),
        name="gla_s5_sample_pre",
    )(x, mods_s, mods_s, s_re.reshape(n, S5_CH), s_im.reshape(n, S5_CH), *pre_w)

    nblk = n // SAMPLE_BLOCK
    st_spec = pl.BlockSpec((SAMPLE_BLOCK, GLA_HEADS, GLA_DK, GLA_DV), lambda i: (i, 0, 0, 0))
    col_spec = pl.BlockSpec((None, GLA_KEY, SAMPLE_BLOCK), lambda i: (i, 0, 0))
    v_spec = pl.BlockSpec((SAMPLE_BLOCK, GLA_VAL), lambda i: (i, 0))
    s_new, o = pl.pallas_call(
        _gla_sample_step_body,
        grid=(nblk,),
        in_specs=[st_spec, col_spec, col_spec, col_spec, v_spec],
        out_specs=[st_spec, v_spec],
        out_shape=[jax.ShapeDtypeStruct(s_gla.shape, F32), sds(GLA_VAL)],
        compiler_params=_params(("arbitrary",)),
        name="gla_sample_step",
    )(s_gla, _to_columns(q), _to_columns(k), _to_columns(dec), v)

    post_w = [m["norm_g"], m["w_glu"], m["b_glu"], m["w_out"], ln_g, ln_b]
    out = pl.pallas_call(
        _mix_sample_post_body,
        grid=(1,),
        in_specs=[fs(GLA_VAL), fs(GLA_VAL), fs(S5_WIDTH), row_spec, ms[2]]
        + [_full_spec(a.shape, 1) for a in post_w],
        out_specs=row_spec,
        out_shape=sds(D_MODEL),
        compiler_params=_params(("arbitrary",)),
        name="gla_s5_sample_post",
    )(o, gg, y_s5, x, mods_s, *post_w)
    return (out, s_new, n_re.reshape(n, S5_GROUPS, S5_STATE), n_im.reshape(n, S5_GROUPS, S5_STATE))


def _pad_to(a, axis, size):
    pad = [(0, 0)] * a.ndim
    pad[axis] = (0, size - a.shape[axis])
    return jnp.pad(a, pad)


def _row(a):
    return a.reshape(1, -1).astype(F32)


def _prep_ffn(wg, wu, wd):
    return (jnp.transpose(wg, (0, 2, 1)).astype(BF16), jnp.transpose(wu, (0, 2, 1)).astype(BF16),
            wd.astype(BF16))


def _prep_mix(w_in, w_out, gla_w_gk, gla_b_gk, gla_norm_g, s5_w_glu, s5_b_glu):
    q, k, v, g, gk_low, u = jnp.split(
        w_in, (GLA_KEY, 2 * GLA_KEY, 2 * GLA_KEY + GLA_VAL, 2 * GLA_KEY + 2 * GLA_VAL,
               2 * GLA_KEY + 2 * GLA_VAL + GLA_GATE_RANK), axis=1)
    w_in_r = jnp.concatenate([q, k, v, g, u, _pad_to(gk_low, 1, LANES)], axis=1)
    return dict(
        w_in=w_in_r.astype(BF16), w_out=w_out.astype(BF16),
        w_gk=_pad_to(gla_w_gk, 0, LANES).astype(BF16), b_gk=_row(gla_b_gk), norm_g=_row(gla_norm_g),
        w_glu=s5_w_glu.astype(BF16), b_glu=_row(s5_b_glu),
    )


def _prep_rwkv(mu, w_r, w_k, w_v, w_o, w0, w1, w2, a0, a1, a2, g1, g2, k_k, k_a, r_k, lnx_g, lnx_b):
    return dict(
        mu=_pad_to(mu, 0, SUBLANES), wr=w_r.astype(BF16), wk=w_k.astype(BF16), wv=w_v.astype(BF16),
        wo=w_o.astype(BF16), w0=_row(w0),
        w1=_pad_to(w1, 1, LORA_PAD).astype(BF16), w2=_pad_to(w2, 0, LORA_PAD).astype(BF16),
        a0=_row(a0),
        a1=_pad_to(a1, 1, LORA_PAD).astype(BF16), a2=_pad_to(a2, 0, LORA_PAD).astype(BF16),
        g1=_pad_to(g1, 1, GATE_LORA_PAD).astype(BF16), g2=_pad_to(g2, 0, GATE_LORA_PAD).astype(BF16),
        k_k=_row(k_k), k_a=_row(k_a), r_k=_row(r_k), lnx_g=_row(lnx_g), lnx_b=_row(lnx_b),
    )


def kernel(x_prompt, x_sample, state_gla, state_s5_re, state_s5_im, state_rwkv_shift, state_rwkv_wkv,
           c_prompt, c_sample, ada_w, ada_b, ln_g, ln_b,
           ffn1_wg, ffn1_wu, ffn1_wd, ffn2_wg, ffn2_wu, ffn2_wd,
           w_in, w_out, gla_w_gk, gla_b_gk, gla_norm_g,
           s5_a_re, s5_a_im, s5_log_step, s5_b_re, s5_b_im, s5_c_re, s5_c_im, s5_d, s5_w_glu, s5_b_glu,
           rwkv_mu, rwkv_w_r, rwkv_w_k, rwkv_w_v, rwkv_w_o, rwkv_w0, rwkv_w1, rwkv_w2,
           rwkv_a0, rwkv_a1, rwkv_a2, rwkv_g1, rwkv_g2, rwkv_k_k, rwkv_k_a, rwkv_r_k,
           rwkv_lnx_g, rwkv_lnx_b):
    bp = x_prompt.shape[0]
    ns = x_sample.shape[0]

    mods_s, mods_p = _ada_mods(c_sample, c_prompt, ada_w, ada_b)
    mods_p = mods_p.reshape(DEPTH, bp, 1, N_MODS * D_MODEL)

    ffn1 = _prep_ffn(ffn1_wg, ffn1_wu, ffn1_wd)
    ffn2 = _prep_ffn(ffn2_wg, ffn2_wu, ffn2_wd)
    mix = _prep_mix(w_in, w_out, gla_w_gk, gla_b_gk, gla_norm_g, s5_w_glu, s5_b_glu)
    s5 = _s5_prepare(s5_a_re, s5_a_im, s5_log_step, s5_b_re, s5_b_im, s5_c_re, s5_c_im, s5_d)
    rwkv = _prep_rwkv(rwkv_mu, rwkv_w_r, rwkv_w_k, rwkv_w_v, rwkv_w_o, rwkv_w0, rwkv_w1, rwkv_w2,
                      rwkv_a0, rwkv_a1, rwkv_a2, rwkv_g1, rwkv_g2, rwkv_k_k, rwkv_k_a, rwkv_r_k,
                      rwkv_lnx_g, rwkv_lnx_b)

    def lnp(layer, idx):
        return _row(ln_g[layer, idx]), _row(ln_b[layer, idx])

    x = x_prompt
    xs = x_sample.reshape(ns, D_MODEL)
    x, xs = _ffn(x, xs, mods_p, mods_s, 0, 0, *ffn1, *lnp(0, 0))
    x, gla_p, s5_re_p, s5_im_p = _mix_prompt(x, mods_p, 0, mix, s5, *lnp(0, 1))
    xs, gla_s, s5_re_s, s5_im_s = _mix_sample(xs, mods_s, 0, state_gla, state_s5_re, state_s5_im,
                                              mix, s5, *lnp(0, 1))
    x, xs = _ffn(x, xs, mods_p, mods_s, 0, 6, *ffn2, *lnp(0, 2))
    x, xs = _ffn(x, xs, mods_p, mods_s, 1, 0, *ffn1, *lnp(1, 0))
    x, shift_p, wkv_p = _rwkv_prompt(x, mods_p, 1, rwkv, *lnp(1, 1))
    xs, shift_s, wkv_s = _rwkv_sample(xs, mods_s, 1, state_rwkv_shift, state_rwkv_wkv, rwkv, *lnp(1, 1))
    y_prompt, xs = _ffn(x, xs, mods_p, mods_s, 1, 6, *ffn2, *lnp(1, 2))
    y_sample = xs.reshape(ns, 1, D_MODEL)

    return (y_prompt, y_sample, gla_p, s5_re_p, s5_im_p, shift_p, wkv_p,
            gla_s, s5_re_s, s5_im_s, shift_s, wkv_s)
```
